```python
import math
import jax, jax.numpy as jnp
from jax import lax
import numpy as np

D_MODEL = 1024
BATCH = 8
SEQ = 4096
DEPTH = 1

DA_HEADS = 4
DA_HD = 64
DA_WIDTH = DA_HEADS * 2 * DA_HD
ML_HEADS = 4
ML_HD = 128
ML_WIDTH = ML_HEADS * ML_HD
CONV_W = 4
CHUNK = 64
Q_BLOCK = 128
N_GROUPS = 4
EXPERTS_PER_GROUP = 8
TOP_K = 2
D_EXPERT = 256
EPS = 1e-6
IN_SIZES = (DA_WIDTH, DA_WIDTH, DA_WIDTH,
            2 * ML_WIDTH, ML_WIDTH, ML_WIDTH,
            2 * ML_HEADS,
            D_MODEL, D_MODEL)
N_IN = sum(IN_SIZES)

kernel_name = 'hybrid_diffattn_mlstm_hmoe_adaln'


def rmsnorm(x, g):
    x32 = x.astype(jnp.float32)
    y = x32 * lax.rsqrt(jnp.mean(x32 * x32, axis=-1, keepdims=True) + EPS)
    return y.astype(x.dtype) * g


def alibi_slopes(n):
    return jnp.asarray(2.0 ** (-8.0 * np.arange(1, n + 1) / n), dtype=jnp.float32)


def causal_conv(x, w, b):
    K = w.shape[0]
    S = x.shape[1]
    xp = jnp.pad(x, ((0, 0), (K - 1, 0), (0, 0)))
    return sum(xp[:, j:j + S] * w[j] for j in range(K)) + b


def diff_attention(q, k, v, lam, lambda_init, head_gain):
    B, S = q.shape[:2]
    nb = S // Q_BLOCK
    slopes = alibi_slopes(DA_HEADS)
    scale = DA_HD ** -0.5
    qb = q.reshape(B, nb, Q_BLOCK, DA_HEADS, 2, DA_HD).swapaxes(0, 1)
    kpos = jnp.arange(S)

    def block(args):
        qi, bi = args
        s = jnp.einsum('bqhcd,bkhcd->bhcqk', qi, k).astype(jnp.float32) * scale
        qpos = bi * Q_BLOCK + jnp.arange(Q_BLOCK)
        dist = (qpos[:, None] - kpos[None, :]).astype(jnp.float32)
        s = s - (slopes[:, None, None] * dist)[None, :, None]
        s = jnp.where(dist >= 0, s, -jnp.inf)
        p = jax.nn.softmax(s, axis=-1)
        a = p[:, :, 0] - lam * p[:, :, 1]
        return jnp.einsum('bhqk,bkhe->bqhe', a.astype(v.dtype), v)

    o = lax.map(block, (qb, jnp.arange(nb)))
    o = o.swapaxes(0, 1).reshape(B, S, DA_HEADS, 2 * DA_HD)
    o = rmsnorm(o, head_gain) * (1.0 - lambda_init)
    return o.reshape(B, S, DA_WIDTH)


def mlstm(q, k, v, i_pre, f_pre):
    B, S, H, dh = q.shape
    nc = S // CHUNK
    f32 = jnp.float32
    k = k * dh ** -0.5
    logi = i_pre.astype(f32)
    logf = jax.nn.log_sigmoid(f_pre.astype(f32))

    def to_chunks(t):
        return t.reshape(B, nc, CHUNK, *t.shape[2:]).swapaxes(0, 1)

    xs = (to_chunks(q), to_chunks(k), to_chunks(v), to_chunks(logi), to_chunks(logf))
    causal = jnp.tril(jnp.ones((CHUNK, CHUNK), dtype=bool))

    def step(carry, inp):
        C, n, m = carry
        qc, kc, vc, li, lf = inp
        qc = qc.astype(f32); kc = kc.astype(f32); vc = vc.astype(f32)
        bh = jnp.cumsum(lf, axis=1).swapaxes(1, 2)
        ih = li.swapaxes(1, 2)
        g = bh[..., -1]
        Dm = bh[..., :, None] - bh[..., None, :] + ih[..., None, :]
        Dm = jnp.where(causal, Dm, -jnp.inf)
        inter = bh + m[..., None]
        mj = jnp.maximum(inter, Dm.max(-1))
        w_intra = jnp.exp(Dm - mj[..., None])
        w_inter = jnp.exp(inter - mj)
        qk = jnp.einsum('bjhd,bshd->bhjs', qc, kc) * w_intra
        num = (jnp.einsum('bhjs,bshe->bjhe', qk, vc)
               + jnp.einsum('bhed,bjhd->bjhe', C, qc) * w_inter.swapaxes(1, 2)[..., None])
        den = qk.sum(-1) + w_inter * jnp.einsum('bhd,bjhd->bhj', n, qc)
        denom = jnp.maximum(jnp.abs(den), jnp.exp(-mj)).swapaxes(1, 2)
        h = num / denom[..., None]
        a = g[..., None] - bh + ih
        m_new = jnp.maximum(g + m, a.max(-1))
        wa = jnp.exp(a - m_new[..., None])
        decay = jnp.exp(g + m - m_new)
        C_new = decay[..., None, None] * C + jnp.einsum('bhs,bshe,bshd->bhed', wa, vc, kc)
        n_new = decay[..., None] * n + jnp.einsum('bhs,bshd->bhd', wa, kc)
        return (C_new, n_new, m_new), h

    init = (jnp.zeros((B, H, dh, dh), f32), jnp.zeros((B, H, dh), f32), jnp.zeros((B, H), f32))
    _, hs = lax.scan(step, init, xs)
    return hs.swapaxes(0, 1).reshape(B, S, H, dh).astype(q.dtype)


def hier_moe(h, w_rg, b_rg, w_re, b_re, w_e1, w_e3, w_e2):
    B, S, D = h.shape
    f32 = jnp.float32
    t = h.reshape(-1, D)
    T = t.shape[0]
    g_logits = (t @ w_rg + b_rg).astype(f32)
    g_prob = jax.nn.softmax(g_logits, axis=-1)
    g_sel = jnp.argmax(g_logits, axis=-1)
    p_grp = jnp.take_along_axis(g_prob, g_sel[:, None], axis=1)[:, 0]
    e_logits = (t @ w_re + b_re).astype(f32).reshape(T, N_GROUPS, EXPERTS_PER_GROUP)
    e_sel = jnp.take_along_axis(e_logits, g_sel[:, None, None], axis=1)[:, 0]
    e_prob = jax.nn.softmax(e_sel, axis=-1)
    topv, topi = lax.top_k(e_prob, TOP_K)
    topv = topv / topv.sum(-1, keepdims=True)
    w_e = jnp.einsum('tke,tk->te', jax.nn.one_hot(topi, EXPERTS_PER_GROUP, dtype=f32), topv)
    gate = (jax.nn.one_hot(g_sel, N_GROUPS, dtype=f32)[:, :, None]
            * w_e[:, None, :] * p_grp[:, None, None]).astype(t.dtype)
    y = jnp.zeros_like(t)
    for gi in range(N_GROUPS):
        a = jnp.einsum('td,edf->tef', t, w_e1[gi])
        b = jnp.einsum('td,edf->tef', t, w_e3[gi])
        hid = jax.nn.silu(a) * b * gate[:, gi, :, None]
        y = y + jnp.einsum('tef,efd->td', hid, w_e2[gi])
    return y.reshape(B, S, D)


def setup_inputs(seed: int = 0) -> dict:
    key = jax.random.key(seed)
    ks = jax.random.split(key, 32)
    L, D, G, E, F = DEPTH, D_MODEL, N_GROUPS, EXPERTS_PER_GROUP, D_EXPERT
    nrm = lambda k, shape, s: jax.random.normal(k, shape, jnp.float32) * s
    f_bias = jnp.broadcast_to(jnp.linspace(3.0, 6.0, ML_HEADS, dtype=jnp.float32), (L, ML_HEADS))
    b_if = jnp.concatenate([nrm(ks[5], (L, ML_HEADS), 0.1),
                            f_bias + nrm(ks[6], (L, ML_HEADS), 0.1)], axis=-1)
    return {
        'x': nrm(ks[0], (BATCH, SEQ, D), 1.0),
        'c': nrm(ks[1], (BATCH, D), 1.0),
        'w_ada': nrm(ks[2], (L, D, 6 * D), 0.5 * D ** -0.5),
        'b_ada': nrm(ks[3], (L, 6 * D), 0.01),
        'norm1': 1.0 + nrm(ks[4], (L, D), 0.02),
        'w_in': nrm(ks[7], (L, D, N_IN), D ** -0.5),
        'b_if': b_if,
        'conv_w': nrm(ks[8], (L, CONV_W, 2 * ML_WIDTH), CONV_W ** -0.5),
        'conv_b': nrm(ks[9], (L, 2 * ML_WIDTH), 0.01),
        'lam_q1': nrm(ks[10], (L, DA_HD), 0.1),
        'lam_k1': nrm(ks[11], (L, DA_HD), 0.1),
        'lam_q2': nrm(ks[12], (L, DA_HD), 0.1),
        'lam_k2': nrm(ks[13], (L, DA_HD), 0.1),
        'diff_norm': 1.0 + nrm(ks[14], (L, 2 * DA_HD), 0.02),
        'mlstm_norm': 1.0 + nrm(ks[15], (L, ML_HD), 0.02),
        'w_br_a': nrm(ks[16], (L, DA_WIDTH, D), DA_WIDTH ** -0.5),
        'w_br_m': nrm(ks[17], (L, ML_WIDTH, D), ML_WIDTH ** -0.5),
        'w_out': nrm(ks[18], (L, D, D), D ** -0.5),
        'norm2': 1.0 + nrm(ks[19], (L, D), 0.02),
        'w_rg': nrm(ks[20], (L, D, G), D ** -0.5),
        'b_rg': nrm(ks[21], (L, G), 0.01),
        'w_re': nrm(ks[22], (L, D, G * E), D ** -0.5),
        'b_re': nrm(ks[23], (L, G * E), 0.01),
        'w_e1': nrm(ks[24], (L, G, E, D, F), D ** -0.5),
        'w_e3': nrm(ks[25], (L, G, E, D, F), D ** -0.5),
        'w_e2': nrm(ks[26], (L, G, E, F, D), F ** -0.5),
        'norm_f': 1.0 + nrm(ks[27], (D,), 0.02),
    }


def reference(x, c, w_ada, b_ada, norm1, w_in, b_if, conv_w, conv_b, lam_q1, lam_k1, lam_q2, lam_k2,
              diff_norm, mlstm_norm, w_br_a, w_br_m, w_out, norm2, w_rg, b_rg, w_re, b_re,
              w_e1, w_e3, w_e2, norm_f):
    B, S, D = x.shape
    cs = jax.nn.silu(c)
    split_idx = list(np.cumsum(IN_SIZES)[:-1])
    for l in range(DEPTH):
        mod = (cs @ w_ada[l] + b_ada[l])[:, None, :]
        sh1, sc1, gt1, sh2, sc2, gt2 = jnp.split(mod, 6, axis=-1)
        h = rmsnorm(x, norm1[l]) * (1.0 + sc1) + sh1
        proj = h @ w_in[l]
        da_q, da_k, da_v, ml_qk, ml_v, ml_o, ml_if, g_a, g_m = jnp.split(proj, split_idx, axis=-1)
        lambda_init = 0.8 - 0.6 * math.exp(-0.3 * l)
        lam = (jnp.exp(jnp.sum(lam_q1[l] * lam_k1[l])) - jnp.exp(jnp.sum(lam_q2[l] * lam_k2[l]))
               + lambda_init)
        y_a = diff_attention(da_q.reshape(B, S, DA_HEADS, 2, DA_HD),
                             da_k.reshape(B, S, DA_HEADS, 2, DA_HD),
                             da_v.reshape(B, S, DA_HEADS, 2 * DA_HD),
                             lam, lambda_init, diff_norm[l])
        qk = jax.nn.silu(causal_conv(ml_qk, conv_w[l], conv_b[l]))
        m_q, m_k = jnp.split(qk, 2, axis=-1)
        gates = ml_if + b_if[l]
        h_t = mlstm(m_q.reshape(B, S, ML_HEADS, ML_HD), m_k.reshape(B, S, ML_HEADS, ML_HD),
                    ml_v.reshape(B, S, ML_HEADS, ML_HD), gates[..., :ML_HEADS], gates[..., ML_HEADS:])
        o_gate = jax.nn.sigmoid(ml_o).reshape(B, S, ML_HEADS, ML_HD)
        y_m = rmsnorm(o_gate * h_t, mlstm_norm[l]).reshape(B, S, ML_WIDTH)
        merged = jax.nn.sigmoid(g_a) * (y_a @ w_br_a[l]) + jax.nn.sigmoid(g_m) * (y_m @ w_br_m[l])
        x = x + gt1 * (merged @ w_out[l])
        h2 = rmsnorm(x, norm2[l]) * (1.0 + sc2) + sh2
        x = x + gt2 * hier_moe(h2, w_rg[l], b_rg[l], w_re[l], b_re[l], w_e1[l], w_e3[l], w_e2[l])
    return rmsnorm(x, norm_f)
```

```python
import functools
import math

import jax
import jax.numpy as jnp
import numpy as np
from jax import lax
from jax.experimental import pallas as pl
from jax.experimental.pallas import tpu as pltpu

F32 = jnp.float32
BF16 = jnp.bfloat16

D_MODEL = 1024
DA_HEADS = 4
DA_HD = 64
DA_WIDTH = DA_HEADS * 2 * DA_HD
ML_HEADS = 4
ML_HD = 128
ML_WIDTH = ML_HEADS * ML_HD
CONV_W = 4
N_GROUPS = 4
EXPERTS_PER_GROUP = 8
N_EXPERTS = N_GROUPS * EXPERTS_PER_GROUP
D_EXPERT = 256
EPS = 1e-6
LAMBDA_INIT = 0.8 - 0.6 * math.exp(-0.3 * 0)

LANES = 128
SUBLANES = 8
SMEM_BLOCK_1D = 1024
NEG = -1e30
VMEM_LIMIT = 56 * 1024 * 1024

COL_MLQK = 0
COL_GA = 1024
COL_GM = 2048
COL_DAQ = 3072
COL_DAK = 3584
COL_DAV = 4096
COL_MLV = 4608
COL_MLO = 5120
COL_IF = 5632
N_PROJ = COL_IF + LANES
PROJ_TN = 1920
assert N_PROJ % PROJ_TN == 0

EXPERT_ROWS = 256
PROJ_TM = 1024
ATTN_T = 512
MLSTM_L = 256
POST_TM = 512


def _cparams(sem):
    return pltpu.CompilerParams(dimension_semantics=sem, vmem_limit_bytes=VMEM_LIMIT)


def _sigmoid(x):
    return 1.0 / (1.0 + jnp.exp(-x))


def _ada_kernel(c_ref, w_ref, b_ref, o_ref):
    c = c_ref[...]
    cs = c * _sigmoid(c)
    o_ref[...] = jnp.dot(cs, w_ref[...], precision=lax.Precision.HIGHEST,
                         preferred_element_type=F32) + b_ref[...]


def _ada(c, w, b):
    B, D = c.shape
    N = w.shape[1]
    tn = 1536
    return pl.pallas_call(
        _ada_kernel,
        grid=(N // tn,),
        in_specs=[pl.BlockSpec((B, D), lambda j: (0, 0)),
                  pl.BlockSpec((D, tn), lambda j: (0, j)),
                  pl.BlockSpec((1, tn), lambda j: (0, j))],
        out_specs=pl.BlockSpec((B, tn), lambda j: (0, j)),
        out_shape=jax.ShapeDtypeStruct((B, N), F32),
        compiler_params=_cparams(("arbitrary",)),
    )(c, w, b.reshape(1, N))


def _proj_kernel(x_ref, mod_ref, g_ref, bif_ref, w_ref, p_ref, gate_ref, h_scr, *, nj):
    j = pl.program_id(2)

    @pl.when(j == 0)
    def _():
        x = x_ref[0]
        y = x * lax.rsqrt(jnp.mean(x * x, axis=-1, keepdims=True) + EPS) * g_ref[...]
        h = y * (1.0 + mod_ref[0, 1:2, :]) + mod_ref[0, 0:1, :]
        h_scr[...] = h.astype(BF16)

    acc = jnp.dot(h_scr[...], w_ref[...], preferred_element_type=F32)
    p_ref[0] = acc.astype(BF16)

    @pl.when(j == nj - 1)
    def _():
        gate_ref[0] = acc[:, PROJ_TN - LANES:] + bif_ref[...]


def _proj(x, mod, norm1, bif, w_bf16, tm):
    B, S, D = x.shape
    nj = N_PROJ // PROJ_TN
    return pl.pallas_call(
        functools.partial(_proj_kernel, nj=nj),
        grid=(B, S // tm, nj),
        in_specs=[pl.BlockSpec((1, tm, D), lambda b, i, j: (b, i, 0)),
                  pl.BlockSpec((1, 6, D), lambda b, i, j: (b, 0, 0)),
                  pl.BlockSpec((1, D), lambda b, i, j: (0, 0)),
                  pl.BlockSpec((1, LANES), lambda b, i, j: (0, 0)),
                  pl.BlockSpec((D, PROJ_TN), lambda b, i, j: (0, j))],
        out_specs=[pl.BlockSpec((1, tm, PROJ_TN), lambda b, i, j: (b, i, j)),
                   pl.BlockSpec((1, tm, LANES), lambda b, i, j: (b, i, 0))],
        out_shape=[jax.ShapeDtypeStruct((B, S, N_PROJ), BF16),
                   jax.ShapeDtypeStruct((B, S, LANES), F32)],
        scratch_shapes=[pltpu.VMEM((tm, D), BF16)],
        compiler_params=_cparams(("arbitrary", "arbitrary", "arbitrary")),
    )(x, mod, norm1.reshape(1, D), bif, w_bf16)


def _attn_kernel(slope_ref, q_ref, k_ref, v_ref, lam_ref, dn_ref, o_ref, m_scr, l_scr, acc_scr, *, t):
    h = pl.program_id(1)
    i = pl.program_id(2)
    kv = pl.program_id(3)

    @pl.when(kv == 0)
    def _():
        m_scr[...] = jnp.full(m_scr.shape, NEG, F32)
        l_scr[...] = jnp.zeros(l_scr.shape, F32)
        acc_scr[...] = jnp.zeros(acc_scr.shape, F32)

    @pl.when(kv <= i)
    def _():
        q = q_ref[0]
        k = k_ref[0]
        v = v_ref[0]
        lane = lax.broadcasted_iota(jnp.int32, (1, LANES), 1)
        row = lax.broadcasted_iota(jnp.int32, (t, t), 0)
        col = lax.broadcasted_iota(jnp.int32, (t, t), 1)
        dist = (row - col + (i - kv) * t).astype(F32)
        bias = slope_ref[h] * dist
        valid = dist >= 0.0
        for c in range(2):
            qc = jnp.where((lane < DA_HD) if c == 0 else (lane >= DA_HD), q, jnp.zeros_like(q))
            s = lax.dot_general(qc, k, (((1,), (1,)), ((), ())), preferred_element_type=F32)
            s = jnp.where(valid, s - bias, NEG)
            m_prev = m_scr[c]
            m_new = jnp.maximum(m_prev, jnp.max(s, axis=-1, keepdims=True))
            alpha = jnp.exp(m_prev - m_new)
            p = jnp.exp(s - m_new)
            l_scr[c] = alpha * l_scr[c] + jnp.sum(p, axis=-1, keepdims=True)
            acc_scr[c] = alpha * acc_scr[c] + jnp.dot(p.astype(BF16), v, preferred_element_type=F32)
            m_scr[c] = m_new

    @pl.when(kv == i)
    def _():
        lv = lam_ref[...]
        lam = (jnp.exp(jnp.sum(lv[0:1] * lv[1:2], axis=-1, keepdims=True))
               - jnp.exp(jnp.sum(lv[2:3] * lv[3:4], axis=-1, keepdims=True)) + LAMBDA_INIT)
        o = acc_scr[0] / l_scr[0] - lam * (acc_scr[1] / l_scr[1])
        o = o * lax.rsqrt(jnp.mean(o * o, axis=-1, keepdims=True) + EPS)
        o_ref[0] = (o * dn_ref[...] * (1.0 - LAMBDA_INIT)).astype(BF16)


def _attn(p, lamv, diff_norm, t):
    B, S, _ = p.shape
    n = S // t
    slopes = jnp.asarray(2.0 ** (-8.0 * np.arange(1, DA_HEADS + 1) / DA_HEADS), dtype=F32)
    qb, kb, vb = COL_DAQ // LANES, COL_DAK // LANES, COL_DAV // LANES
    grid_spec = pltpu.PrefetchScalarGridSpec(
        num_scalar_prefetch=1,
        grid=(B, DA_HEADS, n, n),
        in_specs=[pl.BlockSpec((1, t, LANES), lambda b, h, i, kv, sl: (b, i, qb + h)),
                  pl.BlockSpec((1, t, LANES), lambda b, h, i, kv, sl: (b, jnp.minimum(kv, i), kb + h)),
                  pl.BlockSpec((1, t, LANES), lambda b, h, i, kv, sl: (b, jnp.minimum(kv, i), vb + h)),
                  pl.BlockSpec((4, DA_HD), lambda b, h, i, kv, sl: (0, 0)),
                  pl.BlockSpec((1, 2 * DA_HD), lambda b, h, i, kv, sl: (0, 0))],
        out_specs=pl.BlockSpec((1, t, LANES), lambda b, h, i, kv, sl: (b, i, h)),
        scratch_shapes=[pltpu.VMEM((2, t, 1), F32), pltpu.VMEM((2, t, 1), F32),
                        pltpu.VMEM((2, t, LANES), F32)],
    )
    return pl.pallas_call(
        functools.partial(_attn_kernel, t=t),
        grid_spec=grid_spec,
        out_shape=jax.ShapeDtypeStruct((B, S, DA_WIDTH), BF16),
        compiler_params=_cparams(("arbitrary",) * 4),
    )(slopes, p, p, p, lamv, diff_norm.reshape(1, 2 * DA_HD))


def _mlstm_kernel(qk_ref, v_ref, o_ref, g_ref, cw_ref, cb_ref, nw_ref, y_ref,
                  ext_scr, c_scr, n_scr, m_scr, *, L):
    i = pl.program_id(1)
    hist = SUBLANES

    @pl.when(i == 0)
    def _():
        ext_scr[0:hist, :] = jnp.zeros((hist, 2 * ML_WIDTH), F32)
        c_scr[...] = jnp.zeros(c_scr.shape, F32)
        n_scr[...] = jnp.zeros(n_scr.shape, F32)
        m_scr[...] = jnp.zeros(m_scr.shape, F32)

    raw = qk_ref[0].astype(F32)
    ext_scr[hist:hist + L, :] = raw
    conv = cb_ref[...] + jnp.zeros((L, 2 * ML_WIDTH), F32)
    for j in range(CONV_W):
        off = hist - (CONV_W - 1) + j
        conv = conv + ext_scr[off:off + L, :] * cw_ref[j:j + 1, :]
    ext_scr[0:hist, :] = raw[L - hist:L, :]
    qkc = conv * _sigmoid(conv)

    gts = g_ref[0]
    fpre = pltpu.roll(gts, LANES - ML_HEADS, axis=1)
    lf = jnp.minimum(fpre, 0.0) - jnp.log(1.0 + jnp.exp(-jnp.abs(fpre)))
    row = lax.broadcasted_iota(jnp.int32, (L, L), 0)
    col = lax.broadcasted_iota(jnp.int32, (L, L), 1)
    causal = col <= row
    bcum = jnp.dot(causal.astype(F32), lf, precision=lax.Precision.HIGHEST, preferred_element_type=F32)
    r = gts - bcum
    rt = r.T
    m_all = m_scr[...]
    lane = lax.broadcasted_iota(jnp.int32, (1, LANES), 1)
    m_next = m_all
    v_all = v_ref[0]
    o_all = o_ref[0]
    for h in range(ML_HEADS):
        hs = slice(h * ML_HD, (h + 1) * ML_HD)
        bcol = bcum[:, h:h + 1]
        rcol = r[:, h:h + 1]
        rrow = rt[h:h + 1, :]
        g = bcum[L - 1:L, h:h + 1]
        mh = m_all[:, h:h + 1]
        dm = jnp.where(causal, bcol + rrow, NEG)
        inter = bcol + mh
        mj = jnp.maximum(inter, jnp.max(dm, axis=-1, keepdims=True))
        w_intra = jnp.exp(dm - mj)
        w_inter = jnp.exp(inter - mj)
        qh = qkc[:, hs]
        kh = qkc[:, ML_WIDTH + h * ML_HD:ML_WIDTH + (h + 1) * ML_HD] * (ML_HD ** -0.5)
        vh = v_all[:, hs]
        qb = qh.astype(BF16)
        kb = kh.astype(BF16)
        s = lax.dot_general(qb, kb, (((1,), (1,)), ((), ())), preferred_element_type=F32) * w_intra
        c_old = c_scr[h]
        n_old = n_scr[h:h + 1, :]
        num = (jnp.dot(s.astype(BF16), vh, preferred_element_type=F32)
               + lax.dot_general(qb, c_old.astype(BF16), (((1,), (1,)), ((), ())),
                                 preferred_element_type=F32) * w_inter)
        den = (jnp.sum(s, axis=-1, keepdims=True)
               + w_inter * jnp.sum(qh * n_old, axis=-1, keepdims=True))
        denom = jnp.maximum(jnp.abs(den), jnp.exp(-mj))
        ht = num / denom
        a_col = g + rcol
        m_new = jnp.maximum(g + mh, jnp.max(a_col, axis=0, keepdims=True))
        wa = jnp.exp(a_col - m_new)
        decay = jnp.exp(g + mh - m_new)
        vw_t = (vh.astype(F32) * wa).T.astype(BF16)
        c_scr[h] = decay * c_old + jnp.dot(vw_t, kb, preferred_element_type=F32)
        n_scr[h:h + 1, :] = decay * n_old + jnp.sum(kh * wa, axis=0, keepdims=True)
        m_next = jnp.where(lane == h, m_new, m_next)
        z = _sigmoid(o_all[:, hs].astype(F32)) * ht
        z = z * lax.rsqrt(jnp.mean(z * z, axis=-1, keepdims=True) + EPS) * nw_ref[...]
        y_ref[0, :, hs] = z.astype(BF16)
    m_scr[...] = m_next


def _mlstm(p, gates, conv_w, conv_b, mlstm_norm, L):
    B, S, _ = p.shape
    return pl.pallas_call(
        functools.partial(_mlstm_kernel, L=L),
        grid=(B, S // L),
        in_specs=[pl.BlockSpec((1, L, 2 * ML_WIDTH), lambda b, i: (b, i, COL_MLQK // (2 * ML_WIDTH))),
                  pl.BlockSpec((1, L, ML_WIDTH), lambda b, i: (b, i, COL_MLV // ML_WIDTH)),
                  pl.BlockSpec((1, L, ML_WIDTH), lambda b, i: (b, i, COL_MLO // ML_WIDTH)),
                  pl.BlockSpec((1, L, LANES), lambda b, i: (b, i, 0)),
                  pl.BlockSpec((CONV_W, 2 * ML_WIDTH), lambda b, i: (0, 0)),
                  pl.BlockSpec((1, 2 * ML_WIDTH), lambda b, i: (0, 0)),
                  pl.BlockSpec((1, ML_HD), lambda b, i: (0, 0))],
        out_specs=pl.BlockSpec((1, L, ML_WIDTH), lambda b, i: (b, i, 0)),
        out_shape=jax.ShapeDtypeStruct((B, S, ML_WIDTH), BF16),
        scratch_shapes=[pltpu.VMEM((SUBLANES + L, 2 * ML_WIDTH), F32),
                        pltpu.VMEM((ML_HEADS, ML_HD, ML_HD), F32),
                        pltpu.VMEM((SUBLANES, ML_HD), F32),
                        pltpu.VMEM((1, LANES), F32)],
        compiler_params=_cparams(("arbitrary", "arbitrary")),
    )(p, p, p, gates, conv_w, conv_b.reshape(1, -1), mlstm_norm.reshape(1, ML_HD))


def _post_kernel(ya_ref, ym_ref, ga_ref, gm_ref, x_ref, mod_ref, wa_ref, wm_ref, wo_ref, n2_ref,
                 wr_ref, br_ref, x1_ref, h2_ref, ri_ref, rw_ref, cnt_ref, run_scr, *, tm):
    first = jnp.logical_and(pl.program_id(0) == 0, pl.program_id(1) == 0)

    @pl.when(first)
    def _():
        run_scr[...] = jnp.zeros(run_scr.shape, F32)

    a = jnp.dot(ya_ref[0], wa_ref[...], preferred_element_type=F32)
    m = jnp.dot(ym_ref[0], wm_ref[...], preferred_element_type=F32)
    merged = _sigmoid(ga_ref[0].astype(F32)) * a + _sigmoid(gm_ref[0].astype(F32)) * m
    o = jnp.dot(merged.astype(BF16), wo_ref[...], preferred_element_type=F32)
    x1 = x_ref[0] + mod_ref[0, 2:3, :] * o
    x1_ref[0] = x1
    h2 = x1 * lax.rsqrt(jnp.mean(x1 * x1, axis=-1, keepdims=True) + EPS) * n2_ref[...]
    h2 = h2 * (1.0 + mod_ref[0, 4:5, :]) + mod_ref[0, 3:4, :]
    h2_ref[0] = h2

    logits = jnp.dot(h2, wr_ref[...], precision=lax.Precision.HIGHEST,
                     preferred_element_type=F32) + br_ref[...]
    lane = lax.broadcasted_iota(jnp.int32, (tm, LANES), 1)
    big = jnp.int32(4 * LANES)
    gl = jnp.where(lane < N_GROUPS, logits, NEG)
    gmax = jnp.max(gl, axis=-1, keepdims=True)
    gsel = jnp.min(jnp.where(gl == gmax, lane, big), axis=-1, keepdims=True)
    pgrp = 1.0 / jnp.sum(jnp.exp(gl - gmax), axis=-1, keepdims=True)
    lo = N_GROUPS + EXPERTS_PER_GROUP * gsel
    el = jnp.where(jnp.logical_and(lane >= lo, lane < lo + EXPERTS_PER_GROUP), logits, NEG)
    e1 = jnp.max(el, axis=-1, keepdims=True)
    i1 = jnp.min(jnp.where(el == e1, lane, big), axis=-1, keepdims=True)
    el2 = jnp.where(lane == i1, NEG, el)
    e2 = jnp.max(el2, axis=-1, keepdims=True)
    i2 = jnp.min(jnp.where(el2 == e2, lane, big), axis=-1, keepdims=True)
    tt = jnp.exp(e2 - e1)
    w1 = pgrp / (1.0 + tt)
    w2 = pgrp * tt / (1.0 + tt)
    eid1 = i1 - N_GROUPS
    eid2 = i2 - N_GROUPS

    oh1 = jnp.where(lane == eid1, 1.0, 0.0).astype(F32)
    oh2 = jnp.where(lane == eid2, 1.0, 0.0).astype(F32)
    cat = jnp.concatenate([oh1, oh2], axis=1).astype(BF16)
    row = lax.broadcasted_iota(jnp.int32, (tm, tm), 0)
    col = lax.broadcasted_iota(jnp.int32, (tm, tm), 1)
    before = jnp.where(col < row, 1.0, 0.0).astype(BF16)
    earlier = jnp.dot(before, cat, preferred_element_type=F32)
    c1 = jnp.sum(oh1, axis=0, keepdims=True)
    c2 = jnp.sum(oh2, axis=0, keepdims=True)
    run = run_scr[...]
    rank1 = jnp.sum((earlier[:, :LANES] + run) * oh1, axis=-1, keepdims=True)
    rank2 = jnp.sum((earlier[:, LANES:] + run + c1) * oh2, axis=-1, keepdims=True)
    run_new = run + c1 + c2
    run_scr[...] = run_new
    cnt_ref[...] = jnp.broadcast_to(run_new, cnt_ref.shape)
    ri = jnp.where(lane == 0, eid1,
                   jnp.where(lane == 1, eid2,
                             jnp.where(lane == 2, rank1.astype(jnp.int32),
                                       jnp.where(lane == 3, rank2.astype(jnp.int32), 0))))
    ri_ref[0] = ri
    rw_ref[0] = jnp.where(lane == 0, w1, jnp.where(lane == 1, w2, 0.0))


def _post(ya, ym, p, x, mod, wa, wm, wo, norm2, wr, br, tm):
    B, S, D = x.shape
    tok = lambda b, i: (b, i, 0)
    const = lambda b, i: (0, 0)
    return pl.pallas_call(
        functools.partial(_post_kernel, tm=tm),
        grid=(B, S // tm),
        in_specs=[pl.BlockSpec((1, tm, DA_WIDTH), tok),
                  pl.BlockSpec((1, tm, ML_WIDTH), tok),
                  pl.BlockSpec((1, tm, D), lambda b, i: (b, i, COL_GA // D_MODEL)),
                  pl.BlockSpec((1, tm, D), lambda b, i: (b, i, COL_GM // D_MODEL)),
                  pl.BlockSpec((1, tm, D), tok),
                  pl.BlockSpec((1, 6, D), lambda b, i: (b, 0, 0)),
                  pl.BlockSpec((DA_WIDTH, D), const),
                  pl.BlockSpec((ML_WIDTH, D), const),
                  pl.BlockSpec((D, D), const),
                  pl.BlockSpec((1, D), const),
                  pl.BlockSpec((D, LANES), const),
                  pl.BlockSpec((1, LANES), const)],
        out_specs=[pl.BlockSpec((1, tm, D), tok),
                   pl.BlockSpec((1, tm, D), tok),
                   pl.BlockSpec((1, tm, LANES), tok),
                   pl.BlockSpec((1, tm, LANES), tok),
                   pl.BlockSpec((SUBLANES, LANES), const)],
        out_shape=[jax.ShapeDtypeStruct((B, S, D), F32),
                   jax.ShapeDtypeStruct((B, S, D), F32),
                   jax.ShapeDtypeStruct((B, S, LANES), jnp.int32),
                   jax.ShapeDtypeStruct((B, S, LANES), F32),
                   jax.ShapeDtypeStruct((SUBLANES, LANES), F32)],
        scratch_shapes=[pltpu.VMEM((1, LANES), F32)],
        compiler_params=_cparams(("arbitrary", "arbitrary")),
    )(ya, ym, p, p, x, mod, wa, wm, wo, norm2.reshape(1, D), wr, br)


def _row_copy(src_hbm, src_row, dst_hbm, dst_row, sem):
    return pltpu.make_async_copy(src_hbm.at[pl.ds(src_row, 1)], dst_hbm.at[pl.ds(dst_row, 1)], sem)


def _dispatch_kernel(ends_ref, pc_ref, dest_ref, h2_hbm, xs_hbm, zbuf, sem, zsem, *, td):
    i = pl.program_id(0)

    @pl.when(i == 0)
    def _():
        zbuf[...] = jnp.zeros(zbuf.shape, F32)
        for e in range(N_EXPERTS):
            @pl.when(pc_ref[e] > 0)
            def _():
                start = pl.multiple_of(ends_ref[e] - EXPERT_ROWS, EXPERT_ROWS)
                cp = pltpu.make_async_copy(zbuf, xs_hbm.at[pl.ds(start, EXPERT_ROWS)], zsem)
                cp.start()
                cp.wait()

    def issue(t, carry):
        tok = i * td + t
        for s in range(2):
            _row_copy(h2_hbm, tok, xs_hbm, dest_ref[2 * t + s], sem).start()
        return carry

    lax.fori_loop(0, td, issue, 0)

    def drain(t, carry):
        for s in range(2):
            _row_copy(h2_hbm, 0, xs_hbm, 0, sem).wait()
        return carry

    lax.fori_loop(0, td, drain, 0)


def _dispatch(ends, pc, dest_flat, h2, n_rows, td):
    T, D = h2.shape
    grid_spec = pltpu.PrefetchScalarGridSpec(
        num_scalar_prefetch=2,
        grid=(T // td,),
        in_specs=[pl.BlockSpec((2 * td,), lambda i, e, c: (i,), memory_space=pltpu.SMEM),
                  pl.BlockSpec(memory_space=pl.ANY)],
        out_specs=pl.BlockSpec(memory_space=pl.ANY),
        scratch_shapes=[pltpu.VMEM((EXPERT_ROWS, D), F32),
                        pltpu.SemaphoreType.DMA(()), pltpu.SemaphoreType.DMA(())],
    )
    return pl.pallas_call(
        functools.partial(_dispatch_kernel, td=td),
        grid_spec=grid_spec,
        out_shape=jax.ShapeDtypeStruct((n_rows, D), F32),
        compiler_params=_cparams(("arbitrary",)),
    )(ends, pc, dest_flat, h2)


def _experts_kernel(te_ref, nt_ref, xs_ref, w1_ref, w3_ref, w2_ref, ys_ref):
    i = pl.program_id(0)

    @pl.when(i < nt_ref[0])
    def _():
        x = xs_ref[...].astype(BF16)
        a = jnp.dot(x, w1_ref[0], preferred_element_type=F32)
        b = jnp.dot(x, w3_ref[0], preferred_element_type=F32)
        hid = (a * _sigmoid(a) * b).astype(BF16)
        ys_ref[...] = jnp.dot(hid, w2_ref[0], preferred_element_type=F32)


def _experts(tile_e, n_tiles, xs, w1, w3, w2):
    n_rows, D = xs.shape
    nt = n_rows // EXPERT_ROWS
    rows = lambda i, te, n: (jnp.minimum(i, jnp.maximum(n[0] - 1, 0)), 0)
    wsel = lambda i, te, n: (te[i], 0, 0)
    grid_spec = pltpu.PrefetchScalarGridSpec(
        num_scalar_prefetch=2,
        grid=(nt,),
        in_specs=[pl.BlockSpec((EXPERT_ROWS, D), rows),
                  pl.BlockSpec((1, D, D_EXPERT), wsel),
                  pl.BlockSpec((1, D, D_EXPERT), wsel),
                  pl.BlockSpec((1, D_EXPERT, D), wsel)],
        out_specs=pl.BlockSpec((EXPERT_ROWS, D), rows),
    )
    return pl.pallas_call(
        _experts_kernel,
        grid_spec=grid_spec,
        out_shape=jax.ShapeDtypeStruct((n_rows, D), F32),
        compiler_params=_cparams(("arbitrary",)),
    )(tile_e, n_tiles, xs, w1, w3, w2)


def _combine_kernel(dest_ref, ys_hbm, x1_ref, rw_ref, mod_ref, nf_ref, o_ref, ybuf, sem, *, tc):
    def issue(t, carry):
        for s in range(2):
            pltpu.make_async_copy(ys_hbm.at[pl.ds(dest_ref[2 * t + s], 1)],
                                  ybuf.at[s, pl.ds(t, 1)], sem).start()
        return carry

    lax.fori_loop(0, tc, issue, 0)

    def drain(t, carry):
        for s in range(2):
            pltpu.make_async_copy(ys_hbm.at[pl.ds(0, 1)], ybuf.at[s, pl.ds(0, 1)], sem).wait()
        return carry

    lax.fori_loop(0, tc, drain, 0)

    rw = rw_ref[0]
    y = rw[:, 0:1] * ybuf[0] + rw[:, 1:2] * ybuf[1]
    xo = x1_ref[0] + mod_ref[0, 5:6, :] * y
    o_ref[0] = xo * lax.rsqrt(jnp.mean(xo * xo, axis=-1, keepdims=True) + EPS) * nf_ref[...]


def _combine(dest_flat, ys, x1, rw, mod, norm_f, tc):
    B, S, D = x1.shape
    n = S // tc
    tok = lambda b, i: (b, i, 0)
    return pl.pallas_call(
        functools.partial(_combine_kernel, tc=tc),
        grid=(B, n),
        in_specs=[pl.BlockSpec((2 * tc,), lambda b, i: (b * n + i,), memory_space=pltpu.SMEM),
                  pl.BlockSpec(memory_space=pl.ANY),
                  pl.BlockSpec((1, tc, D), tok),
                  pl.BlockSpec((1, tc, LANES), tok),
                  pl.BlockSpec((1, 6, D), lambda b, i: (b, 0, 0)),
                  pl.BlockSpec((1, D), lambda b, i: (0, 0))],
        out_specs=pl.BlockSpec((1, tc, D), tok),
        out_shape=jax.ShapeDtypeStruct((B, S, D), F32),
        scratch_shapes=[pltpu.VMEM((2, tc, D), F32), pltpu.SemaphoreType.DMA(())],
        compiler_params=_cparams(("arbitrary", "arbitrary")),
    )(dest_flat, ys, x1, rw, mod, norm_f.reshape(1, D))


def _pick(n, pref):
    t = min(n, pref)
    assert n % t == 0, (n, pref)
    return t


def kernel(x, c, w_ada, b_ada, norm1, w_in, b_if, conv_w, conv_b, lam_q1, lam_k1, lam_q2, lam_k2,
           diff_norm, mlstm_norm, w_br_a, w_br_m, w_out, norm2, w_rg, b_rg, w_re, b_re,
           w_e1, w_e3, w_e2, norm_f):
    B, S, D = x.shape
    assert D == D_MODEL and w_ada.shape[0] == 1
    T = B * S
    l = 0

    mod = _ada(c, w_ada[l], b_ada[l]).reshape(B, 6, D)

    w = w_in[l]
    o_q, o_k, o_v, o_qk, o_mv, o_mo, o_if, o_ga, o_gm = np.cumsum((0,) + (
        DA_WIDTH, DA_WIDTH, DA_WIDTH, 2 * ML_WIDTH, ML_WIDTH, ML_WIDTH, 2 * ML_HEADS, D_MODEL))
    w_perm = jnp.concatenate([
        w[:, o_qk:o_qk + 2 * ML_WIDTH], w[:, o_ga:o_ga + D], w[:, o_gm:o_gm + D],
        w[:, o_q:o_q + DA_WIDTH] * (DA_HD ** -0.5), w[:, o_k:o_k + DA_WIDTH], w[:, o_v:o_v + DA_WIDTH],
        w[:, o_mv:o_mv + ML_WIDTH], w[:, o_mo:o_mo + ML_WIDTH], w[:, o_if:o_if + 2 * ML_HEADS],
        jnp.zeros((D, LANES - 2 * ML_HEADS), F32)], axis=1).astype(BF16)
    bif = jnp.concatenate([b_if[l], jnp.zeros((LANES - 2 * ML_HEADS,), F32)]).reshape(1, LANES)

    p, gates = _proj(x, mod, norm1[l], bif, w_perm, _pick(S, PROJ_TM))

    lamv = jnp.stack([lam_q1[l], lam_k1[l], lam_q2[l], lam_k2[l]])
    ya = _attn(p, lamv, diff_norm[l], _pick(S, ATTN_T))
    ym = _mlstm(p, gates, conv_w[l], conv_b[l], mlstm_norm[l], _pick(S, MLSTM_L))

    wr = jnp.concatenate([w_rg[l], w_re[l], jnp.zeros((D, LANES - N_GROUPS - N_EXPERTS), F32)], axis=1)
    br = jnp.concatenate([b_rg[l], b_re[l], jnp.zeros((LANES - N_GROUPS - N_EXPERTS,), F32)]).reshape(1, LANES)
    x1, h2, ri, rw, cnt = _post(ya, ym, p, x, mod, w_br_a[l].astype(BF16), w_br_m[l].astype(BF16),
                                w_out[l].astype(BF16), norm2[l], wr, br, _pick(S, POST_TM))

    counts = cnt[0, :N_EXPERTS].astype(jnp.int32)
    pc = ((counts + EXPERT_ROWS - 1) // EXPERT_ROWS) * EXPERT_ROWS
    ends = jnp.cumsum(pc)
    offs = ends - pc
    ri = ri.reshape(T, LANES)
    eid = ri[:, 0:2]
    rank = ri[:, 2:4]
    dest = rank + jnp.sum(jnp.where(eid[..., None] == jnp.arange(N_EXPERTS), offs, 0), axis=-1)
    dest_flat = dest.reshape(2 * T).astype(jnp.int32)
    n_rows = 2 * T + N_EXPERTS * EXPERT_ROWS
    n_tiles = n_rows // EXPERT_ROWS
    tile_e = jnp.sum(jnp.arange(n_tiles)[:, None] * EXPERT_ROWS >= ends[None, :], axis=1)
    tile_e = jnp.minimum(tile_e, N_EXPERTS - 1).astype(jnp.int32)
    used_tiles = (ends[-1:] // EXPERT_ROWS).astype(jnp.int32)

    xs = _dispatch(ends.astype(jnp.int32), pc.astype(jnp.int32), dest_flat, h2.reshape(T, D), n_rows,
                   _pick(T, SMEM_BLOCK_1D))
    w1 = w_e1[l].reshape(N_EXPERTS, D, D_EXPERT).astype(BF16)
    w3 = w_e3[l].reshape(N_EXPERTS, D, D_EXPERT).astype(BF16)
    w2 = w_e2[l].reshape(N_EXPERTS, D_EXPERT, D).astype(BF16)
    ys = _experts(tile_e, used_tiles, xs, w1, w3, w2)
    return _combine(dest_flat, ys, x1, rw, mod, norm_f, _pick(S, SMEM_BLOCK_1D // 2))
```

```python
import functools
import math

import jax
import jax.numpy as jnp
import numpy as np
from jax import lax
from jax.experimental import pallas as pl
from jax.experimental.pallas import tpu as pltpu

F32 = jnp.float32
BF16 = jnp.bfloat16

D_MODEL = 1024
DA_HEADS = 4
DA_HD = 64
DA_WIDTH = DA_HEADS * 2 * DA_HD
ML_HEADS = 4
ML_HD = 128
ML_WIDTH = ML_HEADS * ML_HD
CONV_W = 4
N_GROUPS = 4
EXPERTS_PER_GROUP = 8
N_EXPERTS = N_GROUPS * EXPERTS_PER_GROUP
D_EXPERT = 256
EPS = 1e-6
LAMBDA_INIT = 0.8 - 0.6 * math.exp(-0.3 * 0)

LANES = 128
SUBLANES = 8
SMEM_BLOCK_1D = 1024
NEG = -1e30
VMEM_LIMIT = 56 * 1024 * 1024

COL_MLQK = 0
COL_GA = 1024
COL_GM = 2048
COL_DAQ = 3072
COL_DAK = 3584
COL_DAV = 4096
COL_MLV = 4608
COL_MLO = 5120
COL_IF = 5632
N_PROJ = COL_IF + LANES
PROJ_TN = 1920
assert N_PROJ % PROJ_TN == 0

EXPERT_ROWS = 256
PROJ_TM = 1024
ATTN_TQ = 1024
ATTN_TK = 512
MLSTM_L = 256
POST_TM = 512


def _cparams(sem):
    return pltpu.CompilerParams(dimension_semantics=sem, vmem_limit_bytes=VMEM_LIMIT)


def _sigmoid(x):
    return 1.0 / (1.0 + jnp.exp(-x))


def _ada_kernel(c_ref, w_ref, b_ref, o_ref):
    c = c_ref[...]
    cs = c * _sigmoid(c)
    o_ref[...] = jnp.dot(cs, w_ref[...], precision=lax.Precision.HIGHEST,
                         preferred_element_type=F32) + b_ref[...]


def _ada(c, w, b):
    B, D = c.shape
    N = w.shape[1]
    tn = 1536
    return pl.pallas_call(
        _ada_kernel,
        grid=(N // tn,),
        in_specs=[pl.BlockSpec((B, D), lambda j: (0, 0)),
                  pl.BlockSpec((D, tn), lambda j: (0, j)),
                  pl.BlockSpec((1, tn), lambda j: (0, j))],
        out_specs=pl.BlockSpec((B, tn), lambda j: (0, j)),
        out_shape=jax.ShapeDtypeStruct((B, N), F32),
        compiler_params=_cparams(("arbitrary",)),
    )(c, w, b.reshape(1, N))


def _proj_kernel(x_ref, mod_ref, g_ref, bif_ref, w_ref, p_ref, gate_ref, h_scr, *, nj):
    j = pl.program_id(2)

    @pl.when(j == 0)
    def _():
        x = x_ref[0]
        y = x * lax.rsqrt(jnp.mean(x * x, axis=-1, keepdims=True) + EPS) * g_ref[...]
        h = y * (1.0 + mod_ref[0, 1:2, :]) + mod_ref[0, 0:1, :]
        h_scr[...] = h.astype(BF16)

    acc = jnp.dot(h_scr[...], w_ref[...], preferred_element_type=F32)
    p_ref[0] = acc.astype(BF16)

    @pl.when(j == nj - 1)
    def _():
        gate_ref[0] = acc[:, PROJ_TN - LANES:] + bif_ref[...]


def _proj(x, mod, norm1, bif, w_bf16, tm):
    B, S, D = x.shape
    nj = N_PROJ // PROJ_TN
    return pl.pallas_call(
        functools.partial(_proj_kernel, nj=nj),
        grid=(B, S // tm, nj),
        in_specs=[pl.BlockSpec((1, tm, D), lambda b, i, j: (b, i, 0)),
                  pl.BlockSpec((1, 6, D), lambda b, i, j: (b, 0, 0)),
                  pl.BlockSpec((1, D), lambda b, i, j: (0, 0)),
                  pl.BlockSpec((1, LANES), lambda b, i, j: (0, 0)),
                  pl.BlockSpec((D, PROJ_TN), lambda b, i, j: (0, j))],
        out_specs=[pl.BlockSpec((1, tm, PROJ_TN), lambda b, i, j: (b, i, j)),
                   pl.BlockSpec((1, tm, LANES), lambda b, i, j: (b, i, 0))],
        out_shape=[jax.ShapeDtypeStruct((B, S, N_PROJ), BF16),
                   jax.ShapeDtypeStruct((B, S, LANES), F32)],
        scratch_shapes=[pltpu.VMEM((tm, D), BF16)],
        compiler_params=_cparams(("arbitrary", "arbitrary", "arbitrary")),
    )(x, mod, norm1.reshape(1, D), bif, w_bf16)


def _attn_block(q_ref, k_ref, v_ref, eq_ref, ek_ref, m_scr, l_scr, acc_scr, off, cb, masked, tq, tk):
    q = q_ref[0]
    k = k_ref[0]
    v = v_ref[0]
    eq = eq_ref[0]
    ek = ek_ref[0]
    lane = lax.broadcasted_iota(jnp.int32, (1, LANES), 1)
    if masked:
        key = lax.broadcasted_iota(jnp.int32, (tk, tq), 0)
        qry = lax.broadcasted_iota(jnp.int32, (tk, tq), 1)
        valid = (key - qry) <= off
    scores = []
    for c in range(2):
        sel = (lane < DA_HD) if c == 0 else (lane >= DA_HD)
        s = lax.dot_general(jnp.where(sel, k, ek), jnp.where(sel, q, eq), (((1,), (1,)), ((), ())),
                            preferred_element_type=F32)
        scores.append(jnp.where(valid, s, NEG) if masked else s)
    for c in range(2):
        s = scores[c]
        m_prev = m_scr[c]
        m_new = jnp.maximum(m_prev, jnp.max(s, axis=0, keepdims=True) - cb)
        alpha = jnp.exp(m_prev - m_new)
        p = jnp.exp(s - (m_new + cb))
        l_scr[c] = alpha * l_scr[c] + jnp.sum(p, axis=0, keepdims=True)
        pv = lax.dot_general(v, p.astype(BF16), (((0,), (0,)), ((), ())), preferred_element_type=F32)
        acc_scr[c] = alpha * acc_scr[c] + pv
        m_scr[c] = m_new


def _attn_kernel(it_ref, jt_ref, slope_ref, q_ref, k_ref, v_ref, eq_ref, ek_ref, lam_ref, dn_ref, o_ref,
                 m_scr, l_scr, acc_scr, *, tq, tk):
    h = pl.program_id(1)
    step = pl.program_id(2)
    i = it_ref[step]
    j = jt_ref[step]
    ratio = tq // tk

    @pl.when(j == 0)
    def _():
        m_scr[...] = jnp.full(m_scr.shape, NEG, F32)
        l_scr[...] = jnp.zeros(l_scr.shape, F32)
        acc_scr[...] = jnp.zeros(acc_scr.shape, F32)

    off = i * tq - j * tk
    cb = slope_ref[h] * off.astype(F32)
    args = (q_ref, k_ref, v_ref, eq_ref, ek_ref, m_scr, l_scr, acc_scr, off, cb)

    @pl.when(j < i * ratio)
    def _():
        _attn_block(*args, masked=False, tq=tq, tk=tk)

    @pl.when(j >= i * ratio)
    def _():
        _attn_block(*args, masked=True, tq=tq, tk=tk)

    @pl.when(j == (i + 1) * ratio - 1)
    def _():
        lv = lam_ref[...]
        lam = (jnp.exp(jnp.sum(lv[0:1] * lv[1:2], axis=-1, keepdims=True))
               - jnp.exp(jnp.sum(lv[2:3] * lv[3:4], axis=-1, keepdims=True)) + LAMBDA_INIT)
        o = acc_scr[0] / l_scr[0] - lam * (acc_scr[1] / l_scr[1])
        o = o * lax.rsqrt(jnp.mean(o * o, axis=0, keepdims=True) + EPS)
        o_ref[0] = (o.T * dn_ref[...] * (1.0 - LAMBDA_INIT)).astype(BF16)


def _alibi_columns(n, slopes, q_side):
    assert n <= 256 * 256 and all(math.log2(s).is_integer() for s in slopes)
    pos = np.arange(n)
    lo, hi = (pos % 256).astype(np.float64), (pos // 256 * 256).astype(np.float64)
    out = np.zeros((len(slopes), n, LANES), np.float64)
    for h, s in enumerate(slopes):
        cols = (-s * lo, -s * hi, np.ones(n), np.ones(n)) if q_side else (np.ones(n), np.ones(n), s * lo, s * hi)
        for base in (0, DA_HD):
            for c, v in enumerate(cols):
                out[h, :, base + c] = v
    return jnp.asarray(out, dtype=BF16)


def _attn(p, lamv, diff_norm, tq, tk):
    B, S, _ = p.shape
    nq, ratio = S // tq, tq // tk
    slopes = [2.0 ** (-8.0 * (h + 1) / DA_HEADS) for h in range(DA_HEADS)]
    steps = [(i, j) for i in range(nq) for j in range((i + 1) * ratio)]
    it = jnp.asarray([s[0] for s in steps], jnp.int32)
    jt = jnp.asarray([s[1] for s in steps], jnp.int32)
    qb, kb, vb = COL_DAQ // LANES, COL_DAK // LANES, COL_DAV // LANES
    grid_spec = pltpu.PrefetchScalarGridSpec(
        num_scalar_prefetch=3,
        grid=(B, DA_HEADS, len(steps)),
        in_specs=[pl.BlockSpec((1, tq, LANES), lambda b, h, s, it, jt, sl: (b, it[s], qb + h)),
                  pl.BlockSpec((1, tk, LANES), lambda b, h, s, it, jt, sl: (b, jt[s], kb + h)),
                  pl.BlockSpec((1, tk, LANES), lambda b, h, s, it, jt, sl: (b, jt[s], vb + h)),
                  pl.BlockSpec((1, tq, LANES), lambda b, h, s, it, jt, sl: (h, 0, 0)),
                  pl.BlockSpec((1, tk, LANES), lambda b, h, s, it, jt, sl: (h, 0, 0)),
                  pl.BlockSpec((4, DA_HD), lambda b, h, s, it, jt, sl: (0, 0)),
                  pl.BlockSpec((1, 2 * DA_HD), lambda b, h, s, it, jt, sl: (0, 0))],
        out_specs=pl.BlockSpec((1, tq, LANES), lambda b, h, s, it, jt, sl: (b, it[s], h)),
        scratch_shapes=[pltpu.VMEM((2, 1, tq), F32), pltpu.VMEM((2, 1, tq), F32),
                        pltpu.VMEM((2, 2 * DA_HD, tq), F32)],
    )
    return pl.pallas_call(
        functools.partial(_attn_kernel, tq=tq, tk=tk),
        grid_spec=grid_spec,
        out_shape=jax.ShapeDtypeStruct((B, S, DA_WIDTH), BF16),
        compiler_params=_cparams(("arbitrary",) * 3),
    )(it, jt, jnp.asarray(slopes, F32), p, p, p, _alibi_columns(tq, slopes, True),
      _alibi_columns(tk, slopes, False), lamv, diff_norm.reshape(1, 2 * DA_HD))


def _mlstm_kernel(qk_ref, v_ref, o_ref, g_ref, cw_ref, cb_ref, nw_ref, y_ref,
                  ext_scr, c_scr, n_scr, m_scr, *, L):
    i = pl.program_id(1)
    hist = SUBLANES

    @pl.when(i == 0)
    def _():
        ext_scr[0:hist, :] = jnp.zeros((hist, 2 * ML_WIDTH), F32)
        c_scr[...] = jnp.zeros(c_scr.shape, F32)
        n_scr[...] = jnp.zeros(n_scr.shape, F32)
        m_scr[...] = jnp.zeros(m_scr.shape, F32)

    raw = qk_ref[0].astype(F32)
    ext_scr[hist:hist + L, :] = raw
    conv = cb_ref[...] + jnp.zeros((L, 2 * ML_WIDTH), F32)
    for j in range(CONV_W):
        off = hist - (CONV_W - 1) + j
        conv = conv + ext_scr[off:off + L, :] * cw_ref[j:j + 1, :]
    ext_scr[0:hist, :] = raw[L - hist:L, :]
    qkc = conv * _sigmoid(conv)

    gts = g_ref[0]
    fpre = pltpu.roll(gts, LANES - ML_HEADS, axis=1)
    lf = jnp.minimum(fpre, 0.0) - jnp.log(1.0 + jnp.exp(-jnp.abs(fpre)))
    row = lax.broadcasted_iota(jnp.int32, (L, L), 0)
    col = lax.broadcasted_iota(jnp.int32, (L, L), 1)
    causal = col <= row
    bcum = jnp.dot(causal.astype(F32), lf, precision=lax.Precision.HIGHEST, preferred_element_type=F32)
    r = gts - bcum
    rt = r.T
    m_all = m_scr[...]
    lane = lax.broadcasted_iota(jnp.int32, (1, LANES), 1)
    m_next = m_all
    v_all = v_ref[0]
    o_all = o_ref[0]
    for h in range(ML_HEADS):
        hs = slice(h * ML_HD, (h + 1) * ML_HD)
        bcol = bcum[:, h:h + 1]
        rcol = r[:, h:h + 1]
        rrow = rt[h:h + 1, :]
        g = bcum[L - 1:L, h:h + 1]
        mh = m_all[:, h:h + 1]
        dm = jnp.where(causal, bcol + rrow, NEG)
        inter = bcol + mh
        mj = jnp.maximum(inter, jnp.max(dm, axis=-1, keepdims=True))
        w_intra = jnp.exp(dm - mj)
        w_inter = jnp.exp(inter - mj)
        qh = qkc[:, hs]
        kh = qkc[:, ML_WIDTH + h * ML_HD:ML_WIDTH + (h + 1) * ML_HD] * (ML_HD ** -0.5)
        vh = v_all[:, hs]
        qb = qh.astype(BF16)
        kb = kh.astype(BF16)
        s = lax.dot_general(qb, kb, (((1,), (1,)), ((), ())), preferred_element_type=F32) * w_intra
        c_old = c_scr[h]
        n_old = n_scr[h:h + 1, :]
        num = (jnp.dot(s.astype(BF16), vh, preferred_element_type=F32)
               + lax.dot_general(qb, c_old.astype(BF16), (((1,), (1,)), ((), ())),
                                 preferred_element_type=F32) * w_inter)
        den = (jnp.sum(s, axis=-1, keepdims=True)
               + w_inter * jnp.sum(qh * n_old, axis=-1, keepdims=True))
        denom = jnp.maximum(jnp.abs(den), jnp.exp(-mj))
        ht = num / denom
        a_col = g + rcol
        m_new = jnp.maximum(g + mh, jnp.max(a_col, axis=0, keepdims=True))
        wa = jnp.exp(a_col - m_new)
        decay = jnp.exp(g + mh - m_new)
        vw_t = (vh.astype(F32) * wa).T.astype(BF16)
        c_scr[h] = decay * c_old + jnp.dot(vw_t, kb, preferred_element_type=F32)
        n_scr[h:h + 1, :] = decay * n_old + jnp.sum(kh * wa, axis=0, keepdims=True)
        m_next = jnp.where(lane == h, m_new, m_next)
        z = _sigmoid(o_all[:, hs].astype(F32)) * ht
        z = z * lax.rsqrt(jnp.mean(z * z, axis=-1, keepdims=True) + EPS) * nw_ref[...]
        y_ref[0, :, hs] = z.astype(BF16)
    m_scr[...] = m_next


def _mlstm(p, gates, conv_w, conv_b, mlstm_norm, L):
    B, S, _ = p.shape
    return pl.pallas_call(
        functools.partial(_mlstm_kernel, L=L),
        grid=(B, S // L),
        in_specs=[pl.BlockSpec((1, L, 2 * ML_WIDTH), lambda b, i: (b, i, COL_MLQK // (2 * ML_WIDTH))),
                  pl.BlockSpec((1, L, ML_WIDTH), lambda b, i: (b, i, COL_MLV // ML_WIDTH)),
                  pl.BlockSpec((1, L, ML_WIDTH), lambda b, i: (b, i, COL_MLO // ML_WIDTH)),
                  pl.BlockSpec((1, L, LANES), lambda b, i: (b, i, 0)),
                  pl.BlockSpec((CONV_W, 2 * ML_WIDTH), lambda b, i: (0, 0)),
                  pl.BlockSpec((1, 2 * ML_WIDTH), lambda b, i: (0, 0)),
                  pl.BlockSpec((1, ML_HD), lambda b, i: (0, 0))],
        out_specs=pl.BlockSpec((1, L, ML_WIDTH), lambda b, i: (b, i, 0)),
        out_shape=jax.ShapeDtypeStruct((B, S, ML_WIDTH), BF16),
        scratch_shapes=[pltpu.VMEM((SUBLANES + L, 2 * ML_WIDTH), F32),
                        pltpu.VMEM((ML_HEADS, ML_HD, ML_HD), F32),
                        pltpu.VMEM((SUBLANES, ML_HD), F32),
                        pltpu.VMEM((1, LANES), F32)],
        compiler_params=_cparams(("arbitrary", "arbitrary")),
    )(p, p, p, gates, conv_w, conv_b.reshape(1, -1), mlstm_norm.reshape(1, ML_HD))


def _post_kernel(ya_ref, ym_ref, ga_ref, gm_ref, x_ref, mod_ref, wa_ref, wm_ref, wo_ref, n2_ref,
                 wr_ref, br_ref, x1_ref, h2_ref, ri_ref, rw_ref, cnt_ref, run_scr, *, tm):
    first = jnp.logical_and(pl.program_id(0) == 0, pl.program_id(1) == 0)

    @pl.when(first)
    def _():
        run_scr[...] = jnp.zeros(run_scr.shape, F32)

    a = jnp.dot(ya_ref[0], wa_ref[...], preferred_element_type=F32)
    m = jnp.dot(ym_ref[0], wm_ref[...], preferred_element_type=F32)
    merged = _sigmoid(ga_ref[0].astype(F32)) * a + _sigmoid(gm_ref[0].astype(F32)) * m
    o = jnp.dot(merged.astype(BF16), wo_ref[...], preferred_element_type=F32)
    x1 = x_ref[0] + mod_ref[0, 2:3, :] * o
    x1_ref[0] = x1
    h2 = x1 * lax.rsqrt(jnp.mean(x1 * x1, axis=-1, keepdims=True) + EPS) * n2_ref[...]
    h2 = h2 * (1.0 + mod_ref[0, 4:5, :]) + mod_ref[0, 3:4, :]
    h2_ref[0] = h2

    logits = jnp.dot(h2, wr_ref[...], precision=lax.Precision.HIGHEST,
                     preferred_element_type=F32) + br_ref[...]
    lane = lax.broadcasted_iota(jnp.int32, (tm, LANES), 1)
    big = jnp.int32(4 * LANES)
    gl = jnp.where(lane < N_GROUPS, logits, NEG)
    gmax = jnp.max(gl, axis=-1, keepdims=True)
    gsel = jnp.min(jnp.where(gl == gmax, lane, big), axis=-1, keepdims=True)
    pgrp = 1.0 / jnp.sum(jnp.exp(gl - gmax), axis=-1, keepdims=True)
    lo = N_GROUPS + EXPERTS_PER_GROUP * gsel
    el = jnp.where(jnp.logical_and(lane >= lo, lane < lo + EXPERTS_PER_GROUP), logits, NEG)
    e1 = jnp.max(el, axis=-1, keepdims=True)
    i1 = jnp.min(jnp.where(el == e1, lane, big), axis=-1, keepdims=True)
    el2 = jnp.where(lane == i1, NEG, el)
    e2 = jnp.max(el2, axis=-1, keepdims=True)
    i2 = jnp.min(jnp.where(el2 == e2, lane, big), axis=-1, keepdims=True)
    tt = jnp.exp(e2 - e1)
    w1 = pgrp / (1.0 + tt)
    w2 = pgrp * tt / (1.0 + tt)
    eid1 = i1 - N_GROUPS
    eid2 = i2 - N_GROUPS

    oh1 = jnp.where(lane == eid1, 1.0, 0.0).astype(F32)
    oh2 = jnp.where(lane == eid2, 1.0, 0.0).astype(F32)
    cat = jnp.concatenate([oh1, oh2], axis=1).astype(BF16)
    row = lax.broadcasted_iota(jnp.int32, (tm, tm), 0)
    col = lax.broadcasted_iota(jnp.int32, (tm, tm), 1)
    before = jnp.where(col < row, 1.0, 0.0).astype(BF16)
    earlier = jnp.dot(before, cat, preferred_element_type=F32)
    c1 = jnp.sum(oh1, axis=0, keepdims=True)
    c2 = jnp.sum(oh2, axis=0, keepdims=True)
    run = run_scr[...]
    rank1 = jnp.sum((earlier[:, :LANES] + run) * oh1, axis=-1, keepdims=True)
    rank2 = jnp.sum((earlier[:, LANES:] + run + c1) * oh2, axis=-1, keepdims=True)
    run_new = run + c1 + c2
    run_scr[...] = run_new
    cnt_ref[...] = jnp.broadcast_to(run_new, cnt_ref.shape)
    ri = jnp.where(lane == 0, eid1,
                   jnp.where(lane == 1, eid2,
                             jnp.where(lane == 2, rank1.astype(jnp.int32),
                                       jnp.where(lane == 3, rank2.astype(jnp.int32), 0))))
    ri_ref[0] = ri
    rw_ref[0] = jnp.where(lane == 0, w1, jnp.where(lane == 1, w2, 0.0))


def _post(ya, ym, p, x, mod, wa, wm, wo, norm2, wr, br, tm):
    B, S, D = x.shape
    tok = lambda b, i: (b, i, 0)
    const = lambda b, i: (0, 0)
    return pl.pallas_call(
        functools.partial(_post_kernel, tm=tm),
        grid=(B, S // tm),
        in_specs=[pl.BlockSpec((1, tm, DA_WIDTH), tok),
                  pl.BlockSpec((1, tm, ML_WIDTH), tok),
                  pl.BlockSpec((1, tm, D), lambda b, i: (b, i, COL_GA // D_MODEL)),
                  pl.BlockSpec((1, tm, D), lambda b, i: (b, i, COL_GM // D_MODEL)),
                  pl.BlockSpec((1, tm, D), tok),
                  pl.BlockSpec((1, 6, D), lambda b, i: (b, 0, 0)),
                  pl.BlockSpec((DA_WIDTH, D), const),
                  pl.BlockSpec((ML_WIDTH, D), const),
                  pl.BlockSpec((D, D), const),
                  pl.BlockSpec((1, D), const),
                  pl.BlockSpec((D, LANES), const),
                  pl.BlockSpec((1, LANES), const)],
        out_specs=[pl.BlockSpec((1, tm, D), tok),
                   pl.BlockSpec((1, tm, D), tok),
                   pl.BlockSpec((1, tm, LANES), tok),
                   pl.BlockSpec((1, tm, LANES), tok),
                   pl.BlockSpec((SUBLANES, LANES), const)],
        out_shape=[jax.ShapeDtypeStruct((B, S, D), F32),
                   jax.ShapeDtypeStruct((B, S, D), F32),
                   jax.ShapeDtypeStruct((B, S, LANES), jnp.int32),
                   jax.ShapeDtypeStruct((B, S, LANES), F32),
                   jax.ShapeDtypeStruct((SUBLANES, LANES), F32)],
        scratch_shapes=[pltpu.VMEM((1, LANES), F32)],
        compiler_params=_cparams(("arbitrary", "arbitrary")),
    )(ya, ym, p, p, x, mod, wa, wm, wo, norm2.reshape(1, D), wr, br)


DMA_UNROLL = 8


def _dispatch_kernel(ends_ref, pc_ref, dest_ref, h2_ref, xs_hbm, zbuf, sem, zsem, *, td):
    i = pl.program_id(0)

    @pl.when(i == 0)
    def _():
        zbuf[...] = jnp.zeros(zbuf.shape, F32)
        for e in range(N_EXPERTS):
            @pl.when(pc_ref[e] > 0)
            def _():
                start = pl.multiple_of(ends_ref[e] - EXPERT_ROWS, EXPERT_ROWS)
                cp = pltpu.make_async_copy(zbuf, xs_hbm.at[pl.ds(start, EXPERT_ROWS)], zsem)
                cp.start()
                cp.wait()

        def fill(tile, carry):
            start = pl.multiple_of(tile * EXPERT_ROWS, EXPERT_ROWS)
            cp = pltpu.make_async_copy(zbuf, xs_hbm.at[pl.ds(start, EXPERT_ROWS)], zsem)
            cp.start()
            cp.wait()
            return carry

        lax.fori_loop(ends_ref[N_EXPERTS - 1] // EXPERT_ROWS, xs_hbm.shape[0] // EXPERT_ROWS, fill, 0)

    def issue(t, carry):
        for s in range(2):
            pltpu.make_async_copy(h2_ref.at[pl.ds(t, 1)], xs_hbm.at[pl.ds(dest_ref[2 * t + s], 1)], sem).start()
        return carry

    lax.fori_loop(0, td, issue, 0, unroll=DMA_UNROLL)
    for s in range(2):
        pltpu.make_async_copy(h2_ref, xs_hbm.at[pl.ds(0, td)], sem).wait()


def _dispatch(ends, pc, dest_flat, h2, n_rows, td):
    T, D = h2.shape
    grid_spec = pltpu.PrefetchScalarGridSpec(
        num_scalar_prefetch=2,
        grid=(T // td,),
        in_specs=[pl.BlockSpec((2 * td,), lambda i, e, c: (i,), memory_space=pltpu.SMEM),
                  pl.BlockSpec((td, D), lambda i, e, c: (i, 0))],
        out_specs=pl.BlockSpec(memory_space=pl.ANY),
        scratch_shapes=[pltpu.VMEM((EXPERT_ROWS, D), F32),
                        pltpu.SemaphoreType.DMA(()), pltpu.SemaphoreType.DMA(())],
    )
    return pl.pallas_call(
        functools.partial(_dispatch_kernel, td=td),
        grid_spec=grid_spec,
        out_shape=jax.ShapeDtypeStruct((n_rows, D), F32),
        compiler_params=_cparams(("arbitrary",)),
    )(ends, pc, dest_flat, h2)


def _experts_kernel(te_ref, nt_ref, xs_ref, w1_ref, w3_ref, w2_ref, ys_ref):
    i = pl.program_id(0)

    @pl.when(i < nt_ref[0])
    def _():
        x = xs_ref[...].astype(BF16)
        a = jnp.dot(x, w1_ref[0], preferred_element_type=F32)
        b = jnp.dot(x, w3_ref[0], preferred_element_type=F32)
        hid = (a * _sigmoid(a) * b).astype(BF16)
        ys_ref[...] = jnp.dot(hid, w2_ref[0], preferred_element_type=F32)

    @pl.when(i >= nt_ref[0])
    def _():
        ys_ref[...] = jnp.zeros(ys_ref.shape, F32)


def _experts(tile_e, n_tiles, xs, w1, w3, w2):
    n_rows, D = xs.shape
    nt = n_rows // EXPERT_ROWS
    rows = lambda i, te, n: (jnp.minimum(i, jnp.maximum(n[0] - 1, 0)), 0)
    wsel = lambda i, te, n: (te[i], 0, 0)
    grid_spec = pltpu.PrefetchScalarGridSpec(
        num_scalar_prefetch=2,
        grid=(nt,),
        in_specs=[pl.BlockSpec((EXPERT_ROWS, D), rows),
                  pl.BlockSpec((1, D, D_EXPERT), wsel),
                  pl.BlockSpec((1, D, D_EXPERT), wsel),
                  pl.BlockSpec((1, D_EXPERT, D), wsel)],
        out_specs=pl.BlockSpec((EXPERT_ROWS, D), lambda i, te, n: (i, 0)),
    )
    return pl.pallas_call(
        _experts_kernel,
        grid_spec=grid_spec,
        out_shape=jax.ShapeDtypeStruct((n_rows, D), F32),
        compiler_params=_cparams(("arbitrary",)),
    )(tile_e, n_tiles, xs, w1, w3, w2)


def _combine_kernel(dcur_ref, dnxt_ref, ys_hbm, x1_ref, rw_ref, mod_ref, nf_ref, o_ref, ybuf, sem, *, tc, nsteps):
    g = pl.program_id(0)
    slot = g % 2

    def gather(dest_ref, sl):
        def issue(t, carry):
            for s in range(2):
                pltpu.make_async_copy(ys_hbm.at[pl.ds(dest_ref[2 * t + s], 1)],
                                      ybuf.at[sl, s, pl.ds(t, 1)], sem.at[sl]).start()
            return carry

        lax.fori_loop(0, tc, issue, 0, unroll=DMA_UNROLL)

    @pl.when(g == 0)
    def _():
        gather(dcur_ref, 0)

    @pl.when(g + 1 < nsteps)
    def _():
        gather(dnxt_ref, 1 - slot)

    for s in range(2):
        pltpu.make_async_copy(ys_hbm.at[pl.ds(0, tc)], ybuf.at[slot, s], sem.at[slot]).wait()

    rw = rw_ref[0]
    y = rw[:, 0:1] * ybuf[slot, 0] + rw[:, 1:2] * ybuf[slot, 1]
    xo = x1_ref[0] + mod_ref[0, 5:6, :] * y
    o_ref[0] = xo * lax.rsqrt(jnp.mean(xo * xo, axis=-1, keepdims=True) + EPS) * nf_ref[...]


def _combine(dest_flat, ys, x1, rw, mod, norm_f, tc):
    B, S, D = x1.shape
    n = S // tc
    nsteps = B * n
    tok = lambda g: (g // n, g % n, 0)
    return pl.pallas_call(
        functools.partial(_combine_kernel, tc=tc, nsteps=nsteps),
        grid=(nsteps,),
        in_specs=[pl.BlockSpec((2 * tc,), lambda g: (g,), memory_space=pltpu.SMEM),
                  pl.BlockSpec((2 * tc,), lambda g: (jnp.minimum(g + 1, nsteps - 1),), memory_space=pltpu.SMEM),
                  pl.BlockSpec(memory_space=pl.ANY),
                  pl.BlockSpec((1, tc, D), tok),
                  pl.BlockSpec((1, tc, LANES), tok),
                  pl.BlockSpec((1, 6, D), lambda g: (g // n, 0, 0)),
                  pl.BlockSpec((1, D), lambda g: (0, 0))],
        out_specs=pl.BlockSpec((1, tc, D), tok),
        out_shape=jax.ShapeDtypeStruct((B, S, D), F32),
        scratch_shapes=[pltpu.VMEM((2, 2, tc, D), F32), pltpu.SemaphoreType.DMA((2,))],
        compiler_params=_cparams(("arbitrary",)),
    )(dest_flat, dest_flat, ys, x1, rw, mod, norm_f.reshape(1, D))


def _pick(n, pref):
    t = min(n, pref)
    assert n % t == 0, (n, pref)
    return t


def kernel(x, c, w_ada, b_ada, norm1, w_in, b_if, conv_w, conv_b, lam_q1, lam_k1, lam_q2, lam_k2,
           diff_norm, mlstm_norm, w_br_a, w_br_m, w_out, norm2, w_rg, b_rg, w_re, b_re,
           w_e1, w_e3, w_e2, norm_f):
    B, S, D = x.shape
    assert D == D_MODEL and w_ada.shape[0] == 1
    T = B * S
    l = 0

    mod = _ada(c, w_ada[l], b_ada[l]).reshape(B, 6, D)

    w = w_in[l]
    o_q, o_k, o_v, o_qk, o_mv, o_mo, o_if, o_ga, o_gm = np.cumsum((0,) + (
        DA_WIDTH, DA_WIDTH, DA_WIDTH, 2 * ML_WIDTH, ML_WIDTH, ML_WIDTH, 2 * ML_HEADS, D_MODEL))
    w_perm = jnp.concatenate([
        w[:, o_qk:o_qk + 2 * ML_WIDTH], w[:, o_ga:o_ga + D], w[:, o_gm:o_gm + D],
        w[:, o_q:o_q + DA_WIDTH] * (DA_HD ** -0.5), w[:, o_k:o_k + DA_WIDTH], w[:, o_v:o_v + DA_WIDTH],
        w[:, o_mv:o_mv + ML_WIDTH], w[:, o_mo:o_mo + ML_WIDTH], w[:, o_if:o_if + 2 * ML_HEADS],
        jnp.zeros((D, LANES - 2 * ML_HEADS), F32)], axis=1).astype(BF16)
    bif = jnp.concatenate([b_if[l], jnp.zeros((LANES - 2 * ML_HEADS,), F32)]).reshape(1, LANES)

    p, gates = _proj(x, mod, norm1[l], bif, w_perm, _pick(S, PROJ_TM))

    lamv = jnp.stack([lam_q1[l], lam_k1[l], lam_q2[l], lam_k2[l]])
    ya = _attn(p, lamv, diff_norm[l], _pick(S, ATTN_TQ), _pick(S, ATTN_TK))
    ym = _mlstm(p, gates, conv_w[l], conv_b[l], mlstm_norm[l], _pick(S, MLSTM_L))

    wr = jnp.concatenate([w_rg[l], w_re[l], jnp.zeros((D, LANES - N_GROUPS - N_EXPERTS), F32)], axis=1)
    br = jnp.concatenate([b_rg[l], b_re[l], jnp.zeros((LANES - N_GROUPS - N_EXPERTS,), F32)]).reshape(1, LANES)
    x1, h2, ri, rw, cnt = _post(ya, ym, p, x, mod, w_br_a[l].astype(BF16), w_br_m[l].astype(BF16),
                                w_out[l].astype(BF16), norm2[l], wr, br, _pick(S, POST_TM))

    counts = cnt[0, :N_EXPERTS].astype(jnp.int32)
    pc = ((counts + EXPERT_ROWS - 1) // EXPERT_ROWS) * EXPERT_ROWS
    ends = jnp.cumsum(pc)
    offs = ends - pc
    ri = ri.reshape(T, LANES)
    eid = ri[:, 0:2]
    rank = ri[:, 2:4]
    dest = rank + jnp.sum(jnp.where(eid[..., None] == jnp.arange(N_EXPERTS), offs, 0), axis=-1)
    dest_flat = dest.reshape(2 * T).astype(jnp.int32)
    n_rows = 2 * T + N_EXPERTS * EXPERT_ROWS
    n_tiles = n_rows // EXPERT_ROWS
    tile_e = jnp.sum(jnp.arange(n_tiles)[:, None] * EXPERT_ROWS >= ends[None, :], axis=1)
    tile_e = jnp.minimum(tile_e, N_EXPERTS - 1).astype(jnp.int32)
    used_tiles = (ends[-1:] // EXPERT_ROWS).astype(jnp.int32)

    xs = _dispatch(ends.astype(jnp.int32), pc.astype(jnp.int32), dest_flat, h2.reshape(T, D), n_rows,
                   _pick(T, SMEM_BLOCK_1D))
    w1 = w_e1[l].reshape(N_EXPERTS, D, D_EXPERT).astype(BF16)
    w3 = w_e3[l].reshape(N_EXPERTS, D, D_EXPERT).astype(BF16)
    w2 = w_e2[l].reshape(N_EXPERTS, D_EXPERT, D).astype(BF16)
    ys = _experts(tile_e, used_tiles, xs, w1, w3, w2)
    return _combine(dest_flat, ys, x1, rw, mod, norm_f, _pick(S, SMEM_BLOCK_1D // 2))
```

```python
import functools
import math

import jax
import jax.numpy as jnp
import numpy as np
from jax import lax
from jax.experimental import pallas as pl
from jax.experimental.pallas import tpu as pltpu

F32 = jnp.float32
BF16 = jnp.bfloat16

D_MODEL = 1024
DA_HEADS = 4
DA_HD = 64
DA_WIDTH = DA_HEADS * 2 * DA_HD
ML_HEADS = 4
ML_HD = 128
ML_WIDTH = ML_HEADS * ML_HD
CONV_W = 4
N_GROUPS = 4
EXPERTS_PER_GROUP = 8
N_EXPERTS = N_GROUPS * EXPERTS_PER_GROUP
D_EXPERT = 256
EPS = 1e-6
LAMBDA_INIT = 0.8 - 0.6 * math.exp(-0.3 * 0)

LANES = 128
SUBLANES = 8
SMEM_BLOCK_1D = 1024
NEG = -1e30
VMEM_LIMIT = 56 * 1024 * 1024

COL_MLQK = 0
COL_GA = 1024
COL_GM = 2048
COL_DAQ = 3072
COL_DAK = 3584
COL_DAV = 4096
COL_MLV = 4608
COL_MLO = 5120
COL_IF = 5632
N_PROJ = COL_IF + LANES
PROJ_TN = 1920
assert N_PROJ % PROJ_TN == 0

EXPERT_ROWS = 256
PROJ_TM = 1024
ATTN_TQ = 1024
ATTN_TK = 512
MLSTM_L = 256
POST_TM = 512


def _cparams(sem):
    return pltpu.CompilerParams(dimension_semantics=sem, vmem_limit_bytes=VMEM_LIMIT)


ROW_WORDS = D_MODEL // 2


def _pack_rows(x):
    return pltpu.pack_elementwise([x[:, :ROW_WORDS], x[:, ROW_WORDS:]], packed_dtype=BF16)


def _unpack_rows(w, half):
    return pltpu.unpack_elementwise(w, index=half, packed_dtype=BF16, unpacked_dtype=F32)


def _sigmoid(x):
    return 0.5 * jnp.tanh(0.5 * x) + 0.5


def _ada_kernel(c_ref, w_ref, b_ref, o_ref):
    c = c_ref[...]
    cs = c * _sigmoid(c)
    o_ref[...] = jnp.dot(cs, w_ref[...], precision=lax.Precision.HIGHEST,
                         preferred_element_type=F32) + b_ref[...]


def _ada(c, w, b):
    B, D = c.shape
    N = w.shape[1]
    tn = 1536
    return pl.pallas_call(
        _ada_kernel,
        grid=(N // tn,),
        in_specs=[pl.BlockSpec((B, D), lambda j: (0, 0)),
                  pl.BlockSpec((D, tn), lambda j: (0, j)),
                  pl.BlockSpec((1, tn), lambda j: (0, j))],
        out_specs=pl.BlockSpec((B, tn), lambda j: (0, j)),
        out_shape=jax.ShapeDtypeStruct((B, N), F32),
        compiler_params=_cparams(("arbitrary",)),
    )(c, w, b.reshape(1, N))


def _proj_kernel(x_ref, mod_ref, g_ref, bif_ref, w_ref, p_ref, gate_ref, h_scr, *, nj):
    j = pl.program_id(2)

    @pl.when(j == 0)
    def _():
        x = x_ref[0]
        y = x * lax.rsqrt(jnp.mean(x * x, axis=-1, keepdims=True) + EPS) * g_ref[...]
        h = y * (1.0 + mod_ref[0, 1:2, :]) + mod_ref[0, 0:1, :]
        h_scr[...] = h.astype(BF16)

    acc = jnp.dot(h_scr[...], w_ref[...], preferred_element_type=F32)
    p_ref[0] = acc.astype(BF16)

    @pl.when(j == nj - 1)
    def _():
        gate_ref[0] = acc[:, PROJ_TN - LANES:] + bif_ref[...]


def _proj(x, mod, norm1, bif, w_bf16, tm):
    B, S, D = x.shape
    nj = N_PROJ // PROJ_TN
    return pl.pallas_call(
        functools.partial(_proj_kernel, nj=nj),
        grid=(B, S // tm, nj),
        in_specs=[pl.BlockSpec((1, tm, D), lambda b, i, j: (b, i, 0)),
                  pl.BlockSpec((1, 6, D), lambda b, i, j: (b, 0, 0)),
                  pl.BlockSpec((1, D), lambda b, i, j: (0, 0)),
                  pl.BlockSpec((1, LANES), lambda b, i, j: (0, 0)),
                  pl.BlockSpec((D, PROJ_TN), lambda b, i, j: (0, j))],
        out_specs=[pl.BlockSpec((1, tm, PROJ_TN), lambda b, i, j: (b, i, j)),
                   pl.BlockSpec((1, tm, LANES), lambda b, i, j: (b, i, 0))],
        out_shape=[jax.ShapeDtypeStruct((B, S, N_PROJ), BF16),
                   jax.ShapeDtypeStruct((B, S, LANES), F32)],
        scratch_shapes=[pltpu.VMEM((tm, D), BF16)],
        compiler_params=_cparams(("arbitrary", "arbitrary", "arbitrary")),
    )(x, mod, norm1.reshape(1, D), bif, w_bf16)


ATTN_SUM_ROWS = 16

def _attn_block(q_ref, k_ref, v_ref, eq_ref, ek_ref, m_scr, acc_scr, off, cb, masked, tq, tk):
    q = q_ref[0]
    k = k_ref[0]
    v = v_ref[0]
    eq = eq_ref[0]
    ek = ek_ref[0]
    lane = lax.broadcasted_iota(jnp.int32, (1, LANES), 1)
    if masked:
        key = lax.broadcasted_iota(jnp.int32, (tk, tq), 0)
        qry = lax.broadcasted_iota(jnp.int32, (tk, tq), 1)
        valid = (key - qry) <= off
    scores = []
    for c in range(2):
        sel = (lane < DA_HD) if c == 0 else (lane >= DA_HD)
        s = lax.dot_general(jnp.where(sel, k, ek), jnp.where(sel, q, eq), (((1,), (1,)), ((), ())),
                            preferred_element_type=F32)
        scores.append(jnp.where(valid, s, NEG) if masked else s)
    vt = jnp.concatenate([v.T, jnp.ones((ATTN_SUM_ROWS, tk), BF16)], axis=0)
    for c in range(2):
        s = scores[c]
        m_prev = m_scr[c]
        m_new = jnp.maximum(m_prev, jnp.max(s, axis=0, keepdims=True) - cb)
        alpha = jnp.exp(m_prev - m_new)
        p = jnp.exp((s - (m_new + cb)).astype(BF16))
        acc_scr[c] = alpha * acc_scr[c] + jnp.dot(vt, p, preferred_element_type=F32)
        m_scr[c] = m_new


def _attn_kernel(it_ref, jt_ref, slope_ref, q_ref, k_ref, v_ref, eq_ref, ek_ref, lam_ref, dn_ref, o_ref,
                 m_scr, acc_scr, *, tq, tk):
    h = pl.program_id(1)
    step = pl.program_id(2)
    i = it_ref[step]
    j = jt_ref[step]
    ratio = tq // tk

    @pl.when(j == 0)
    def _():
        m_scr[...] = jnp.full(m_scr.shape, NEG, F32)
        acc_scr[...] = jnp.zeros(acc_scr.shape, F32)

    off = i * tq - j * tk
    cb = slope_ref[h] * off.astype(F32)
    args = (q_ref, k_ref, v_ref, eq_ref, ek_ref, m_scr, acc_scr, off, cb)

    @pl.when(j < i * ratio)
    def _():
        _attn_block(*args, masked=False, tq=tq, tk=tk)

    @pl.when(j >= i * ratio)
    def _():
        _attn_block(*args, masked=True, tq=tq, tk=tk)

    @pl.when(j == (i + 1) * ratio - 1)
    def _():
        lv = lam_ref[...]
        lam = (jnp.exp(jnp.sum(lv[0:1] * lv[1:2], axis=-1, keepdims=True))
               - jnp.exp(jnp.sum(lv[2:3] * lv[3:4], axis=-1, keepdims=True)) + LAMBDA_INIT)
        vd = 2 * DA_HD
        o = (acc_scr[0, 0:vd, :] / acc_scr[0, vd:vd + 1, :]
             - lam * (acc_scr[1, 0:vd, :] / acc_scr[1, vd:vd + 1, :]))
        o = o * lax.rsqrt(jnp.mean(o * o, axis=0, keepdims=True) + EPS)
        o_ref[0] = (o.T * dn_ref[...] * (1.0 - LAMBDA_INIT)).astype(BF16)


def _alibi_columns(n, slopes, q_side):
    assert n <= 256 * 256 and all(math.log2(s).is_integer() for s in slopes)
    pos = np.arange(n)
    lo, hi = (pos % 256).astype(np.float64), (pos // 256 * 256).astype(np.float64)
    out = np.zeros((len(slopes), n, LANES), np.float64)
    for h, s in enumerate(slopes):
        cols = (-s * lo, -s * hi, np.ones(n), np.ones(n)) if q_side else (np.ones(n), np.ones(n), s * lo, s * hi)
        for base in (0, DA_HD):
            for c, v in enumerate(cols):
                out[h, :, base + c] = v
    return jnp.asarray(out, dtype=BF16)


def _attn(p, lamv, diff_norm, tq, tk):
    B, S, _ = p.shape
    nq, ratio = S // tq, tq // tk
    slopes = [2.0 ** (-8.0 * (h + 1) / DA_HEADS) for h in range(DA_HEADS)]
    steps = [(i, j) for i in range(nq) for j in range((i + 1) * ratio)]
    it = jnp.asarray([s[0] for s in steps], jnp.int32)
    jt = jnp.asarray([s[1] for s in steps], jnp.int32)
    qb, kb, vb = COL_DAQ // LANES, COL_DAK // LANES, COL_DAV // LANES
    grid_spec = pltpu.PrefetchScalarGridSpec(
        num_scalar_prefetch=3,
        grid=(B, DA_HEADS, len(steps)),
        in_specs=[pl.BlockSpec((1, tq, LANES), lambda b, h, s, it, jt, sl: (b, it[s], qb + h)),
                  pl.BlockSpec((1, tk, LANES), lambda b, h, s, it, jt, sl: (b, jt[s], kb + h)),
                  pl.BlockSpec((1, tk, LANES), lambda b, h, s, it, jt, sl: (b, jt[s], vb + h)),
                  pl.BlockSpec((1, tq, LANES), lambda b, h, s, it, jt, sl: (h, 0, 0)),
                  pl.BlockSpec((1, tk, LANES), lambda b, h, s, it, jt, sl: (h, 0, 0)),
                  pl.BlockSpec((4, DA_HD), lambda b, h, s, it, jt, sl: (0, 0)),
                  pl.BlockSpec((1, 2 * DA_HD), lambda b, h, s, it, jt, sl: (0, 0))],
        out_specs=pl.BlockSpec((1, tq, LANES), lambda b, h, s, it, jt, sl: (b, it[s], h)),
        scratch_shapes=[pltpu.VMEM((2, 1, tq), F32),
                        pltpu.VMEM((2, 2 * DA_HD + ATTN_SUM_ROWS, tq), F32)],
    )
    return pl.pallas_call(
        functools.partial(_attn_kernel, tq=tq, tk=tk),
        grid_spec=grid_spec,
        out_shape=jax.ShapeDtypeStruct((B, S, DA_WIDTH), BF16),
        compiler_params=_cparams(("arbitrary",) * 3),
    )(it, jt, jnp.asarray(slopes, F32), p, p, p, _alibi_columns(tq, slopes, True),
      _alibi_columns(tk, slopes, False), lamv, diff_norm.reshape(1, 2 * DA_HD))


def _mlstm_kernel(qk_ref, v_ref, o_ref, g_ref, cw_ref, cb_ref, nw_ref, y_ref,
                  ext_scr, c_scr, n_scr, m_scr, *, L):
    i = pl.program_id(1)
    hist = SUBLANES

    @pl.when(i == 0)
    def _():
        ext_scr[0:hist, :] = jnp.zeros((hist, 2 * ML_WIDTH), F32)
        c_scr[...] = jnp.zeros(c_scr.shape, F32)
        n_scr[...] = jnp.zeros(n_scr.shape, F32)
        m_scr[...] = jnp.zeros(m_scr.shape, F32)

    raw = qk_ref[0].astype(F32)
    ext_scr[hist:hist + L, :] = raw
    conv = cb_ref[...] + jnp.zeros((L, 2 * ML_WIDTH), F32)
    for j in range(CONV_W):
        off = hist - (CONV_W - 1) + j
        conv = conv + ext_scr[off:off + L, :] * cw_ref[j:j + 1, :]
    ext_scr[0:hist, :] = raw[L - hist:L, :]
    qkc = conv * _sigmoid(conv)

    gts = g_ref[0]
    fpre = pltpu.roll(gts, LANES - ML_HEADS, axis=1)
    lf = jnp.minimum(fpre, 0.0) - jnp.log(1.0 + jnp.exp(-jnp.abs(fpre)))
    row = lax.broadcasted_iota(jnp.int32, (L, L), 0)
    col = lax.broadcasted_iota(jnp.int32, (L, L), 1)
    causal = col <= row
    bcum = jnp.dot(causal.astype(F32), lf, precision=lax.Precision.HIGHEST, preferred_element_type=F32)
    r = gts - bcum
    rt = r.T
    m_all = m_scr[...]
    lane = lax.broadcasted_iota(jnp.int32, (1, LANES), 1)
    m_next = m_all
    v_all = v_ref[0]
    o_all = o_ref[0]
    for h in range(ML_HEADS):
        hs = slice(h * ML_HD, (h + 1) * ML_HD)
        bcol = bcum[:, h:h + 1]
        rcol = r[:, h:h + 1]
        rrow = rt[h:h + 1, :]
        g = bcum[L - 1:L, h:h + 1]
        mh = m_all[:, h:h + 1]
        dm = jnp.where(causal, bcol + rrow, NEG)
        inter = bcol + mh
        mj = jnp.maximum(inter, jnp.max(dm, axis=-1, keepdims=True))
        w_intra = jnp.exp(dm - mj)
        w_inter = jnp.exp(inter - mj)
        qh = qkc[:, hs]
        kh = qkc[:, ML_WIDTH + h * ML_HD:ML_WIDTH + (h + 1) * ML_HD] * (ML_HD ** -0.5)
        vh = v_all[:, hs]
        qb = qh.astype(BF16)
        kb = kh.astype(BF16)
        s = lax.dot_general(qb, kb, (((1,), (1,)), ((), ())), preferred_element_type=F32) * w_intra
        c_old = c_scr[h]
        n_old = n_scr[h:h + 1, :]
        num = (jnp.dot(s.astype(BF16), vh, preferred_element_type=F32)
               + lax.dot_general(qb, c_old.astype(BF16), (((1,), (1,)), ((), ())),
                                 preferred_element_type=F32) * w_inter)
        den = (jnp.sum(s, axis=-1, keepdims=True)
               + w_inter * jnp.sum(qh * n_old, axis=-1, keepdims=True))
        denom = jnp.maximum(jnp.abs(den), jnp.exp(-mj))
        ht = num / denom
        a_col = g + rcol
        m_new = jnp.maximum(g + mh, jnp.max(a_col, axis=0, keepdims=True))
        wa = jnp.exp(a_col - m_new)
        decay = jnp.exp(g + mh - m_new)
        vw_t = (vh.astype(F32) * wa).T.astype(BF16)
        c_scr[h] = decay * c_old + jnp.dot(vw_t, kb, preferred_element_type=F32)
        n_scr[h:h + 1, :] = decay * n_old + jnp.sum(kh * wa, axis=0, keepdims=True)
        m_next = jnp.where(lane == h, m_new, m_next)
        z = _sigmoid(o_all[:, hs].astype(F32)) * ht
        z = z * lax.rsqrt(jnp.mean(z * z, axis=-1, keepdims=True) + EPS) * nw_ref[...]
        y_ref[0, :, hs] = z.astype(BF16)
    m_scr[...] = m_next


def _mlstm(p, gates, conv_w, conv_b, mlstm_norm, L):
    B, S, _ = p.shape
    return pl.pallas_call(
        functools.partial(_mlstm_kernel, L=L),
        grid=(B, S // L),
        in_specs=[pl.BlockSpec((1, L, 2 * ML_WIDTH), lambda b, i: (b, i, COL_MLQK // (2 * ML_WIDTH))),
                  pl.BlockSpec((1, L, ML_WIDTH), lambda b, i: (b, i, COL_MLV // ML_WIDTH)),
                  pl.BlockSpec((1, L, ML_WIDTH), lambda b, i: (b, i, COL_MLO // ML_WIDTH)),
                  pl.BlockSpec((1, L, LANES), lambda b, i: (b, i, 0)),
                  pl.BlockSpec((CONV_W, 2 * ML_WIDTH), lambda b, i: (0, 0)),
                  pl.BlockSpec((1, 2 * ML_WIDTH), lambda b, i: (0, 0)),
                  pl.BlockSpec((1, ML_HD), lambda b, i: (0, 0))],
        out_specs=pl.BlockSpec((1, L, ML_WIDTH), lambda b, i: (b, i, 0)),
        out_shape=jax.ShapeDtypeStruct((B, S, ML_WIDTH), BF16),
        scratch_shapes=[pltpu.VMEM((SUBLANES + L, 2 * ML_WIDTH), F32),
                        pltpu.VMEM((ML_HEADS, ML_HD, ML_HD), F32),
                        pltpu.VMEM((SUBLANES, ML_HD), F32),
                        pltpu.VMEM((1, LANES), F32)],
        compiler_params=_cparams(("arbitrary", "arbitrary")),
    )(p, p, p, gates, conv_w, conv_b.reshape(1, -1), mlstm_norm.reshape(1, ML_HD))


def _post_kernel(ya_ref, ym_ref, ga_ref, gm_ref, x_ref, mod_ref, wa_ref, wm_ref, wo_ref, n2_ref,
                 wr_ref, br_ref, x1_ref, h2_ref, ri_ref, rw_ref, cnt_ref, run_scr, *, tm):
    first = jnp.logical_and(pl.program_id(0) == 0, pl.program_id(1) == 0)

    @pl.when(first)
    def _():
        run_scr[...] = jnp.zeros(run_scr.shape, F32)

    a = jnp.dot(ya_ref[0], wa_ref[...], preferred_element_type=F32)
    m = jnp.dot(ym_ref[0], wm_ref[...], preferred_element_type=F32)
    merged = _sigmoid(ga_ref[0]).astype(F32) * a + _sigmoid(gm_ref[0]).astype(F32) * m
    o = jnp.dot(merged.astype(BF16), wo_ref[...], preferred_element_type=F32)
    x1 = x_ref[0] + mod_ref[0, 2:3, :] * o
    x1_ref[0] = x1
    h2 = x1 * lax.rsqrt(jnp.mean(x1 * x1, axis=-1, keepdims=True) + EPS) * n2_ref[...]
    h2 = h2 * (1.0 + mod_ref[0, 4:5, :]) + mod_ref[0, 3:4, :]
    h2_ref[0] = _pack_rows(h2)

    hi = h2.astype(BF16)
    lo = (h2 - hi.astype(F32)).astype(BF16)
    logits = jnp.dot(jnp.concatenate([hi, lo, hi], axis=1), wr_ref[...], preferred_element_type=F32) + br_ref[...]
    lane = lax.broadcasted_iota(jnp.int32, (tm, LANES), 1)
    big = jnp.int32(4 * LANES)
    gl = jnp.where(lane < N_GROUPS, logits, NEG)
    gmax = jnp.max(gl, axis=-1, keepdims=True)
    gsel = jnp.min(jnp.where(gl == gmax, lane, big), axis=-1, keepdims=True)
    pgrp = 1.0 / jnp.sum(jnp.exp(gl - gmax), axis=-1, keepdims=True)
    lo = N_GROUPS + EXPERTS_PER_GROUP * gsel
    el = jnp.where(jnp.logical_and(lane >= lo, lane < lo + EXPERTS_PER_GROUP), logits, NEG)
    e1 = jnp.max(el, axis=-1, keepdims=True)
    i1 = jnp.min(jnp.where(el == e1, lane, big), axis=-1, keepdims=True)
    el2 = jnp.where(lane == i1, NEG, el)
    e2 = jnp.max(el2, axis=-1, keepdims=True)
    i2 = jnp.min(jnp.where(el2 == e2, lane, big), axis=-1, keepdims=True)
    tt = jnp.exp(e2 - e1)
    w1 = pgrp / (1.0 + tt)
    w2 = pgrp * tt / (1.0 + tt)
    eid1 = i1 - N_GROUPS
    eid2 = i2 - N_GROUPS

    oh1 = jnp.where(lane == eid1, 1.0, 0.0).astype(F32)
    oh2 = jnp.where(lane == eid2, 1.0, 0.0).astype(F32)
    cat = jnp.concatenate([oh1, oh2], axis=1).astype(BF16)
    row = lax.broadcasted_iota(jnp.int32, (tm, tm), 0)
    col = lax.broadcasted_iota(jnp.int32, (tm, tm), 1)
    before = jnp.where(col < row, 1.0, 0.0).astype(BF16)
    earlier = jnp.dot(before, cat, preferred_element_type=F32)
    c1 = jnp.sum(oh1, axis=0, keepdims=True)
    c2 = jnp.sum(oh2, axis=0, keepdims=True)
    run = run_scr[...]
    rank1 = jnp.sum((earlier[:, :LANES] + run) * oh1, axis=-1, keepdims=True)
    rank2 = jnp.sum((earlier[:, LANES:] + run + c1) * oh2, axis=-1, keepdims=True)
    run_new = run + c1 + c2
    run_scr[...] = run_new
    cnt_ref[...] = jnp.broadcast_to(run_new, cnt_ref.shape)
    ri = jnp.where(lane == 0, eid1,
                   jnp.where(lane == 1, eid2,
                             jnp.where(lane == 2, rank1.astype(jnp.int32),
                                       jnp.where(lane == 3, rank2.astype(jnp.int32), 0))))
    ri_ref[0] = ri
    rw_ref[0] = jnp.where(lane == 0, w1, jnp.where(lane == 1, w2, 0.0))


def _post(ya, ym, p, x, mod, wa, wm, wo, norm2, wr, br, tm):
    B, S, D = x.shape
    tok = lambda b, i: (b, i, 0)
    const = lambda b, i: (0, 0)
    return pl.pallas_call(
        functools.partial(_post_kernel, tm=tm),
        grid=(B, S // tm),
        in_specs=[pl.BlockSpec((1, tm, DA_WIDTH), tok),
                  pl.BlockSpec((1, tm, ML_WIDTH), tok),
                  pl.BlockSpec((1, tm, D), lambda b, i: (b, i, COL_GA // D_MODEL)),
                  pl.BlockSpec((1, tm, D), lambda b, i: (b, i, COL_GM // D_MODEL)),
                  pl.BlockSpec((1, tm, D), tok),
                  pl.BlockSpec((1, 6, D), lambda b, i: (b, 0, 0)),
                  pl.BlockSpec((DA_WIDTH, D), const),
                  pl.BlockSpec((ML_WIDTH, D), const),
                  pl.BlockSpec((D, D), const),
                  pl.BlockSpec((1, D), const),
                  pl.BlockSpec((3 * D, LANES), const),
                  pl.BlockSpec((1, LANES), const)],
        out_specs=[pl.BlockSpec((1, tm, D), tok),
                   pl.BlockSpec((1, tm, ROW_WORDS), tok),
                   pl.BlockSpec((1, tm, LANES), tok),
                   pl.BlockSpec((1, tm, LANES), tok),
                   pl.BlockSpec((SUBLANES, LANES), const)],
        out_shape=[jax.ShapeDtypeStruct((B, S, D), F32),
                   jax.ShapeDtypeStruct((B, S, ROW_WORDS), jnp.uint32),
                   jax.ShapeDtypeStruct((B, S, LANES), jnp.int32),
                   jax.ShapeDtypeStruct((B, S, LANES), F32),
                   jax.ShapeDtypeStruct((SUBLANES, LANES), F32)],
        scratch_shapes=[pltpu.VMEM((1, LANES), F32)],
        compiler_params=_cparams(("arbitrary", "arbitrary")),
    )(ya, ym, p, p, x, mod, wa, wm, wo, norm2.reshape(1, D), wr, br)


DMA_UNROLL = 8


def _dispatch_kernel(ends_ref, pc_ref, dest_ref, h2_ref, xs_hbm, zbuf, sem, zsem, *, td):
    i = pl.program_id(0)

    @pl.when(i == 0)
    def _():
        zbuf[...] = jnp.zeros(zbuf.shape, zbuf.dtype)
        for e in range(N_EXPERTS):
            @pl.when(pc_ref[e] > 0)
            def _():
                start = pl.multiple_of(ends_ref[e] - EXPERT_ROWS, EXPERT_ROWS)
                cp = pltpu.make_async_copy(zbuf, xs_hbm.at[pl.ds(start, EXPERT_ROWS)], zsem)
                cp.start()
                cp.wait()

        def fill(tile, carry):
            start = pl.multiple_of(tile * EXPERT_ROWS, EXPERT_ROWS)
            cp = pltpu.make_async_copy(zbuf, xs_hbm.at[pl.ds(start, EXPERT_ROWS)], zsem)
            cp.start()
            cp.wait()
            return carry

        lax.fori_loop(ends_ref[N_EXPERTS - 1] // EXPERT_ROWS, xs_hbm.shape[0] // EXPERT_ROWS, fill, 0)

    def issue(t, carry):
        for s in range(2):
            pltpu.make_async_copy(h2_ref.at[pl.ds(t, 1)], xs_hbm.at[pl.ds(dest_ref[2 * t + s], 1)], sem).start()
        return carry

    lax.fori_loop(0, td, issue, 0, unroll=DMA_UNROLL)
    for s in range(2):
        pltpu.make_async_copy(h2_ref, xs_hbm.at[pl.ds(0, td)], sem).wait()


def _dispatch(ends, pc, dest_flat, h2, n_rows, td):
    T, D = h2.shape
    grid_spec = pltpu.PrefetchScalarGridSpec(
        num_scalar_prefetch=2,
        grid=(T // td,),
        in_specs=[pl.BlockSpec((2 * td,), lambda i, e, c: (i,), memory_space=pltpu.SMEM),
                  pl.BlockSpec((td, D), lambda i, e, c: (i, 0))],
        out_specs=pl.BlockSpec(memory_space=pl.ANY),
        scratch_shapes=[pltpu.VMEM((EXPERT_ROWS, D), h2.dtype),
                        pltpu.SemaphoreType.DMA(()), pltpu.SemaphoreType.DMA(())],
    )
    return pl.pallas_call(
        functools.partial(_dispatch_kernel, td=td),
        grid_spec=grid_spec,
        out_shape=jax.ShapeDtypeStruct((n_rows, D), h2.dtype),
        compiler_params=_cparams(("arbitrary",)),
    )(ends, pc, dest_flat, h2)


def _experts_kernel(te_ref, nt_ref, xs_ref, w1_ref, w3_ref, w2_ref, ys_ref):
    i = pl.program_id(0)

    @pl.when(i < nt_ref[0])
    def _():
        w = xs_ref[...]
        xl = _unpack_rows(w, 0).astype(BF16)
        xh = _unpack_rows(w, 1).astype(BF16)

        def up(w_ref):
            return (jnp.dot(xl, w_ref[0, :ROW_WORDS, :], preferred_element_type=F32)
                    + jnp.dot(xh, w_ref[0, ROW_WORDS:, :], preferred_element_type=F32))

        a = up(w1_ref)
        b = up(w3_ref)
        hid = (a * _sigmoid(a) * b).astype(BF16)
        ys_ref[...] = _pack_rows(jnp.dot(hid, w2_ref[0], preferred_element_type=F32))

    @pl.when(i >= nt_ref[0])
    def _():
        ys_ref[...] = jnp.zeros(ys_ref.shape, ys_ref.dtype)


def _experts(tile_e, n_tiles, xs, w1, w3, w2):
    n_rows = xs.shape[0]
    D = D_MODEL
    nt = n_rows // EXPERT_ROWS
    rows = lambda i, te, n: (jnp.minimum(i, jnp.maximum(n[0] - 1, 0)), 0)
    wsel = lambda i, te, n: (te[i], 0, 0)
    grid_spec = pltpu.PrefetchScalarGridSpec(
        num_scalar_prefetch=2,
        grid=(nt,),
        in_specs=[pl.BlockSpec((EXPERT_ROWS, ROW_WORDS), rows),
                  pl.BlockSpec((1, D, D_EXPERT), wsel),
                  pl.BlockSpec((1, D, D_EXPERT), wsel),
                  pl.BlockSpec((1, D_EXPERT, D), wsel)],
        out_specs=pl.BlockSpec((EXPERT_ROWS, ROW_WORDS), lambda i, te, n: (i, 0)),
    )
    return pl.pallas_call(
        _experts_kernel,
        grid_spec=grid_spec,
        out_shape=jax.ShapeDtypeStruct((n_rows, ROW_WORDS), jnp.uint32),
        compiler_params=_cparams(("arbitrary",)),
    )(tile_e, n_tiles, xs, w1, w3, w2)


def _combine_kernel(dcur_ref, dnxt_ref, ys_hbm, x1_ref, rw_ref, mod_ref, nf_ref, o_ref, ybuf, sem, *, tc, nsteps):
    g = pl.program_id(0)
    slot = g % 2

    def gather(dest_ref, sl):
        def issue(t, carry):
            for s in range(2):
                pltpu.make_async_copy(ys_hbm.at[pl.ds(dest_ref[2 * t + s], 1)],
                                      ybuf.at[sl, s, pl.ds(t, 1)], sem.at[sl]).start()
            return carry

        lax.fori_loop(0, tc, issue, 0, unroll=DMA_UNROLL)

    @pl.when(g == 0)
    def _():
        gather(dcur_ref, 0)

    @pl.when(g + 1 < nsteps)
    def _():
        gather(dnxt_ref, 1 - slot)

    for s in range(2):
        pltpu.make_async_copy(ys_hbm.at[pl.ds(0, tc)], ybuf.at[slot, s], sem.at[slot]).wait()

    rw = rw_ref[0]
    y0 = ybuf[slot, 0]
    y1 = ybuf[slot, 1]
    y = jnp.concatenate([rw[:, 0:1] * _unpack_rows(y0, half) + rw[:, 1:2] * _unpack_rows(y1, half)
                         for half in range(2)], axis=1)
    xo = x1_ref[0] + mod_ref[0, 5:6, :] * y
    o_ref[0] = xo * lax.rsqrt(jnp.mean(xo * xo, axis=-1, keepdims=True) + EPS) * nf_ref[...]


def _combine(dest_flat, ys, x1, rw, mod, norm_f, tc):
    B, S, D = x1.shape
    n = S // tc
    nsteps = B * n
    tok = lambda g: (g // n, g % n, 0)
    return pl.pallas_call(
        functools.partial(_combine_kernel, tc=tc, nsteps=nsteps),
        grid=(nsteps,),
        in_specs=[pl.BlockSpec((2 * tc,), lambda g: (g,), memory_space=pltpu.SMEM),
                  pl.BlockSpec((2 * tc,), lambda g: (jnp.minimum(g + 1, nsteps - 1),), memory_space=pltpu.SMEM),
                  pl.BlockSpec(memory_space=pl.ANY),
                  pl.BlockSpec((1, tc, D), tok),
                  pl.BlockSpec((1, tc, LANES), tok),
                  pl.BlockSpec((1, 6, D), lambda g: (g // n, 0, 0)),
                  pl.BlockSpec((1, D), lambda g: (0, 0))],
        out_specs=pl.BlockSpec((1, tc, D), tok),
        out_shape=jax.ShapeDtypeStruct((B, S, D), F32),
        scratch_shapes=[pltpu.VMEM((2, 2, tc, ROW_WORDS), jnp.uint32), pltpu.SemaphoreType.DMA((2,))],
        compiler_params=_cparams(("arbitrary",)),
    )(dest_flat, dest_flat, ys, x1, rw, mod, norm_f.reshape(1, D))


def _pick(n, pref):
    t = min(n, pref)
    assert n % t == 0, (n, pref)
    return t


def kernel(x, c, w_ada, b_ada, norm1, w_in, b_if, conv_w, conv_b, lam_q1, lam_k1, lam_q2, lam_k2,
           diff_norm, mlstm_norm, w_br_a, w_br_m, w_out, norm2, w_rg, b_rg, w_re, b_re,
           w_e1, w_e3, w_e2, norm_f):
    B, S, D = x.shape
    assert D == D_MODEL and w_ada.shape[0] == 1
    T = B * S
    l = 0

    mod = _ada(c, w_ada[l], b_ada[l]).reshape(B, 6, D)

    w = w_in[l]
    o_q, o_k, o_v, o_qk, o_mv, o_mo, o_if, o_ga, o_gm = np.cumsum((0,) + (
        DA_WIDTH, DA_WIDTH, DA_WIDTH, 2 * ML_WIDTH, ML_WIDTH, ML_WIDTH, 2 * ML_HEADS, D_MODEL))
    w_perm = jnp.concatenate([
        w[:, o_qk:o_qk + 2 * ML_WIDTH], w[:, o_ga:o_ga + D], w[:, o_gm:o_gm + D],
        w[:, o_q:o_q + DA_WIDTH] * (DA_HD ** -0.5), w[:, o_k:o_k + DA_WIDTH], w[:, o_v:o_v + DA_WIDTH],
        w[:, o_mv:o_mv + ML_WIDTH], w[:, o_mo:o_mo + ML_WIDTH], w[:, o_if:o_if + 2 * ML_HEADS],
        jnp.zeros((D, LANES - 2 * ML_HEADS), F32)], axis=1).astype(BF16)
    bif = jnp.concatenate([b_if[l], jnp.zeros((LANES - 2 * ML_HEADS,), F32)]).reshape(1, LANES)

    p, gates = _proj(x, mod, norm1[l], bif, w_perm, _pick(S, PROJ_TM))

    lamv = jnp.stack([lam_q1[l], lam_k1[l], lam_q2[l], lam_k2[l]])
    ya = _attn(p, lamv, diff_norm[l], _pick(S, ATTN_TQ), _pick(S, ATTN_TK))
    ym = _mlstm(p, gates, conv_w[l], conv_b[l], mlstm_norm[l], _pick(S, MLSTM_L))

    wr = jnp.concatenate([w_rg[l], w_re[l], jnp.zeros((D, LANES - N_GROUPS - N_EXPERTS), F32)], axis=1)
    wr_hi = wr.astype(BF16)
    wr_lo = (wr - wr_hi.astype(F32)).astype(BF16)
    wr = jnp.concatenate([wr_hi, wr_hi, wr_lo], axis=0)
    br = jnp.concatenate([b_rg[l], b_re[l], jnp.zeros((LANES - N_GROUPS - N_EXPERTS,), F32)]).reshape(1, LANES)
    x1, h2, ri, rw, cnt = _post(ya, ym, p, x, mod, w_br_a[l].astype(BF16), w_br_m[l].astype(BF16),
                                w_out[l].astype(BF16), norm2[l], wr, br, _pick(S, POST_TM))

    counts = cnt[0, :N_EXPERTS].astype(jnp.int32)
    pc = ((counts + EXPERT_ROWS - 1) // EXPERT_ROWS) * EXPERT_ROWS
    ends = jnp.cumsum(pc)
    offs = ends - pc
    ri = ri.reshape(T, LANES)
    eid = ri[:, 0:2]
    rank = ri[:, 2:4]
    dest = rank + jnp.sum(jnp.where(eid[..., None] == jnp.arange(N_EXPERTS), offs, 0), axis=-1)
    dest_flat = dest.reshape(2 * T).astype(jnp.int32)
    n_rows = 2 * T + N_EXPERTS * EXPERT_ROWS
    n_tiles = n_rows // EXPERT_ROWS
    tile_e = jnp.sum(jnp.arange(n_tiles)[:, None] * EXPERT_ROWS >= ends[None, :], axis=1)
    tile_e = jnp.minimum(tile_e, N_EXPERTS - 1).astype(jnp.int32)
    used_tiles = (ends[-1:] // EXPERT_ROWS).astype(jnp.int32)

    xs = _dispatch(ends.astype(jnp.int32), pc.astype(jnp.int32), dest_flat, h2.reshape(T, ROW_WORDS), n_rows,
                   _pick(T, SMEM_BLOCK_1D))
    w1 = w_e1[l].reshape(N_EXPERTS, D, D_EXPERT).astype(BF16)
    w3 = w_e3[l].reshape(N_EXPERTS, D, D_EXPERT).astype(BF16)
    w2 = w_e2[l].reshape(N_EXPERTS, D_EXPERT, D).astype(BF16)
    ys = _experts(tile_e, used_tiles, xs, w1, w3, w2)
    return _combine(dest_flat, ys, x1, rw, mod, norm_f, _pick(S, SMEM_BLOCK_1D // 2))
```

```python
import functools
import math

import jax
import jax.numpy as jnp
import numpy as np
from jax import lax
from jax.experimental import pallas as pl
from jax.experimental.pallas import tpu as pltpu

F32 = jnp.float32
BF16 = jnp.bfloat16

D_MODEL = 1024
DA_HEADS = 4
DA_HD = 64
DA_WIDTH = DA_HEADS * 2 * DA_HD
ML_HEADS = 4
ML_HD = 128
ML_WIDTH = ML_HEADS * ML_HD
CONV_W = 4
N_GROUPS = 4
EXPERTS_PER_GROUP = 8
N_EXPERTS = N_GROUPS * EXPERTS_PER_GROUP
D_EXPERT = 256
EPS = 1e-6
LAMBDA_INIT = 0.8 - 0.6 * math.exp(-0.3 * 0)

LANES = 128
SUBLANES = 8
SMEM_BLOCK_1D = 1024
NEG = -1e30
VMEM_LIMIT = 56 * 1024 * 1024

COL_MLQK = 0
COL_GA = 1024
COL_GM = 2048
COL_DAQ = 3072
COL_DAK = 3584
COL_DAV = 4096
COL_MLV = 4608
COL_MLO = 5120
COL_IF = 5632
N_PROJ = COL_IF + LANES
PROJ_TN = 1920
assert N_PROJ % PROJ_TN == 0

EXPERT_ROWS = 256
PROJ_TM = 1024
ATTN_TQ = 1024
ATTN_TK = 512
MLSTM_L = 256
POST_TM = 512
ROUTE_ROWS = 48


def _cparams(sem):
    return pltpu.CompilerParams(dimension_semantics=sem, vmem_limit_bytes=VMEM_LIMIT)


def _sigmoid(x):
    return 0.5 * jnp.tanh(0.5 * x) + 0.5


def _ada_kernel(c_ref, w_ref, b_ref, o_ref):
    c = c_ref[...]
    cs = c * _sigmoid(c)
    o_ref[...] = jnp.dot(cs, w_ref[...], precision=lax.Precision.HIGHEST,
                         preferred_element_type=F32) + b_ref[...]


def _ada(c, w, b):
    B, D = c.shape
    N = w.shape[1]
    tn = 1536
    return pl.pallas_call(
        _ada_kernel,
        grid=(N // tn,),
        in_specs=[pl.BlockSpec((B, D), lambda j: (0, 0)),
                  pl.BlockSpec((D, tn), lambda j: (0, j)),
                  pl.BlockSpec((1, tn), lambda j: (0, j))],
        out_specs=pl.BlockSpec((B, tn), lambda j: (0, j)),
        out_shape=jax.ShapeDtypeStruct((B, N), F32),
        compiler_params=_cparams(("arbitrary",)),
    )(c, w, b.reshape(1, N))


def _proj_kernel(x_ref, mod_ref, g_ref, bif_ref, w_ref, p_ref, gate_ref, h_scr, *, nj):
    j = pl.program_id(2)

    @pl.when(j == 0)
    def _():
        x = x_ref[0]
        y = x * lax.rsqrt(jnp.mean(x * x, axis=-1, keepdims=True) + EPS) * g_ref[...]
        h = y * (1.0 + mod_ref[0, 1:2, :]) + mod_ref[0, 0:1, :]
        h_scr[...] = h.astype(BF16)

    acc = jnp.dot(h_scr[...], w_ref[...], preferred_element_type=F32)
    p_ref[0] = acc.astype(BF16)

    @pl.when(j == nj - 1)
    def _():
        gate_ref[0] = acc[:, PROJ_TN - LANES:] + bif_ref[...]


def _proj(x, mod, norm1, bif, w_bf16, tm):
    B, S, D = x.shape
    nj = N_PROJ // PROJ_TN
    return pl.pallas_call(
        functools.partial(_proj_kernel, nj=nj),
        grid=(B, S // tm, nj),
        in_specs=[pl.BlockSpec((1, tm, D), lambda b, i, j: (b, i, 0)),
                  pl.BlockSpec((1, 6, D), lambda b, i, j: (b, 0, 0)),
                  pl.BlockSpec((1, D), lambda b, i, j: (0, 0)),
                  pl.BlockSpec((1, LANES), lambda b, i, j: (0, 0)),
                  pl.BlockSpec((D, PROJ_TN), lambda b, i, j: (0, j))],
        out_specs=[pl.BlockSpec((1, tm, PROJ_TN), lambda b, i, j: (b, i, j)),
                   pl.BlockSpec((1, tm, LANES), lambda b, i, j: (b, i, 0))],
        out_shape=[jax.ShapeDtypeStruct((B, S, N_PROJ), BF16),
                   jax.ShapeDtypeStruct((B, S, LANES), F32)],
        scratch_shapes=[pltpu.VMEM((tm, D), BF16)],
        compiler_params=_cparams(("arbitrary", "arbitrary", "arbitrary")),
    )(x, mod, norm1.reshape(1, D), bif, w_bf16)


ATTN_SUM_ROWS = 16

ATTN_HEADS = 2


def _attn_block(q_ref, k_ref, v_ref, eq_ref, ek_ref, m_scr, acc_scr, off, cbs, masked, tq, tk):
    lane = lax.broadcasted_iota(jnp.int32, (1, LANES), 1)
    if masked:
        key = lax.broadcasted_iota(jnp.int32, (tk, tq), 0)
        qry = lax.broadcasted_iota(jnp.int32, (tk, tq), 1)
        valid = (key - qry) <= off
    scores = []
    for u in range(ATTN_HEADS):
        hl = slice(u * LANES, (u + 1) * LANES)
        q = q_ref[0, :, hl]
        k = k_ref[0, :, hl]
        for c in range(2):
            sel = (lane < DA_HD) if c == 0 else (lane >= DA_HD)
            s = lax.dot_general(jnp.where(sel, k, ek_ref[u]), jnp.where(sel, q, eq_ref[u]),
                                (((1,), (1,)), ((), ())), preferred_element_type=F32)
            scores.append(jnp.where(valid, s, NEG) if masked else s)
    for u in range(ATTN_HEADS):
        vt = jnp.concatenate([v_ref[0, :, u * LANES:(u + 1) * LANES].T,
                              jnp.ones((ATTN_SUM_ROWS, tk), BF16)], axis=0)
        for c in range(2):
            n = 2 * u + c
            s = scores[n]
            m_prev = m_scr[n]
            m_new = jnp.maximum(m_prev, jnp.max(s, axis=0, keepdims=True) - cbs[u])
            alpha = jnp.exp(m_prev - m_new)
            p = jnp.exp((s - (m_new + cbs[u])).astype(BF16))
            acc_scr[n] = alpha * acc_scr[n] + jnp.dot(vt, p, preferred_element_type=F32)
            m_scr[n] = m_new


def _attn_kernel(it_ref, jt_ref, slope_ref, q_ref, k_ref, v_ref, eq_ref, ek_ref, lam_ref, dn_ref, o_ref,
                 m_scr, acc_scr, *, tq, tk):
    hp = pl.program_id(1)
    step = pl.program_id(2)
    i = it_ref[step]
    j = jt_ref[step]
    ratio = tq // tk

    @pl.when(j == 0)
    def _():
        m_scr[...] = jnp.full(m_scr.shape, NEG, F32)
        acc_scr[...] = jnp.zeros(acc_scr.shape, F32)

    off = i * tq - j * tk
    cbs = [slope_ref[hp * ATTN_HEADS + u] * off.astype(F32) for u in range(ATTN_HEADS)]
    args = (q_ref, k_ref, v_ref, eq_ref, ek_ref, m_scr, acc_scr, off, cbs)

    @pl.when(j < i * ratio)
    def _():
        _attn_block(*args, masked=False, tq=tq, tk=tk)

    @pl.when(j >= i * ratio)
    def _():
        _attn_block(*args, masked=True, tq=tq, tk=tk)

    @pl.when(j == (i + 1) * ratio - 1)
    def _():
        lv = lam_ref[...]
        lam = (jnp.exp(jnp.sum(lv[0:1] * lv[1:2], axis=-1, keepdims=True))
               - jnp.exp(jnp.sum(lv[2:3] * lv[3:4], axis=-1, keepdims=True)) + LAMBDA_INIT)
        vd = 2 * DA_HD
        for u in range(ATTN_HEADS):
            a0, a1 = acc_scr[2 * u], acc_scr[2 * u + 1]
            o = a0[0:vd] / a0[vd:vd + 1] - lam * (a1[0:vd] / a1[vd:vd + 1])
            o = o * lax.rsqrt(jnp.mean(o * o, axis=0, keepdims=True) + EPS)
            o_ref[0, :, u * LANES:(u + 1) * LANES] = (o.T * dn_ref[...] * (1.0 - LAMBDA_INIT)).astype(BF16)


def _alibi_columns(n, slopes, q_side):
    assert n <= 256 * 256 and all(math.log2(s).is_integer() for s in slopes)
    pos = np.arange(n)
    lo, hi = (pos % 256).astype(np.float64), (pos // 256 * 256).astype(np.float64)
    out = np.zeros((len(slopes), n, LANES), np.float64)
    for h, s in enumerate(slopes):
        cols = (-s * lo, -s * hi, np.ones(n), np.ones(n)) if q_side else (np.ones(n), np.ones(n), s * lo, s * hi)
        for base in (0, DA_HD):
            for c, v in enumerate(cols):
                out[h, :, base + c] = v
    return jnp.asarray(out, dtype=BF16)


def _attn(p, lamv, diff_norm, tq, tk):
    B, S, _ = p.shape
    nq, ratio = S // tq, tq // tk
    slopes = [2.0 ** (-8.0 * (h + 1) / DA_HEADS) for h in range(DA_HEADS)]
    steps = [(i, j) for i in range(nq) for j in range((i + 1) * ratio)]
    it = jnp.asarray([s[0] for s in steps], jnp.int32)
    jt = jnp.asarray([s[1] for s in steps], jnp.int32)
    hw = ATTN_HEADS * LANES
    qb, kb, vb = COL_DAQ // hw, COL_DAK // hw, COL_DAV // hw
    grid_spec = pltpu.PrefetchScalarGridSpec(
        num_scalar_prefetch=3,
        grid=(B, DA_HEADS // ATTN_HEADS, len(steps)),
        in_specs=[pl.BlockSpec((1, tq, hw), lambda b, h, s, it, jt, sl: (b, it[s], qb + h)),
                  pl.BlockSpec((1, tk, hw), lambda b, h, s, it, jt, sl: (b, jt[s], kb + h)),
                  pl.BlockSpec((1, tk, hw), lambda b, h, s, it, jt, sl: (b, jt[s], vb + h)),
                  pl.BlockSpec((ATTN_HEADS, tq, LANES), lambda b, h, s, it, jt, sl: (h, 0, 0)),
                  pl.BlockSpec((ATTN_HEADS, tk, LANES), lambda b, h, s, it, jt, sl: (h, 0, 0)),
                  pl.BlockSpec((4, DA_HD), lambda b, h, s, it, jt, sl: (0, 0)),
                  pl.BlockSpec((1, 2 * DA_HD), lambda b, h, s, it, jt, sl: (0, 0))],
        out_specs=pl.BlockSpec((1, tq, hw), lambda b, h, s, it, jt, sl: (b, it[s], h)),
        scratch_shapes=[pltpu.VMEM((2 * ATTN_HEADS, 1, tq), F32),
                        pltpu.VMEM((2 * ATTN_HEADS, 2 * DA_HD + ATTN_SUM_ROWS, tq), F32)],
    )
    return pl.pallas_call(
        functools.partial(_attn_kernel, tq=tq, tk=tk),
        grid_spec=grid_spec,
        out_shape=jax.ShapeDtypeStruct((B, S, DA_WIDTH), BF16),
        compiler_params=_cparams(("arbitrary",) * 3),
    )(it, jt, jnp.asarray(slopes, F32), p, p, p, _alibi_columns(tq, slopes, True),
      _alibi_columns(tk, slopes, False), lamv, diff_norm.reshape(1, 2 * DA_HD))


def _mlstm_kernel(qk_ref, v_ref, o_ref, g_ref, cw_ref, cb_ref, nw_ref, y_ref,
                  ext_scr, c_scr, n_scr, m_scr, *, L):
    i = pl.program_id(1)
    hist = SUBLANES

    @pl.when(i == 0)
    def _():
        ext_scr[0:hist, :] = jnp.zeros((hist, 2 * ML_WIDTH), F32)
        c_scr[...] = jnp.zeros(c_scr.shape, F32)
        n_scr[...] = jnp.zeros(n_scr.shape, F32)
        m_scr[...] = jnp.zeros(m_scr.shape, F32)

    raw = qk_ref[0].astype(F32)
    ext_scr[hist:hist + L, :] = raw
    conv = cb_ref[...] + jnp.zeros((L, 2 * ML_WIDTH), F32)
    for j in range(CONV_W):
        off = hist - (CONV_W - 1) + j
        conv = conv + ext_scr[off:off + L, :] * cw_ref[j:j + 1, :]
    ext_scr[0:hist, :] = raw[L - hist:L, :]
    qkc = conv * _sigmoid(conv)

    gts = g_ref[0]
    fpre = pltpu.roll(gts, LANES - ML_HEADS, axis=1)
    lf = jnp.minimum(fpre, 0.0) - jnp.log(1.0 + jnp.exp(-jnp.abs(fpre)))
    row = lax.broadcasted_iota(jnp.int32, (L, L), 0)
    col = lax.broadcasted_iota(jnp.int32, (L, L), 1)
    causal = col <= row
    bcum = jnp.dot(causal.astype(F32), lf, precision=lax.Precision.HIGHEST, preferred_element_type=F32)
    r = gts - bcum
    rt = r.T
    m_all = m_scr[...]
    lane = lax.broadcasted_iota(jnp.int32, (1, LANES), 1)
    m_next = m_all
    v_all = v_ref[0]
    o_all = o_ref[0]
    for h in range(ML_HEADS):
        hs = slice(h * ML_HD, (h + 1) * ML_HD)
        bcol = bcum[:, h:h + 1]
        rcol = r[:, h:h + 1]
        rrow = rt[h:h + 1, :]
        g = bcum[L - 1:L, h:h + 1]
        mh = m_all[:, h:h + 1]
        dm = jnp.where(causal, bcol + rrow, NEG)
        inter = bcol + mh
        mj = jnp.maximum(inter, jnp.max(dm, axis=-1, keepdims=True))
        w_intra = jnp.exp(dm - mj)
        w_inter = jnp.exp(inter - mj)
        qh = qkc[:, hs]
        kh = qkc[:, ML_WIDTH + h * ML_HD:ML_WIDTH + (h + 1) * ML_HD] * (ML_HD ** -0.5)
        vh = v_all[:, hs]
        qb = qh.astype(BF16)
        kb = kh.astype(BF16)
        s = lax.dot_general(qb, kb, (((1,), (1,)), ((), ())), preferred_element_type=F32) * w_intra
        c_old = c_scr[h]
        n_old = n_scr[h:h + 1, :]
        num = (jnp.dot(s.astype(BF16), vh, preferred_element_type=F32)
               + lax.dot_general(qb, c_old.astype(BF16), (((1,), (1,)), ((), ())),
                                 preferred_element_type=F32) * w_inter)
        den = (jnp.sum(s, axis=-1, keepdims=True)
               + w_inter * jnp.sum(qh * n_old, axis=-1, keepdims=True))
        denom = jnp.maximum(jnp.abs(den), jnp.exp(-mj))
        ht = num / denom
        a_col = g + rcol
        m_new = jnp.maximum(g + mh, jnp.max(a_col, axis=0, keepdims=True))
        wa = jnp.exp(a_col - m_new)
        decay = jnp.exp(g + mh - m_new)
        vw_t = (vh.astype(F32) * wa).T.astype(BF16)
        c_scr[h] = decay * c_old + jnp.dot(vw_t, kb, preferred_element_type=F32)
        n_scr[h:h + 1, :] = decay * n_old + jnp.sum(kh * wa, axis=0, keepdims=True)
        m_next = jnp.where(lane == h, m_new, m_next)
        z = _sigmoid(o_all[:, hs].astype(F32)) * ht
        z = z * lax.rsqrt(jnp.mean(z * z, axis=-1, keepdims=True) + EPS) * nw_ref[...]
        y_ref[0, :, hs] = z.astype(BF16)
    m_scr[...] = m_next


def _mlstm(p, gates, conv_w, conv_b, mlstm_norm, L):
    B, S, _ = p.shape
    return pl.pallas_call(
        functools.partial(_mlstm_kernel, L=L),
        grid=(B, S // L),
        in_specs=[pl.BlockSpec((1, L, 2 * ML_WIDTH), lambda b, i: (b, i, COL_MLQK // (2 * ML_WIDTH))),
                  pl.BlockSpec((1, L, ML_WIDTH), lambda b, i: (b, i, COL_MLV // ML_WIDTH)),
                  pl.BlockSpec((1, L, ML_WIDTH), lambda b, i: (b, i, COL_MLO // ML_WIDTH)),
                  pl.BlockSpec((1, L, LANES), lambda b, i: (b, i, 0)),
                  pl.BlockSpec((CONV_W, 2 * ML_WIDTH), lambda b, i: (0, 0)),
                  pl.BlockSpec((1, 2 * ML_WIDTH), lambda b, i: (0, 0)),
                  pl.BlockSpec((1, ML_HD), lambda b, i: (0, 0))],
        out_specs=pl.BlockSpec((1, L, ML_WIDTH), lambda b, i: (b, i, 0)),
        out_shape=jax.ShapeDtypeStruct((B, S, ML_WIDTH), BF16),
        scratch_shapes=[pltpu.VMEM((SUBLANES + L, 2 * ML_WIDTH), F32),
                        pltpu.VMEM((ML_HEADS, ML_HD, ML_HD), F32),
                        pltpu.VMEM((SUBLANES, ML_HD), F32),
                        pltpu.VMEM((1, LANES), F32)],
        compiler_params=_cparams(("arbitrary", "arbitrary")),
    )(p, p, p, gates, conv_w, conv_b.reshape(1, -1), mlstm_norm.reshape(1, ML_HD))


def _post_kernel(ya_ref, ym_ref, ga_ref, gm_ref, x_ref, mod_ref, wa_ref, wm_ref, wo_ref, n2_ref,
                 wr_ref, br_ref, x1_ref, h2_ref, ri_ref, rw_ref, cnt_ref, run_scr, *, tm):
    first = jnp.logical_and(pl.program_id(0) == 0, pl.program_id(1) == 0)

    @pl.when(first)
    def _():
        run_scr[...] = jnp.zeros(run_scr.shape, F32)

    a = jnp.dot(ya_ref[0], wa_ref[...], preferred_element_type=F32)
    m = jnp.dot(ym_ref[0], wm_ref[...], preferred_element_type=F32)
    merged = _sigmoid(ga_ref[0]).astype(F32) * a + _sigmoid(gm_ref[0]).astype(F32) * m
    o = jnp.dot(merged.astype(BF16), wo_ref[...], preferred_element_type=F32)
    x1 = x_ref[0] + mod_ref[0, 2:3, :] * o
    x1_ref[0] = x1
    h2 = x1 * lax.rsqrt(jnp.mean(x1 * x1, axis=-1, keepdims=True) + EPS) * n2_ref[...]
    h2 = h2 * (1.0 + mod_ref[0, 4:5, :]) + mod_ref[0, 3:4, :]
    h2_ref[0] = h2

    hi = h2.astype(BF16)
    lo = (h2 - hi.astype(F32)).astype(BF16)
    logits = lax.dot_general(wr_ref[...], jnp.concatenate([hi, lo, hi], axis=1), (((1,), (1,)), ((), ())),
                             preferred_element_type=F32) + br_ref[...]
    sub = lax.broadcasted_iota(jnp.int32, (ROUTE_ROWS, tm), 0)
    big = jnp.int32(4 * ROUTE_ROWS)
    gl = jnp.where(sub < N_GROUPS, logits, NEG)
    gmax = jnp.max(gl, axis=0, keepdims=True)
    gsel = jnp.min(jnp.where(gl == gmax, sub, big), axis=0, keepdims=True)
    pgrp = 1.0 / jnp.sum(jnp.exp(gl - gmax), axis=0, keepdims=True)
    first = N_GROUPS + EXPERTS_PER_GROUP * gsel
    el = jnp.where(jnp.logical_and(sub >= first, sub < first + EXPERTS_PER_GROUP), logits, NEG)
    e1 = jnp.max(el, axis=0, keepdims=True)
    i1 = jnp.min(jnp.where(el == e1, sub, big), axis=0, keepdims=True)
    el2 = jnp.where(sub == i1, NEG, el)
    e2 = jnp.max(el2, axis=0, keepdims=True)
    i2 = jnp.min(jnp.where(el2 == e2, sub, big), axis=0, keepdims=True)
    tt = jnp.exp(e2 - e1)
    w1 = pgrp / (1.0 + tt)
    w2 = pgrp * tt / (1.0 + tt)

    oh1 = jnp.where(sub == i1, 1.0, 0.0).astype(F32)
    oh2 = jnp.where(sub == i2, 1.0, 0.0).astype(F32)
    cat = jnp.concatenate([oh1, oh2], axis=0).astype(BF16)
    src = lax.broadcasted_iota(jnp.int32, (tm, tm), 0)
    dst = lax.broadcasted_iota(jnp.int32, (tm, tm), 1)
    before = jnp.where(src < dst, 1.0, 0.0).astype(BF16)
    earlier = jnp.dot(cat, before, preferred_element_type=F32)
    c1 = jnp.sum(oh1, axis=1, keepdims=True)
    c2 = jnp.sum(oh2, axis=1, keepdims=True)
    run = run_scr[...]
    rank1 = jnp.sum((earlier[:ROUTE_ROWS] + run) * oh1, axis=0, keepdims=True)
    rank2 = jnp.sum((earlier[ROUTE_ROWS:] + run + c1) * oh2, axis=0, keepdims=True)
    run_new = run + c1 + c2
    run_scr[...] = run_new
    cnt_ref[...] = jnp.broadcast_to(run_new, cnt_ref.shape)
    row = lax.broadcasted_iota(jnp.int32, (SUBLANES, tm), 0)
    ri_ref[0] = jnp.where(row == 0, i1 - N_GROUPS,
                          jnp.where(row == 1, i2 - N_GROUPS,
                                    jnp.where(row == 2, rank1.astype(jnp.int32),
                                              jnp.where(row == 3, rank2.astype(jnp.int32), 0))))
    rw_ref[0] = jnp.where(row == 0, w1, jnp.where(row == 1, w2, 0.0))


def _post(ya, ym, p, x, mod, wa, wm, wo, norm2, wr, br, tm):
    B, S, D = x.shape
    tok = lambda b, i: (b, i, 0)
    const = lambda b, i: (0, 0)
    return pl.pallas_call(
        functools.partial(_post_kernel, tm=tm),
        grid=(B, S // tm),
        in_specs=[pl.BlockSpec((1, tm, DA_WIDTH), tok),
                  pl.BlockSpec((1, tm, ML_WIDTH), tok),
                  pl.BlockSpec((1, tm, D), lambda b, i: (b, i, COL_GA // D_MODEL)),
                  pl.BlockSpec((1, tm, D), lambda b, i: (b, i, COL_GM // D_MODEL)),
                  pl.BlockSpec((1, tm, D), tok),
                  pl.BlockSpec((1, 6, D), lambda b, i: (b, 0, 0)),
                  pl.BlockSpec((DA_WIDTH, D), const),
                  pl.BlockSpec((ML_WIDTH, D), const),
                  pl.BlockSpec((D, D), const),
                  pl.BlockSpec((1, D), const),
                  pl.BlockSpec((ROUTE_ROWS, 3 * D), const),
                  pl.BlockSpec((ROUTE_ROWS, 1), const)],
        out_specs=[pl.BlockSpec((1, tm, D), tok),
                   pl.BlockSpec((1, tm, D), tok),
                   pl.BlockSpec((1, SUBLANES, tm), lambda b, i: (b, 0, i)),
                   pl.BlockSpec((1, SUBLANES, tm), lambda b, i: (b, 0, i)),
                   pl.BlockSpec((ROUTE_ROWS, LANES), const)],
        out_shape=[jax.ShapeDtypeStruct((B, S, D), F32),
                   jax.ShapeDtypeStruct((B, S, D), F32),
                   jax.ShapeDtypeStruct((B, SUBLANES, S), jnp.int32),
                   jax.ShapeDtypeStruct((B, SUBLANES, S), F32),
                   jax.ShapeDtypeStruct((ROUTE_ROWS, LANES), F32)],
        scratch_shapes=[pltpu.VMEM((ROUTE_ROWS, 1), F32)],
        compiler_params=_cparams(("arbitrary", "arbitrary")),
    )(ya, ym, p, p, x, mod, wa, wm, wo, norm2.reshape(1, D), wr, br)


DMA_UNROLL = 8


def _dispatch_kernel(ends_ref, pc_ref, dest_ref, h2_ref, xs_hbm, zbuf, sem, zsem, *, td):
    i = pl.program_id(0)

    @pl.when(i == 0)
    def _():
        zbuf[...] = jnp.zeros(zbuf.shape, zbuf.dtype)
        for e in range(N_EXPERTS):
            @pl.when(pc_ref[e] > 0)
            def _():
                start = pl.multiple_of(ends_ref[e] - EXPERT_ROWS, EXPERT_ROWS)
                cp = pltpu.make_async_copy(zbuf, xs_hbm.at[pl.ds(start, EXPERT_ROWS)], zsem)
                cp.start()
                cp.wait()

        def fill(tile, carry):
            start = pl.multiple_of(tile * EXPERT_ROWS, EXPERT_ROWS)
            cp = pltpu.make_async_copy(zbuf, xs_hbm.at[pl.ds(start, EXPERT_ROWS)], zsem)
            cp.start()
            cp.wait()
            return carry

        lax.fori_loop(ends_ref[N_EXPERTS - 1] // EXPERT_ROWS, xs_hbm.shape[0] // EXPERT_ROWS, fill, 0)

    def issue(t, carry):
        for s in range(2):
            pltpu.make_async_copy(h2_ref.at[pl.ds(t, 1)], xs_hbm.at[pl.ds(dest_ref[2 * t + s], 1)], sem).start()
        return carry

    lax.fori_loop(0, td, issue, 0, unroll=DMA_UNROLL)
    for s in range(2):
        pltpu.make_async_copy(h2_ref, xs_hbm.at[pl.ds(0, td)], sem).wait()


def _dispatch(ends, pc, dest_flat, h2, n_rows, td):
    T, D = h2.shape
    grid_spec = pltpu.PrefetchScalarGridSpec(
        num_scalar_prefetch=2,
        grid=(T // td,),
        in_specs=[pl.BlockSpec((2 * td,), lambda i, e, c: (i,), memory_space=pltpu.SMEM),
                  pl.BlockSpec((td, D), lambda i, e, c: (i, 0))],
        out_specs=pl.BlockSpec(memory_space=pl.ANY),
        scratch_shapes=[pltpu.VMEM((EXPERT_ROWS, D), h2.dtype),
                        pltpu.SemaphoreType.DMA(()), pltpu.SemaphoreType.DMA(())],
    )
    return pl.pallas_call(
        functools.partial(_dispatch_kernel, td=td),
        grid_spec=grid_spec,
        out_shape=jax.ShapeDtypeStruct((n_rows, D), h2.dtype),
        compiler_params=_cparams(("arbitrary",)),
    )(ends, pc, dest_flat, h2)


def _experts_kernel(te_ref, nt_ref, xs_ref, w1_ref, w3_ref, w2_ref, ys_ref):
    i = pl.program_id(0)

    @pl.when(i < nt_ref[0])
    def _():
        x = xs_ref[...].astype(BF16)
        a = jnp.dot(x, w1_ref[0].astype(BF16), preferred_element_type=F32)
        b = jnp.dot(x, w3_ref[0].astype(BF16), preferred_element_type=F32)
        hid = (a * _sigmoid(a) * b).astype(BF16)
        ys_ref[...] = jnp.dot(hid, w2_ref[0].astype(BF16), preferred_element_type=F32)

    @pl.when(i >= nt_ref[0])
    def _():
        ys_ref[...] = jnp.zeros(ys_ref.shape, ys_ref.dtype)


def _experts(tile_e, n_tiles, xs, w1, w3, w2):
    n_rows = xs.shape[0]
    D = D_MODEL
    nt = n_rows // EXPERT_ROWS
    rows = lambda i, te, n: (jnp.minimum(i, jnp.maximum(n[0] - 1, 0)), 0)
    wsel = lambda i, te, n: (te[i], 0, 0)
    grid_spec = pltpu.PrefetchScalarGridSpec(
        num_scalar_prefetch=2,
        grid=(nt,),
        in_specs=[pl.BlockSpec((EXPERT_ROWS, D), rows),
                  pl.BlockSpec((1, D, D_EXPERT), wsel),
                  pl.BlockSpec((1, D, D_EXPERT), wsel),
                  pl.BlockSpec((1, D_EXPERT, D), wsel)],
        out_specs=pl.BlockSpec((EXPERT_ROWS, D), lambda i, te, n: (i, 0)),
    )
    return pl.pallas_call(
        _experts_kernel,
        grid_spec=grid_spec,
        out_shape=jax.ShapeDtypeStruct((n_rows, D), F32),
        compiler_params=_cparams(("arbitrary",)),
    )(tile_e, n_tiles, xs, w1, w3, w2)


def _combine_kernel(dcur_ref, dnxt_ref, ys_hbm, x1_ref, rw_ref, mod_ref, nf_ref, o_ref, ybuf, sem, *, tc, nsteps):
    g = pl.program_id(0)
    slot = g % 2

    def gather(dest_ref, sl):
        def issue(t, carry):
            for s in range(2):
                pltpu.make_async_copy(ys_hbm.at[pl.ds(dest_ref[2 * t + s], 1)],
                                      ybuf.at[sl, s, pl.ds(t, 1)], sem.at[sl]).start()
            return carry

        lax.fori_loop(0, tc, issue, 0, unroll=DMA_UNROLL)

    @pl.when(g == 0)
    def _():
        gather(dcur_ref, 0)

    @pl.when(g + 1 < nsteps)
    def _():
        gather(dnxt_ref, 1 - slot)

    for s in range(2):
        pltpu.make_async_copy(ys_hbm.at[pl.ds(0, tc)], ybuf.at[slot, s], sem.at[slot]).wait()

    rw = jnp.concatenate([rw_ref[0], jnp.zeros((LANES - SUBLANES, tc), F32)], axis=0).T
    y = rw[:, 0:1] * ybuf[slot, 0] + rw[:, 1:2] * ybuf[slot, 1]
    xo = x1_ref[0] + mod_ref[0, 5:6, :] * y
    o_ref[0] = xo * lax.rsqrt(jnp.mean(xo * xo, axis=-1, keepdims=True) + EPS) * nf_ref[...]


def _combine(dest_flat, ys, x1, rw, mod, norm_f, tc):
    B, S, D = x1.shape
    n = S // tc
    nsteps = B * n
    tok = lambda g: (g // n, g % n, 0)
    return pl.pallas_call(
        functools.partial(_combine_kernel, tc=tc, nsteps=nsteps),
        grid=(nsteps,),
        in_specs=[pl.BlockSpec((2 * tc,), lambda g: (g,), memory_space=pltpu.SMEM),
                  pl.BlockSpec((2 * tc,), lambda g: (jnp.minimum(g + 1, nsteps - 1),), memory_space=pltpu.SMEM),
                  pl.BlockSpec(memory_space=pl.ANY),
                  pl.BlockSpec((1, tc, D), tok),
                  pl.BlockSpec((1, SUBLANES, tc), lambda g: (g // n, 0, g % n)),
                  pl.BlockSpec((1, 6, D), lambda g: (g // n, 0, 0)),
                  pl.BlockSpec((1, D), lambda g: (0, 0))],
        out_specs=pl.BlockSpec((1, tc, D), tok),
        out_shape=jax.ShapeDtypeStruct((B, S, D), F32),
        scratch_shapes=[pltpu.VMEM((2, 2, tc, D), F32), pltpu.SemaphoreType.DMA((2,))],
        compiler_params=_cparams(("arbitrary",)),
    )(dest_flat, dest_flat, ys, x1, rw, mod, norm_f.reshape(1, D))


def _pick(n, pref):
    t = min(n, pref)
    assert n % t == 0, (n, pref)
    return t


def kernel(x, c, w_ada, b_ada, norm1, w_in, b_if, conv_w, conv_b, lam_q1, lam_k1, lam_q2, lam_k2,
           diff_norm, mlstm_norm, w_br_a, w_br_m, w_out, norm2, w_rg, b_rg, w_re, b_re,
           w_e1, w_e3, w_e2, norm_f):
    B, S, D = x.shape
    assert D == D_MODEL and w_ada.shape[0] == 1
    T = B * S
    l = 0

    mod = _ada(c, w_ada[l], b_ada[l]).reshape(B, 6, D)

    w = w_in[l]
    o_q, o_k, o_v, o_qk, o_mv, o_mo, o_if, o_ga, o_gm = np.cumsum((0,) + (
        DA_WIDTH, DA_WIDTH, DA_WIDTH, 2 * ML_WIDTH, ML_WIDTH, ML_WIDTH, 2 * ML_HEADS, D_MODEL))
    w_perm = jnp.concatenate([
        w[:, o_qk:o_qk + 2 * ML_WIDTH], w[:, o_ga:o_ga + D], w[:, o_gm:o_gm + D],
        w[:, o_q:o_q + DA_WIDTH] * (DA_HD ** -0.5), w[:, o_k:o_k + DA_WIDTH], w[:, o_v:o_v + DA_WIDTH],
        w[:, o_mv:o_mv + ML_WIDTH], w[:, o_mo:o_mo + ML_WIDTH], w[:, o_if:o_if + 2 * ML_HEADS],
        jnp.zeros((D, LANES - 2 * ML_HEADS), F32)], axis=1).astype(BF16)
    bif = jnp.concatenate([b_if[l], jnp.zeros((LANES - 2 * ML_HEADS,), F32)]).reshape(1, LANES)

    p, gates = _proj(x, mod, norm1[l], bif, w_perm, _pick(S, PROJ_TM))

    lamv = jnp.stack([lam_q1[l], lam_k1[l], lam_q2[l], lam_k2[l]])
    ya = _attn(p, lamv, diff_norm[l], _pick(S, ATTN_TQ), _pick(S, ATTN_TK))
    ym = _mlstm(p, gates, conv_w[l], conv_b[l], mlstm_norm[l], _pick(S, MLSTM_L))

    pad_rows = ROUTE_ROWS - N_GROUPS - N_EXPERTS
    wr = jnp.concatenate([w_rg[l], w_re[l], jnp.zeros((D, pad_rows), F32)], axis=1).T
    wr_hi = wr.astype(BF16)
    wr_lo = (wr - wr_hi.astype(F32)).astype(BF16)
    wr = jnp.concatenate([wr_hi, wr_hi, wr_lo], axis=1)
    br = jnp.concatenate([b_rg[l], b_re[l], jnp.zeros((pad_rows,), F32)]).reshape(ROUTE_ROWS, 1)
    x1, h2, ri, rw, cnt = _post(ya, ym, p, x, mod, w_br_a[l].astype(BF16), w_br_m[l].astype(BF16),
                                w_out[l].astype(BF16), norm2[l], wr, br, _pick(S, POST_TM))

    counts = cnt[N_GROUPS:N_GROUPS + N_EXPERTS, 0].astype(jnp.int32)
    pc = ((counts + EXPERT_ROWS - 1) // EXPERT_ROWS) * EXPERT_ROWS
    ends = jnp.cumsum(pc)
    offs = ends - pc
    eid = ri[:, 0:2, :]
    rank = ri[:, 2:4, :]
    dest = rank + jnp.sum(jnp.where(eid[..., None] == jnp.arange(N_EXPERTS), offs, 0), axis=-1)
    dest_flat = dest.transpose(0, 2, 1).reshape(2 * T).astype(jnp.int32)
    n_rows = 2 * T + N_EXPERTS * EXPERT_ROWS
    n_tiles = n_rows // EXPERT_ROWS
    tile_e = jnp.sum(jnp.arange(n_tiles)[:, None] * EXPERT_ROWS >= ends[None, :], axis=1)
    tile_e = jnp.minimum(tile_e, N_EXPERTS - 1).astype(jnp.int32)
    used_tiles = (ends[-1:] // EXPERT_ROWS).astype(jnp.int32)

    xs = _dispatch(ends.astype(jnp.int32), pc.astype(jnp.int32), dest_flat, h2.reshape(T, D), n_rows,
                   _pick(T, SMEM_BLOCK_1D))
    w1 = w_e1[l].reshape(N_EXPERTS, D, D_EXPERT)
    w3 = w_e3[l].reshape(N_EXPERTS, D, D_EXPERT)
    w2 = w_e2[l].reshape(N_EXPERTS, D_EXPERT, D)
    ys = _experts(tile_e, used_tiles, xs, w1, w3, w2)
    return _combine(dest_flat, ys, x1, rw, mod, norm_f, _pick(S, SMEM_BLOCK_1D // 2))
```

```python
import functools
import math

import jax
import jax.numpy as jnp
import numpy as np
from jax import lax
from jax.experimental import pallas as pl
from jax.experimental.pallas import tpu as pltpu

F32 = jnp.float32
BF16 = jnp.bfloat16

D_MODEL = 1024
DA_HEADS = 4
DA_HD = 64
DA_WIDTH = DA_HEADS * 2 * DA_HD
ML_HEADS = 4
ML_HD = 128
ML_WIDTH = ML_HEADS * ML_HD
CONV_W = 4
N_GROUPS = 4
EXPERTS_PER_GROUP = 8
N_EXPERTS = N_GROUPS * EXPERTS_PER_GROUP
D_EXPERT = 256
EPS = 1e-6
LAMBDA_INIT = 0.8 - 0.6 * math.exp(-0.3 * 0)

LANES = 128
SUBLANES = 8
SMEM_BLOCK_1D = 1024
NEG = -1e30
VMEM_LIMIT = 56 * 1024 * 1024

COL_MLQK = 0
COL_GA = 1024
COL_GM = 2048
COL_DAQ = 3072
COL_DAK = 3584
COL_DAV = 4096
COL_MLV = 4608
COL_MLO = 5120
COL_IF = 5632
N_PROJ = COL_IF + LANES
PROJ_TN = 1920
assert N_PROJ % PROJ_TN == 0

EXPERT_ROWS = 256
PROJ_TM = 1024
ATTN_TQ = 1024
ATTN_TK = 512
MLSTM_L = 256
POST_TM = 512
ROUTE_ROWS = 48


def _cparams(sem):
    return pltpu.CompilerParams(dimension_semantics=sem, vmem_limit_bytes=VMEM_LIMIT)


ROW_PARTS = D_MODEL // LANES
assert ROW_PARTS == SUBLANES


def _store_rows(ref, lead, x):
    n = x.shape[0]
    for a in range(ROW_PARTS):
        ref[lead + (pl.ds(a, n, stride=ROW_PARTS), slice(None))] = x[:, a * LANES:(a + 1) * LANES]


def _load_rows(ref, lead, n):
    return jnp.concatenate([ref[lead + (pl.ds(a, n, stride=ROW_PARTS), slice(None))] for a in range(ROW_PARTS)],
                           axis=1)


def _sigmoid(x):
    return 0.5 * jnp.tanh(0.5 * x) + 0.5


def _ada_kernel(c_ref, w_ref, b_ref, o_ref):
    c = c_ref[...]
    cs = c * _sigmoid(c)
    o_ref[...] = jnp.dot(cs, w_ref[...], precision=lax.Precision.HIGHEST,
                         preferred_element_type=F32) + b_ref[...]


def _ada(c, w, b):
    B, D = c.shape
    N = w.shape[1]
    tn = 1536
    return pl.pallas_call(
        _ada_kernel,
        grid=(N // tn,),
        in_specs=[pl.BlockSpec((B, D), lambda j: (0, 0)),
                  pl.BlockSpec((D, tn), lambda j: (0, j)),
                  pl.BlockSpec((1, tn), lambda j: (0, j))],
        out_specs=pl.BlockSpec((B, tn), lambda j: (0, j)),
        out_shape=jax.ShapeDtypeStruct((B, N), F32),
        compiler_params=_cparams(("arbitrary",)),
    )(c, w, b.reshape(1, N))


def _proj_kernel(x_ref, mod_ref, g_ref, bif_ref, w_ref, p_ref, gate_ref, h_scr, *, nj):
    j = pl.program_id(2)

    @pl.when(j == 0)
    def _():
        x = x_ref[0]
        y = x * lax.rsqrt(jnp.mean(x * x, axis=-1, keepdims=True) + EPS) * g_ref[...]
        h = y * (1.0 + mod_ref[0, 1:2, :]) + mod_ref[0, 0:1, :]
        h_scr[...] = h.astype(BF16)

    acc = jnp.dot(h_scr[...], w_ref[...], preferred_element_type=F32)
    p_ref[0] = acc.astype(BF16)

    @pl.when(j == nj - 1)
    def _():
        gate_ref[0] = acc[:, PROJ_TN - LANES:] + bif_ref[...]


def _proj(x, mod, norm1, bif, w_bf16, tm):
    B, S, D = x.shape
    nj = N_PROJ // PROJ_TN
    return pl.pallas_call(
        functools.partial(_proj_kernel, nj=nj),
        grid=(B, S // tm, nj),
        in_specs=[pl.BlockSpec((1, tm, D), lambda b, i, j: (b, i, 0)),
                  pl.BlockSpec((1, 6, D), lambda b, i, j: (b, 0, 0)),
                  pl.BlockSpec((1, D), lambda b, i, j: (0, 0)),
                  pl.BlockSpec((1, LANES), lambda b, i, j: (0, 0)),
                  pl.BlockSpec((D, PROJ_TN), lambda b, i, j: (0, j))],
        out_specs=[pl.BlockSpec((1, tm, PROJ_TN), lambda b, i, j: (b, i, j)),
                   pl.BlockSpec((1, tm, LANES), lambda b, i, j: (b, i, 0))],
        out_shape=[jax.ShapeDtypeStruct((B, S, N_PROJ), BF16),
                   jax.ShapeDtypeStruct((B, S, LANES), F32)],
        scratch_shapes=[pltpu.VMEM((tm, D), BF16)],
        compiler_params=_cparams(("arbitrary", "arbitrary", "arbitrary")),
    )(x, mod, norm1.reshape(1, D), bif, w_bf16)


ATTN_SUM_ROWS = 16

ATTN_HEADS = 2


def _attn_block(q_ref, k_ref, v_ref, eq_ref, ek_ref, m_scr, acc_scr, off, cbs, masked, q0, tq, tk):
    qs = slice(q0, tq)
    lane = lax.broadcasted_iota(jnp.int32, (1, LANES), 1)
    if masked:
        key = lax.broadcasted_iota(jnp.int32, (tk, tq - q0), 0)
        qry = lax.broadcasted_iota(jnp.int32, (tk, tq - q0), 1) + q0
        valid = (key - qry) <= off
    scores = []
    for u in range(ATTN_HEADS):
        hl = slice(u * LANES, (u + 1) * LANES)
        q = q_ref[0, qs, hl]
        k = k_ref[0, :, hl]
        for c in range(2):
            sel = (lane < DA_HD) if c == 0 else (lane >= DA_HD)
            s = lax.dot_general(jnp.where(sel, k, ek_ref[u]), jnp.where(sel, q, eq_ref[u, qs]),
                                (((1,), (1,)), ((), ())), preferred_element_type=F32)
            scores.append(jnp.where(valid, s, NEG) if masked else s)
    for u in range(ATTN_HEADS):
        vt = jnp.concatenate([v_ref[0, :, u * LANES:(u + 1) * LANES].T,
                              jnp.ones((ATTN_SUM_ROWS, tk), BF16)], axis=0)
        for c in range(2):
            n = 2 * u + c
            s = scores[n]
            m_prev = m_scr[n, :, qs]
            m_new = jnp.maximum(m_prev, jnp.max(s, axis=0, keepdims=True) - cbs[u])
            alpha = jnp.exp(m_prev - m_new)
            p = jnp.exp((s - (m_new + cbs[u])).astype(BF16))
            acc_scr[n, :, qs] = alpha * acc_scr[n, :, qs] + jnp.dot(vt, p, preferred_element_type=F32)
            m_scr[n, :, qs] = m_new


def _attn_kernel(it_ref, jt_ref, slope_ref, q_ref, k_ref, v_ref, eq_ref, ek_ref, lam_ref, dn_ref, o_ref,
                 m_scr, acc_scr, *, tq, tk):
    hp = pl.program_id(1)
    step = pl.program_id(2)
    i = it_ref[step]
    j = jt_ref[step]
    ratio = tq // tk

    @pl.when(j == 0)
    def _():
        m_scr[...] = jnp.full(m_scr.shape, NEG, F32)
        acc_scr[...] = jnp.zeros(acc_scr.shape, F32)

    off = i * tq - j * tk
    cbs = [slope_ref[hp * ATTN_HEADS + u] * off.astype(F32) for u in range(ATTN_HEADS)]
    args = (q_ref, k_ref, v_ref, eq_ref, ek_ref, m_scr, acc_scr, off, cbs)

    @pl.when(j < i * ratio)
    def _():
        _attn_block(*args, masked=False, q0=0, tq=tq, tk=tk)

    for r in range(ratio):
        @pl.when(j == i * ratio + r)
        def _():
            _attn_block(*args, masked=True, q0=r * tk, tq=tq, tk=tk)

    @pl.when(j == (i + 1) * ratio - 1)
    def _():
        lv = lam_ref[...]
        lam = (jnp.exp(jnp.sum(lv[0:1] * lv[1:2], axis=-1, keepdims=True))
               - jnp.exp(jnp.sum(lv[2:3] * lv[3:4], axis=-1, keepdims=True)) + LAMBDA_INIT)
        vd = 2 * DA_HD
        for u in range(ATTN_HEADS):
            a0, a1 = acc_scr[2 * u], acc_scr[2 * u + 1]
            o = a0[0:vd] / a0[vd:vd + 1] - lam * (a1[0:vd] / a1[vd:vd + 1])
            o = o * lax.rsqrt(jnp.mean(o * o, axis=0, keepdims=True) + EPS)
            o_ref[0, :, u * LANES:(u + 1) * LANES] = (o.T * dn_ref[...] * (1.0 - LAMBDA_INIT)).astype(BF16)


def _alibi_columns(n, slopes, q_side):
    assert n <= 256 * 256 and all(math.log2(s).is_integer() for s in slopes)
    pos = np.arange(n)
    lo, hi = (pos % 256).astype(np.float64), (pos // 256 * 256).astype(np.float64)
    out = np.zeros((len(slopes), n, LANES), np.float64)
    for h, s in enumerate(slopes):
        cols = (-s * lo, -s * hi, np.ones(n), np.ones(n)) if q_side else (np.ones(n), np.ones(n), s * lo, s * hi)
        for base in (0, DA_HD):
            for c, v in enumerate(cols):
                out[h, :, base + c] = v
    return jnp.asarray(out, dtype=BF16)


def _attn(p, lamv, diff_norm, tq, tk):
    B, S, _ = p.shape
    nq, ratio = S // tq, tq // tk
    slopes = [2.0 ** (-8.0 * (h + 1) / DA_HEADS) for h in range(DA_HEADS)]
    steps = [(i, j) for i in range(nq) for j in range((i + 1) * ratio)]
    it = jnp.asarray([s[0] for s in steps], jnp.int32)
    jt = jnp.asarray([s[1] for s in steps], jnp.int32)
    hw = ATTN_HEADS * LANES
    qb, kb, vb = COL_DAQ // hw, COL_DAK // hw, COL_DAV // hw
    grid_spec = pltpu.PrefetchScalarGridSpec(
        num_scalar_prefetch=3,
        grid=(B, DA_HEADS // ATTN_HEADS, len(steps)),
        in_specs=[pl.BlockSpec((1, tq, hw), lambda b, h, s, it, jt, sl: (b, it[s], qb + h)),
                  pl.BlockSpec((1, tk, hw), lambda b, h, s, it, jt, sl: (b, jt[s], kb + h)),
                  pl.BlockSpec((1, tk, hw), lambda b, h, s, it, jt, sl: (b, jt[s], vb + h)),
                  pl.BlockSpec((ATTN_HEADS, tq, LANES), lambda b, h, s, it, jt, sl: (h, 0, 0)),
                  pl.BlockSpec((ATTN_HEADS, tk, LANES), lambda b, h, s, it, jt, sl: (h, 0, 0)),
                  pl.BlockSpec((4, DA_HD), lambda b, h, s, it, jt, sl: (0, 0)),
                  pl.BlockSpec((1, 2 * DA_HD), lambda b, h, s, it, jt, sl: (0, 0))],
        out_specs=pl.BlockSpec((1, tq, hw), lambda b, h, s, it, jt, sl: (b, it[s], h)),
        scratch_shapes=[pltpu.VMEM((2 * ATTN_HEADS, 1, tq), F32),
                        pltpu.VMEM((2 * ATTN_HEADS, 2 * DA_HD + ATTN_SUM_ROWS, tq), F32)],
    )
    return pl.pallas_call(
        functools.partial(_attn_kernel, tq=tq, tk=tk),
        grid_spec=grid_spec,
        out_shape=jax.ShapeDtypeStruct((B, S, DA_WIDTH), BF16),
        compiler_params=_cparams(("arbitrary",) * 3),
    )(it, jt, jnp.asarray(slopes, F32), p, p, p, _alibi_columns(tq, slopes, True),
      _alibi_columns(tk, slopes, False), lamv, diff_norm.reshape(1, 2 * DA_HD))


def _mlstm_kernel(qk_ref, v_ref, o_ref, g_ref, cw_ref, cb_ref, nw_ref, y_ref,
                  ext_scr, c_scr, n_scr, m_scr, *, L):
    i = pl.program_id(1)
    hist = SUBLANES

    @pl.when(i == 0)
    def _():
        ext_scr[0:hist, :] = jnp.zeros((hist, 2 * ML_WIDTH), F32)
        c_scr[...] = jnp.zeros(c_scr.shape, F32)
        n_scr[...] = jnp.zeros(n_scr.shape, F32)
        m_scr[...] = jnp.zeros(m_scr.shape, F32)

    raw = qk_ref[0].astype(F32)
    ext_scr[hist:hist + L, :] = raw
    conv = cb_ref[...] + jnp.zeros((L, 2 * ML_WIDTH), F32)
    for j in range(CONV_W):
        off = hist - (CONV_W - 1) + j
        conv = conv + ext_scr[off:off + L, :] * cw_ref[j:j + 1, :]
    ext_scr[0:hist, :] = raw[L - hist:L, :]
    qkc = conv * _sigmoid(conv)

    gts = g_ref[0]
    fpre = pltpu.roll(gts, LANES - ML_HEADS, axis=1)
    lf = jnp.minimum(fpre, 0.0) - jnp.log(1.0 + jnp.exp(-jnp.abs(fpre)))
    row = lax.broadcasted_iota(jnp.int32, (L, L), 0)
    col = lax.broadcasted_iota(jnp.int32, (L, L), 1)
    causal = col <= row
    bcum = jnp.dot(causal.astype(F32), lf, precision=lax.Precision.HIGHEST, preferred_element_type=F32)
    r = gts - bcum
    rt = r.T
    m_all = m_scr[...]
    lane = lax.broadcasted_iota(jnp.int32, (1, LANES), 1)
    m_next = m_all
    v_all = v_ref[0]
    o_all = o_ref[0]
    for h in range(ML_HEADS):
        hs = slice(h * ML_HD, (h + 1) * ML_HD)
        bcol = bcum[:, h:h + 1]
        rcol = r[:, h:h + 1]
        rrow = rt[h:h + 1, :]
        g = bcum[L - 1:L, h:h + 1]
        mh = m_all[:, h:h + 1]
        dm = jnp.where(causal, bcol + rrow, NEG)
        inter = bcol + mh
        mj = jnp.maximum(inter, jnp.max(dm, axis=-1, keepdims=True))
        w_intra = jnp.exp(dm - mj)
        w_inter = jnp.exp(inter - mj)
        qh = qkc[:, hs]
        kh = qkc[:, ML_WIDTH + h * ML_HD:ML_WIDTH + (h + 1) * ML_HD] * (ML_HD ** -0.5)
        vh = v_all[:, hs]
        qb = qh.astype(BF16)
        kb = kh.astype(BF16)
        s = lax.dot_general(qb, kb, (((1,), (1,)), ((), ())), preferred_element_type=F32) * w_intra
        c_old = c_scr[h]
        n_old = n_scr[h:h + 1, :]
        num = (jnp.dot(s.astype(BF16), vh, preferred_element_type=F32)
               + lax.dot_general(qb, c_old.astype(BF16), (((1,), (1,)), ((), ())),
                                 preferred_element_type=F32) * w_inter)
        den = (jnp.sum(s, axis=-1, keepdims=True)
               + w_inter * jnp.sum(qh * n_old, axis=-1, keepdims=True))
        denom = jnp.maximum(jnp.abs(den), jnp.exp(-mj))
        ht = num / denom
        a_col = g + rcol
        m_new = jnp.maximum(g + mh, jnp.max(a_col, axis=0, keepdims=True))
        wa = jnp.exp(a_col - m_new)
        decay = jnp.exp(g + mh - m_new)
        vw_t = (vh.astype(F32) * wa).T.astype(BF16)
        c_scr[h] = decay * c_old + jnp.dot(vw_t, kb, preferred_element_type=F32)
        n_scr[h:h + 1, :] = decay * n_old + jnp.sum(kh * wa, axis=0, keepdims=True)
        m_next = jnp.where(lane == h, m_new, m_next)
        z = _sigmoid(o_all[:, hs].astype(F32)) * ht
        z = z * lax.rsqrt(jnp.mean(z * z, axis=-1, keepdims=True) + EPS) * nw_ref[...]
        y_ref[0, :, hs] = z.astype(BF16)
    m_scr[...] = m_next


def _mlstm(p, gates, conv_w, conv_b, mlstm_norm, L):
    B, S, _ = p.shape
    return pl.pallas_call(
        functools.partial(_mlstm_kernel, L=L),
        grid=(B, S // L),
        in_specs=[pl.BlockSpec((1, L, 2 * ML_WIDTH), lambda b, i: (b, i, COL_MLQK // (2 * ML_WIDTH))),
                  pl.BlockSpec((1, L, ML_WIDTH), lambda b, i: (b, i, COL_MLV // ML_WIDTH)),
                  pl.BlockSpec((1, L, ML_WIDTH), lambda b, i: (b, i, COL_MLO // ML_WIDTH)),
                  pl.BlockSpec((1, L, LANES), lambda b, i: (b, i, 0)),
                  pl.BlockSpec((CONV_W, 2 * ML_WIDTH), lambda b, i: (0, 0)),
                  pl.BlockSpec((1, 2 * ML_WIDTH), lambda b, i: (0, 0)),
                  pl.BlockSpec((1, ML_HD), lambda b, i: (0, 0))],
        out_specs=pl.BlockSpec((1, L, ML_WIDTH), lambda b, i: (b, i, 0)),
        out_shape=jax.ShapeDtypeStruct((B, S, ML_WIDTH), BF16),
        scratch_shapes=[pltpu.VMEM((SUBLANES + L, 2 * ML_WIDTH), F32),
                        pltpu.VMEM((ML_HEADS, ML_HD, ML_HD), F32),
                        pltpu.VMEM((SUBLANES, ML_HD), F32),
                        pltpu.VMEM((1, LANES), F32)],
        compiler_params=_cparams(("arbitrary", "arbitrary")),
    )(p, p, p, gates, conv_w, conv_b.reshape(1, -1), mlstm_norm.reshape(1, ML_HD))


def _post_kernel(ya_ref, ym_ref, ga_ref, gm_ref, x_ref, mod_ref, wa_ref, wm_ref, wo_ref, n2_ref,
                 wr_ref, br_ref, x1_ref, h2_ref, ri_ref, rw_ref, cnt_ref, run_scr, *, tm):
    first = jnp.logical_and(pl.program_id(0) == 0, pl.program_id(1) == 0)

    @pl.when(first)
    def _():
        run_scr[...] = jnp.zeros(run_scr.shape, F32)

    a = jnp.dot(ya_ref[0], wa_ref[...], preferred_element_type=F32)
    m = jnp.dot(ym_ref[0], wm_ref[...], preferred_element_type=F32)
    merged = _sigmoid(ga_ref[0]).astype(F32) * a + _sigmoid(gm_ref[0]).astype(F32) * m
    o = jnp.dot(merged.astype(BF16), wo_ref[...], preferred_element_type=F32)
    x1 = x_ref[0] + mod_ref[0, 2:3, :] * o
    x1_ref[0] = x1
    h2 = x1 * lax.rsqrt(jnp.mean(x1 * x1, axis=-1, keepdims=True) + EPS) * n2_ref[...]
    h2 = h2 * (1.0 + mod_ref[0, 4:5, :]) + mod_ref[0, 3:4, :]
    _store_rows(h2_ref, (0,), h2)

    hi = h2.astype(BF16)
    lo = (h2 - hi.astype(F32)).astype(BF16)
    logits = lax.dot_general(wr_ref[...], jnp.concatenate([hi, lo, hi], axis=1), (((1,), (1,)), ((), ())),
                             preferred_element_type=F32) + br_ref[...]
    sub = lax.broadcasted_iota(jnp.int32, (ROUTE_ROWS, tm), 0)
    big = jnp.int32(4 * ROUTE_ROWS)
    gl = jnp.where(sub < N_GROUPS, logits, NEG)
    gmax = jnp.max(gl, axis=0, keepdims=True)
    gsel = jnp.min(jnp.where(gl == gmax, sub, big), axis=0, keepdims=True)
    pgrp = 1.0 / jnp.sum(jnp.exp(gl - gmax), axis=0, keepdims=True)
    first = N_GROUPS + EXPERTS_PER_GROUP * gsel
    el = jnp.where(jnp.logical_and(sub >= first, sub < first + EXPERTS_PER_GROUP), logits, NEG)
    e1 = jnp.max(el, axis=0, keepdims=True)
    i1 = jnp.min(jnp.where(el == e1, sub, big), axis=0, keepdims=True)
    el2 = jnp.where(sub == i1, NEG, el)
    e2 = jnp.max(el2, axis=0, keepdims=True)
    i2 = jnp.min(jnp.where(el2 == e2, sub, big), axis=0, keepdims=True)
    tt = jnp.exp(e2 - e1)
    w1 = pgrp / (1.0 + tt)
    w2 = pgrp * tt / (1.0 + tt)

    oh1 = jnp.where(sub == i1, 1.0, 0.0).astype(F32)
    oh2 = jnp.where(sub == i2, 1.0, 0.0).astype(F32)
    cat = jnp.concatenate([oh1, oh2], axis=0).astype(BF16)
    src = lax.broadcasted_iota(jnp.int32, (tm, tm), 0)
    dst = lax.broadcasted_iota(jnp.int32, (tm, tm), 1)
    before = jnp.where(src < dst, 1.0, 0.0).astype(BF16)
    earlier = jnp.dot(cat, before, preferred_element_type=F32)
    c1 = jnp.sum(oh1, axis=1, keepdims=True)
    c2 = jnp.sum(oh2, axis=1, keepdims=True)
    run = run_scr[...]
    rank1 = jnp.sum((earlier[:ROUTE_ROWS] + run) * oh1, axis=0, keepdims=True)
    rank2 = jnp.sum((earlier[ROUTE_ROWS:] + run + c1) * oh2, axis=0, keepdims=True)
    run_new = run + c1 + c2
    run_scr[...] = run_new
    cnt_ref[...] = jnp.broadcast_to(run_new, cnt_ref.shape)
    row = lax.broadcasted_iota(jnp.int32, (SUBLANES, tm), 0)
    ri_ref[0] = jnp.where(row == 0, i1 - N_GROUPS,
                          jnp.where(row == 1, i2 - N_GROUPS,
                                    jnp.where(row == 2, rank1.astype(jnp.int32),
                                              jnp.where(row == 3, rank2.astype(jnp.int32), 0))))
    rw_ref[0] = jnp.where(row == 0, w1, jnp.where(row == 1, w2, 0.0))


def _post(ya, ym, p, x, mod, wa, wm, wo, norm2, wr, br, tm):
    B, S, D = x.shape
    tok = lambda b, i: (b, i, 0)
    const = lambda b, i: (0, 0)
    return pl.pallas_call(
        functools.partial(_post_kernel, tm=tm),
        grid=(B, S // tm),
        in_specs=[pl.BlockSpec((1, tm, DA_WIDTH), tok),
                  pl.BlockSpec((1, tm, ML_WIDTH), tok),
                  pl.BlockSpec((1, tm, D), lambda b, i: (b, i, COL_GA // D_MODEL)),
                  pl.BlockSpec((1, tm, D), lambda b, i: (b, i, COL_GM // D_MODEL)),
                  pl.BlockSpec((1, tm, D), tok),
                  pl.BlockSpec((1, 6, D), lambda b, i: (b, 0, 0)),
                  pl.BlockSpec((DA_WIDTH, D), const),
                  pl.BlockSpec((ML_WIDTH, D), const),
                  pl.BlockSpec((D, D), const),
                  pl.BlockSpec((1, D), const),
                  pl.BlockSpec((ROUTE_ROWS, 3 * D), const),
                  pl.BlockSpec((ROUTE_ROWS, 1), const)],
        out_specs=[pl.BlockSpec((1, tm, D), tok),
                   pl.BlockSpec((1, tm * ROW_PARTS, LANES), tok),
                   pl.BlockSpec((1, SUBLANES, tm), lambda b, i: (b, 0, i)),
                   pl.BlockSpec((1, SUBLANES, tm), lambda b, i: (b, 0, i)),
                   pl.BlockSpec((ROUTE_ROWS, LANES), const)],
        out_shape=[jax.ShapeDtypeStruct((B, S, D), F32),
                   jax.ShapeDtypeStruct((B, S * ROW_PARTS, LANES), F32),
                   jax.ShapeDtypeStruct((B, SUBLANES, S), jnp.int32),
                   jax.ShapeDtypeStruct((B, SUBLANES, S), F32),
                   jax.ShapeDtypeStruct((ROUTE_ROWS, LANES), F32)],
        scratch_shapes=[pltpu.VMEM((ROUTE_ROWS, 1), F32)],
        compiler_params=_cparams(("arbitrary", "arbitrary")),
    )(ya, ym, p, p, x, mod, wa, wm, wo, norm2.reshape(1, D), wr, br)


DMA_UNROLL = 8


def _dispatch_kernel(ends_ref, pc_ref, dest_ref, h2_ref, xs_hbm, zbuf, sem, zsem, *, td):
    i = pl.program_id(0)
    tile_rows = EXPERT_ROWS * ROW_PARTS

    def zero_tile(first_row):
        start = pl.multiple_of(first_row * ROW_PARTS, tile_rows)
        cp = pltpu.make_async_copy(zbuf, xs_hbm.at[pl.ds(start, tile_rows)], zsem)
        cp.start()
        cp.wait()

    @pl.when(i == 0)
    def _():
        zbuf[...] = jnp.zeros(zbuf.shape, zbuf.dtype)
        for e in range(N_EXPERTS):
            @pl.when(pc_ref[e] > 0)
            def _():
                zero_tile(ends_ref[e] - EXPERT_ROWS)

        def fill(tile, carry):
            zero_tile(tile * EXPERT_ROWS)
            return carry

        lax.fori_loop(ends_ref[N_EXPERTS - 1] // EXPERT_ROWS, xs_hbm.shape[0] // tile_rows, fill, 0)

    def issue(t, carry):
        src = h2_ref.at[pl.ds(pl.multiple_of(t * ROW_PARTS, ROW_PARTS), ROW_PARTS)]
        for s in range(2):
            dst = pl.multiple_of(dest_ref[2 * t + s] * ROW_PARTS, ROW_PARTS)
            pltpu.make_async_copy(src, xs_hbm.at[pl.ds(dst, ROW_PARTS)], sem).start()
        return carry

    lax.fori_loop(0, td, issue, 0, unroll=DMA_UNROLL)
    for s in range(2):
        pltpu.make_async_copy(h2_ref, xs_hbm.at[pl.ds(0, td * ROW_PARTS)], sem).wait()


def _dispatch(ends, pc, dest_flat, h2, n_rows, td):
    T = h2.shape[0] // ROW_PARTS
    grid_spec = pltpu.PrefetchScalarGridSpec(
        num_scalar_prefetch=2,
        grid=(T // td,),
        in_specs=[pl.BlockSpec((2 * td,), lambda i, e, c: (i,), memory_space=pltpu.SMEM),
                  pl.BlockSpec((td * ROW_PARTS, LANES), lambda i, e, c: (i, 0))],
        out_specs=pl.BlockSpec(memory_space=pl.ANY),
        scratch_shapes=[pltpu.VMEM((EXPERT_ROWS * ROW_PARTS, LANES), h2.dtype),
                        pltpu.SemaphoreType.DMA(()), pltpu.SemaphoreType.DMA(())],
    )
    return pl.pallas_call(
        functools.partial(_dispatch_kernel, td=td),
        grid_spec=grid_spec,
        out_shape=jax.ShapeDtypeStruct((n_rows * ROW_PARTS, LANES), h2.dtype),
        compiler_params=_cparams(("arbitrary",)),
    )(ends, pc, dest_flat, h2)


def _experts_kernel(te_ref, nt_ref, xs_ref, w1_ref, w3_ref, w2_ref, ys_ref):
    i = pl.program_id(0)

    @pl.when(i < nt_ref[0])
    def _():
        x = _load_rows(xs_ref, (), EXPERT_ROWS).astype(BF16)
        a = jnp.dot(x, w1_ref[0].astype(BF16), preferred_element_type=F32)
        b = jnp.dot(x, w3_ref[0].astype(BF16), preferred_element_type=F32)
        hid = (a * _sigmoid(a) * b).astype(BF16)
        _store_rows(ys_ref, (), jnp.dot(hid, w2_ref[0].astype(BF16), preferred_element_type=F32))

    @pl.when(i >= nt_ref[0])
    def _():
        ys_ref[...] = jnp.zeros(ys_ref.shape, ys_ref.dtype)


def _experts(tile_e, n_tiles, xs, w1, w3, w2):
    n_rows = xs.shape[0] // ROW_PARTS
    D = D_MODEL
    nt = n_rows // EXPERT_ROWS
    tile = (EXPERT_ROWS * ROW_PARTS, LANES)
    rows = lambda i, te, n: (jnp.minimum(i, jnp.maximum(n[0] - 1, 0)), 0)
    wsel = lambda i, te, n: (te[i], 0, 0)
    grid_spec = pltpu.PrefetchScalarGridSpec(
        num_scalar_prefetch=2,
        grid=(nt,),
        in_specs=[pl.BlockSpec(tile, rows),
                  pl.BlockSpec((1, D, D_EXPERT), wsel),
                  pl.BlockSpec((1, D, D_EXPERT), wsel),
                  pl.BlockSpec((1, D_EXPERT, D), wsel)],
        out_specs=pl.BlockSpec(tile, lambda i, te, n: (i, 0)),
    )
    return pl.pallas_call(
        _experts_kernel,
        grid_spec=grid_spec,
        out_shape=jax.ShapeDtypeStruct((n_rows * ROW_PARTS, LANES), F32),
        compiler_params=_cparams(("arbitrary",)),
    )(tile_e, n_tiles, xs, w1, w3, w2)


def _combine_kernel(dcur_ref, dnxt_ref, ys_hbm, x1_ref, rw_ref, mod_ref, nf_ref, o_ref, ybuf, sem, *, tc, nsteps):
    g = pl.program_id(0)
    slot = g % 2

    def gather(dest_ref, sl):
        def issue(t, carry):
            dst = pl.ds(pl.multiple_of(t * ROW_PARTS, ROW_PARTS), ROW_PARTS)
            for s in range(2):
                src = pl.multiple_of(dest_ref[2 * t + s] * ROW_PARTS, ROW_PARTS)
                pltpu.make_async_copy(ys_hbm.at[pl.ds(src, ROW_PARTS)], ybuf.at[sl, s, dst], sem.at[sl]).start()
            return carry

        lax.fori_loop(0, tc, issue, 0, unroll=DMA_UNROLL)

    @pl.when(g == 0)
    def _():
        gather(dcur_ref, 0)

    @pl.when(g + 1 < nsteps)
    def _():
        gather(dnxt_ref, 1 - slot)

    for s in range(2):
        pltpu.make_async_copy(ys_hbm.at[pl.ds(0, tc * ROW_PARTS)], ybuf.at[slot, s], sem.at[slot]).wait()

    rw = jnp.concatenate([rw_ref[0], jnp.zeros((LANES - SUBLANES, tc), F32)], axis=0).T
    y = rw[:, 0:1] * _load_rows(ybuf, (slot, 0), tc) + rw[:, 1:2] * _load_rows(ybuf, (slot, 1), tc)
    xo = x1_ref[0] + mod_ref[0, 5:6, :] * y
    o_ref[0] = xo * lax.rsqrt(jnp.mean(xo * xo, axis=-1, keepdims=True) + EPS) * nf_ref[...]


def _combine(dest_flat, ys, x1, rw, mod, norm_f, tc):
    B, S, D = x1.shape
    n = S // tc
    nsteps = B * n
    tok = lambda g: (g // n, g % n, 0)
    return pl.pallas_call(
        functools.partial(_combine_kernel, tc=tc, nsteps=nsteps),
        grid=(nsteps,),
        in_specs=[pl.BlockSpec((2 * tc,), lambda g: (g,), memory_space=pltpu.SMEM),
                  pl.BlockSpec((2 * tc,), lambda g: (jnp.minimum(g + 1, nsteps - 1),), memory_space=pltpu.SMEM),
                  pl.BlockSpec(memory_space=pl.ANY),
                  pl.BlockSpec((1, tc, D), tok),
                  pl.BlockSpec((1, SUBLANES, tc), lambda g: (g // n, 0, g % n)),
                  pl.BlockSpec((1, 6, D), lambda g: (g // n, 0, 0)),
                  pl.BlockSpec((1, D), lambda g: (0, 0))],
        out_specs=pl.BlockSpec((1, tc, D), tok),
        out_shape=jax.ShapeDtypeStruct((B, S, D), F32),
        scratch_shapes=[pltpu.VMEM((2, 2, tc * ROW_PARTS, LANES), F32), pltpu.SemaphoreType.DMA((2,))],
        compiler_params=_cparams(("arbitrary",)),
    )(dest_flat, dest_flat, ys, x1, rw, mod, norm_f.reshape(1, D))


def _pick(n, pref):
    t = min(n, pref)
    assert n % t == 0, (n, pref)
    return t


def kernel(x, c, w_ada, b_ada, norm1, w_in, b_if, conv_w, conv_b, lam_q1, lam_k1, lam_q2, lam_k2,
           diff_norm, mlstm_norm, w_br_a, w_br_m, w_out, norm2, w_rg, b_rg, w_re, b_re,
           w_e1, w_e3, w_e2, norm_f):
    B, S, D = x.shape
    assert D == D_MODEL and w_ada.shape[0] == 1
    T = B * S
    l = 0

    mod = _ada(c, w_ada[l], b_ada[l]).reshape(B, 6, D)

    w = w_in[l]
    o_q, o_k, o_v, o_qk, o_mv, o_mo, o_if, o_ga, o_gm = np.cumsum((0,) + (
        DA_WIDTH, DA_WIDTH, DA_WIDTH, 2 * ML_WIDTH, ML_WIDTH, ML_WIDTH, 2 * ML_HEADS, D_MODEL))
    w_perm = jnp.concatenate([
        w[:, o_qk:o_qk + 2 * ML_WIDTH], w[:, o_ga:o_ga + D], w[:, o_gm:o_gm + D],
        w[:, o_q:o_q + DA_WIDTH] * (DA_HD ** -0.5), w[:, o_k:o_k + DA_WIDTH], w[:, o_v:o_v + DA_WIDTH],
        w[:, o_mv:o_mv + ML_WIDTH], w[:, o_mo:o_mo + ML_WIDTH], w[:, o_if:o_if + 2 * ML_HEADS],
        jnp.zeros((D, LANES - 2 * ML_HEADS), F32)], axis=1).astype(BF16)
    bif = jnp.concatenate([b_if[l], jnp.zeros((LANES - 2 * ML_HEADS,), F32)]).reshape(1, LANES)

    p, gates = _proj(x, mod, norm1[l], bif, w_perm, _pick(S, PROJ_TM))

    lamv = jnp.stack([lam_q1[l], lam_k1[l], lam_q2[l], lam_k2[l]])
    ya = _attn(p, lamv, diff_norm[l], _pick(S, ATTN_TQ), _pick(S, ATTN_TK))
    ym = _mlstm(p, gates, conv_w[l], conv_b[l], mlstm_norm[l], _pick(S, MLSTM_L))

    pad_rows = ROUTE_ROWS - N_GROUPS - N_EXPERTS
    wr = jnp.concatenate([w_rg[l], w_re[l], jnp.zeros((D, pad_rows), F32)], axis=1).T
    wr_hi = wr.astype(BF16)
    wr_lo = (wr - wr_hi.astype(F32)).astype(BF16)
    wr = jnp.concatenate([wr_hi, wr_hi, wr_lo], axis=1)
    br = jnp.concatenate([b_rg[l], b_re[l], jnp.zeros((pad_rows,), F32)]).reshape(ROUTE_ROWS, 1)
    x1, h2, ri, rw, cnt = _post(ya, ym, p, x, mod, w_br_a[l].astype(BF16), w_br_m[l].astype(BF16),
                                w_out[l].astype(BF16), norm2[l], wr, br, _pick(S, POST_TM))

    counts = cnt[N_GROUPS:N_GROUPS + N_EXPERTS, 0].astype(jnp.int32)
    pc = ((counts + EXPERT_ROWS - 1) // EXPERT_ROWS) * EXPERT_ROWS
    ends = jnp.cumsum(pc)
    offs = ends - pc
    eid = ri[:, 0:2, :]
    rank = ri[:, 2:4, :]
    dest = rank + jnp.sum(jnp.where(eid[..., None] == jnp.arange(N_EXPERTS), offs, 0), axis=-1)
    dest_flat = dest.transpose(0, 2, 1).reshape(2 * T).astype(jnp.int32)
    n_rows = 2 * T + N_EXPERTS * EXPERT_ROWS
    n_tiles = n_rows // EXPERT_ROWS
    tile_e = jnp.sum(jnp.arange(n_tiles)[:, None] * EXPERT_ROWS >= ends[None, :], axis=1)
    tile_e = jnp.minimum(tile_e, N_EXPERTS - 1).astype(jnp.int32)
    used_tiles = (ends[-1:] // EXPERT_ROWS).astype(jnp.int32)

    xs = _dispatch(ends.astype(jnp.int32), pc.astype(jnp.int32), dest_flat, h2.reshape(T * ROW_PARTS, LANES), n_rows,
                   _pick(T, SMEM_BLOCK_1D))
    w1 = w_e1[l].reshape(N_EXPERTS, D, D_EXPERT)
    w3 = w_e3[l].reshape(N_EXPERTS, D, D_EXPERT)
    w2 = w_e2[l].reshape(N_EXPERTS, D_EXPERT, D)
    ys = _experts(tile_e, used_tiles, xs, w1, w3, w2)
    return _combine(dest_flat, ys, x1, rw, mod, norm_f, _pick(S, SMEM_BLOCK_1D // 2))
```

```python
import functools
import math

import jax
import jax.numpy as jnp
import numpy as np
from jax import lax
from jax.experimental import pallas as pl
from jax.experimental.pallas import tpu as pltpu

F32 = jnp.float32
BF16 = jnp.bfloat16

D_MODEL = 1024
DA_HEADS = 4
DA_HD = 64
DA_WIDTH = DA_HEADS * 2 * DA_HD
ML_HEADS = 4
ML_HD = 128
ML_WIDTH = ML_HEADS * ML_HD
CONV_W = 4
N_GROUPS = 4
EXPERTS_PER_GROUP = 8
N_EXPERTS = N_GROUPS * EXPERTS_PER_GROUP
D_EXPERT = 256
EPS = 1e-6
LAMBDA_INIT = 0.8 - 0.6 * math.exp(-0.3 * 0)

LANES = 128
SUBLANES = 8
SMEM_BLOCK_1D = 1024
NEG = -1e30
VMEM_LIMIT = 56 * 1024 * 1024

COL_MLQK = 0
COL_GA = 1024
COL_GM = 2048
COL_DAQ = 3072
COL_DAK = 3584
COL_DAV = 4096
COL_MLV = 4608
COL_MLO = 5120
COL_IF = 5632
N_PROJ = COL_IF + LANES
PROJ_TN = 1920
assert N_PROJ % PROJ_TN == 0

EXPERT_ROWS = 256
PROJ_TM = 1024
ATTN_TQ = 1024
ATTN_TK = 512
MLSTM_L = 256
CONV_TAIL = 16
POST_TM = 512
ROUTE_ROWS = 48


def _cparams(sem):
    return pltpu.CompilerParams(dimension_semantics=sem, vmem_limit_bytes=VMEM_LIMIT)


ROW_PARTS = D_MODEL // LANES
assert ROW_PARTS == SUBLANES


def _store_rows(ref, lead, x):
    n = x.shape[0]
    for a in range(ROW_PARTS):
        ref[lead + (pl.ds(a, n, stride=ROW_PARTS), slice(None))] = x[:, a * LANES:(a + 1) * LANES]


def _load_rows(ref, lead, n):
    return jnp.concatenate([ref[lead + (pl.ds(a, n, stride=ROW_PARTS), slice(None))] for a in range(ROW_PARTS)],
                           axis=1)


def _sigmoid(x):
    return 0.5 * jnp.tanh(0.5 * x) + 0.5


def _ada_kernel(c_ref, w_ref, b_ref, o_ref):
    c = c_ref[...]
    cs = c * _sigmoid(c)
    o_ref[...] = jnp.dot(cs, w_ref[...], precision=lax.Precision.HIGHEST,
                         preferred_element_type=F32) + b_ref[...]


def _ada(c, w, b):
    B, D = c.shape
    N = w.shape[1]
    tn = 1536
    return pl.pallas_call(
        _ada_kernel,
        grid=(N // tn,),
        in_specs=[pl.BlockSpec((B, D), lambda j: (0, 0)),
                  pl.BlockSpec((D, tn), lambda j: (0, j)),
                  pl.BlockSpec((1, tn), lambda j: (0, j))],
        out_specs=pl.BlockSpec((B, tn), lambda j: (0, j)),
        out_shape=jax.ShapeDtypeStruct((B, N), F32),
        compiler_params=_cparams(("arbitrary",)),
    )(c, w, b.reshape(1, N))


def _proj_kernel(x_ref, mod_ref, g_ref, bif_ref, w_ref, p_ref, gate_ref, h_scr, *, nj):
    j = pl.program_id(2)

    @pl.when(j == 0)
    def _():
        x = x_ref[0]
        y = x * lax.rsqrt(jnp.mean(x * x, axis=-1, keepdims=True) + EPS) * g_ref[...]
        h = y * (1.0 + mod_ref[0, 1:2, :]) + mod_ref[0, 0:1, :]
        h_scr[...] = h.astype(BF16)

    acc = jnp.dot(h_scr[...], w_ref[...], preferred_element_type=F32)
    p_ref[0] = acc.astype(BF16)

    @pl.when(j == nj - 1)
    def _():
        gate_ref[0] = acc[:, PROJ_TN - LANES:] + bif_ref[...]


def _proj(x, mod, norm1, bif, w_bf16, tm):
    B, S, D = x.shape
    nj = N_PROJ // PROJ_TN
    return pl.pallas_call(
        functools.partial(_proj_kernel, nj=nj),
        grid=(B, S // tm, nj),
        in_specs=[pl.BlockSpec((1, tm, D), lambda b, i, j: (b, i, 0)),
                  pl.BlockSpec((1, 6, D), lambda b, i, j: (b, 0, 0)),
                  pl.BlockSpec((1, D), lambda b, i, j: (0, 0)),
                  pl.BlockSpec((1, LANES), lambda b, i, j: (0, 0)),
                  pl.BlockSpec((D, PROJ_TN), lambda b, i, j: (0, j))],
        out_specs=[pl.BlockSpec((1, tm, PROJ_TN), lambda b, i, j: (b, i, j)),
                   pl.BlockSpec((1, tm, LANES), lambda b, i, j: (b, i, 0))],
        out_shape=[jax.ShapeDtypeStruct((B, S, N_PROJ), BF16),
                   jax.ShapeDtypeStruct((B, S, LANES), F32)],
        scratch_shapes=[pltpu.VMEM((tm, D), BF16)],
        compiler_params=_cparams(("arbitrary", "arbitrary", "arbitrary")),
    )(x, mod, norm1.reshape(1, D), bif, w_bf16)


ATTN_SUM_ROWS = 16

ATTN_HEADS = 2


def _attn_block(q_ref, k_ref, v_ref, eq_ref, ek_ref, m_scr, acc_scr, off, cbs, masked, q0, tq, tk):
    qs = slice(q0, tq)
    lane = lax.broadcasted_iota(jnp.int32, (1, LANES), 1)
    if masked:
        key = lax.broadcasted_iota(jnp.int32, (tk, tq - q0), 0)
        qry = lax.broadcasted_iota(jnp.int32, (tk, tq - q0), 1) + q0
        valid = (key - qry) <= off
    scores = []
    for u in range(ATTN_HEADS):
        hl = slice(u * LANES, (u + 1) * LANES)
        q = q_ref[0, qs, hl]
        k = k_ref[0, :, hl]
        for c in range(2):
            sel = (lane < DA_HD) if c == 0 else (lane >= DA_HD)
            s = lax.dot_general(jnp.where(sel, k, ek_ref[u]), jnp.where(sel, q, eq_ref[u, qs]),
                                (((1,), (1,)), ((), ())), preferred_element_type=F32)
            scores.append(jnp.where(valid, s, NEG) if masked else s)
    for u in range(ATTN_HEADS):
        vt = jnp.concatenate([v_ref[0, :, u * LANES:(u + 1) * LANES].T,
                              jnp.ones((ATTN_SUM_ROWS, tk), BF16)], axis=0)
        for c in range(2):
            n = 2 * u + c
            s = scores[n]
            m_prev = m_scr[n, :, qs]
            m_new = jnp.maximum(m_prev, jnp.max(s, axis=0, keepdims=True) - cbs[u])
            alpha = jnp.exp(m_prev - m_new)
            p = jnp.exp((s - (m_new + cbs[u])).astype(BF16))
            acc_scr[n, :, qs] = alpha * acc_scr[n, :, qs] + jnp.dot(vt, p, preferred_element_type=F32)
            m_scr[n, :, qs] = m_new


def _attn_kernel(it_ref, jt_ref, slope_ref, q_ref, k_ref, v_ref, eq_ref, ek_ref, lam_ref, dn_ref, o_ref,
                 m_scr, acc_scr, *, tq, tk):
    hp = pl.program_id(1)
    step = pl.program_id(2)
    i = it_ref[step]
    j = jt_ref[step]
    ratio = tq // tk

    @pl.when(j == 0)
    def _():
        m_scr[...] = jnp.full(m_scr.shape, NEG, F32)
        acc_scr[...] = jnp.zeros(acc_scr.shape, F32)

    off = i * tq - j * tk
    cbs = [slope_ref[hp * ATTN_HEADS + u] * off.astype(F32) for u in range(ATTN_HEADS)]
    args = (q_ref, k_ref, v_ref, eq_ref, ek_ref, m_scr, acc_scr, off, cbs)

    @pl.when(j < i * ratio)
    def _():
        _attn_block(*args, masked=False, q0=0, tq=tq, tk=tk)

    for r in range(ratio):
        @pl.when(j == i * ratio + r)
        def _():
            _attn_block(*args, masked=True, q0=r * tk, tq=tq, tk=tk)

    @pl.when(j == (i + 1) * ratio - 1)
    def _():
        lv = lam_ref[...]
        lam = (jnp.exp(jnp.sum(lv[0:1] * lv[1:2], axis=-1, keepdims=True))
               - jnp.exp(jnp.sum(lv[2:3] * lv[3:4], axis=-1, keepdims=True)) + LAMBDA_INIT)
        vd = 2 * DA_HD
        for u in range(ATTN_HEADS):
            a0, a1 = acc_scr[2 * u], acc_scr[2 * u + 1]
            o = a0[0:vd] / a0[vd:vd + 1] - lam * (a1[0:vd] / a1[vd:vd + 1])
            o = o * lax.rsqrt(jnp.mean(o * o, axis=0, keepdims=True) + EPS)
            o_ref[0, :, u * LANES:(u + 1) * LANES] = (o.T * dn_ref[...] * (1.0 - LAMBDA_INIT)).astype(BF16)


def _alibi_columns(n, slopes, q_side):
    assert n <= 256 * 256 and all(math.log2(s).is_integer() for s in slopes)
    pos = np.arange(n)
    lo, hi = (pos % 256).astype(np.float64), (pos // 256 * 256).astype(np.float64)
    out = np.zeros((len(slopes), n, LANES), np.float64)
    for h, s in enumerate(slopes):
        cols = (-s * lo, -s * hi, np.ones(n), np.ones(n)) if q_side else (np.ones(n), np.ones(n), s * lo, s * hi)
        for base in (0, DA_HD):
            for c, v in enumerate(cols):
                out[h, :, base + c] = v
    return jnp.asarray(out, dtype=BF16)


def _attn(p, lamv, diff_norm, tq, tk):
    B, S, _ = p.shape
    nq, ratio = S // tq, tq // tk
    slopes = [2.0 ** (-8.0 * (h + 1) / DA_HEADS) for h in range(DA_HEADS)]
    steps = [(i, j) for i in range(nq) for j in range((i + 1) * ratio)]
    it = jnp.asarray([s[0] for s in steps], jnp.int32)
    jt = jnp.asarray([s[1] for s in steps], jnp.int32)
    hw = ATTN_HEADS * LANES
    qb, kb, vb = COL_DAQ // hw, COL_DAK // hw, COL_DAV // hw
    grid_spec = pltpu.PrefetchScalarGridSpec(
        num_scalar_prefetch=3,
        grid=(B, DA_HEADS // ATTN_HEADS, len(steps)),
        in_specs=[pl.BlockSpec((1, tq, hw), lambda b, h, s, it, jt, sl: (b, it[s], qb + h)),
                  pl.BlockSpec((1, tk, hw), lambda b, h, s, it, jt, sl: (b, jt[s], kb + h)),
                  pl.BlockSpec((1, tk, hw), lambda b, h, s, it, jt, sl: (b, jt[s], vb + h)),
                  pl.BlockSpec((ATTN_HEADS, tq, LANES), lambda b, h, s, it, jt, sl: (h, 0, 0)),
                  pl.BlockSpec((ATTN_HEADS, tk, LANES), lambda b, h, s, it, jt, sl: (h, 0, 0)),
                  pl.BlockSpec((4, DA_HD), lambda b, h, s, it, jt, sl: (0, 0)),
                  pl.BlockSpec((1, 2 * DA_HD), lambda b, h, s, it, jt, sl: (0, 0))],
        out_specs=pl.BlockSpec((1, tq, hw), lambda b, h, s, it, jt, sl: (b, it[s], h)),
        scratch_shapes=[pltpu.VMEM((2 * ATTN_HEADS, 1, tq), F32),
                        pltpu.VMEM((2 * ATTN_HEADS, 2 * DA_HD + ATTN_SUM_ROWS, tq), F32)],
    )
    return pl.pallas_call(
        functools.partial(_attn_kernel, tq=tq, tk=tk),
        grid_spec=grid_spec,
        out_shape=jax.ShapeDtypeStruct((B, S, DA_WIDTH), BF16),
        compiler_params=_cparams(("arbitrary",) * 3),
    )(it, jt, jnp.asarray(slopes, F32), p, p, p, _alibi_columns(tq, slopes, True),
      _alibi_columns(tk, slopes, False), lamv, diff_norm.reshape(1, 2 * DA_HD))


def _mlstm_kernel(qk_ref, v_ref, o_ref, g_ref, cw_ref, cb_ref, nw_ref, y_ref,
                  ext_scr, c_scr, n_scr, m_scr, *, L):
    i = pl.program_id(1)

    @pl.when(i == 0)
    def _():
        ext_scr[...] = jnp.zeros(ext_scr.shape, BF16)
        c_scr[...] = jnp.zeros(c_scr.shape, F32)
        n_scr[...] = jnp.zeros(n_scr.shape, F32)
        m_scr[...] = jnp.zeros(m_scr.shape, F32)

    raw = qk_ref[0]
    tail = ext_scr[...]
    t_out = lax.broadcasted_iota(jnp.int32, (L, L), 0)
    t_in = lax.broadcasted_iota(jnp.int32, (L, L), 1)
    h_out = lax.broadcasted_iota(jnp.int32, (CONV_TAIL, CONV_TAIL), 0)
    h_in = lax.broadcasted_iota(jnp.int32, (CONV_TAIL, CONV_TAIL), 1)
    conv = cb_ref[...] + raw.astype(F32) * cw_ref[CONV_W - 1:CONV_W, :]
    for k in range(1, CONV_W):
        shift = jnp.where(t_in == t_out - k, 1.0, 0.0).astype(BF16)
        head = jnp.where(h_in == h_out - k + CONV_TAIL, 1.0, 0.0).astype(BF16)
        xk = jnp.dot(shift, raw, preferred_element_type=F32)
        fix = jnp.dot(head, tail, preferred_element_type=F32)
        xk = jnp.concatenate([xk[:CONV_TAIL] + fix, xk[CONV_TAIL:]], axis=0)
        conv = conv + xk * cw_ref[CONV_W - 1 - k:CONV_W - k, :]
    ext_scr[...] = raw[L - CONV_TAIL:L, :]
    qkc = conv * _sigmoid(conv)

    gts = g_ref[0]
    fpre = pltpu.roll(gts, LANES - ML_HEADS, axis=1)
    lf = jnp.minimum(fpre, 0.0) - jnp.log(1.0 + jnp.exp(-jnp.abs(fpre)))
    row = lax.broadcasted_iota(jnp.int32, (L, L), 0)
    col = lax.broadcasted_iota(jnp.int32, (L, L), 1)
    causal = col <= row
    bcum = jnp.dot(causal.astype(F32), lf, precision=lax.Precision.HIGHEST, preferred_element_type=F32)
    r = gts - bcum
    rt = r.T
    m_all = m_scr[...]
    lane = lax.broadcasted_iota(jnp.int32, (1, LANES), 1)
    m_next = m_all
    v_all = v_ref[0]
    o_all = o_ref[0]
    for h in range(ML_HEADS):
        hs = slice(h * ML_HD, (h + 1) * ML_HD)
        bcol = bcum[:, h:h + 1]
        rcol = r[:, h:h + 1]
        rrow = rt[h:h + 1, :]
        g = bcum[L - 1:L, h:h + 1]
        mh = m_all[:, h:h + 1]
        dm = jnp.where(causal, bcol + rrow, NEG)
        inter = bcol + mh
        mj = jnp.maximum(inter, jnp.max(dm, axis=-1, keepdims=True))
        w_intra = jnp.exp(dm - mj)
        w_inter = jnp.exp(inter - mj)
        qh = qkc[:, hs]
        kh = qkc[:, ML_WIDTH + h * ML_HD:ML_WIDTH + (h + 1) * ML_HD] * (ML_HD ** -0.5)
        vh = v_all[:, hs]
        qb = qh.astype(BF16)
        kb = kh.astype(BF16)
        s = lax.dot_general(qb, kb, (((1,), (1,)), ((), ())), preferred_element_type=F32) * w_intra
        c_old = c_scr[h]
        n_old = n_scr[h:h + 1, :]
        num = (jnp.dot(s.astype(BF16), vh, preferred_element_type=F32)
               + lax.dot_general(qb, c_old.astype(BF16), (((1,), (1,)), ((), ())),
                                 preferred_element_type=F32) * w_inter)
        den = (jnp.sum(s, axis=-1, keepdims=True)
               + w_inter * jnp.sum(qh * n_old, axis=-1, keepdims=True))
        denom = jnp.maximum(jnp.abs(den), jnp.exp(-mj))
        ht = num / denom
        a_col = g + rcol
        m_new = jnp.maximum(g + mh, jnp.max(a_col, axis=0, keepdims=True))
        wa = jnp.exp(a_col - m_new)
        decay = jnp.exp(g + mh - m_new)
        vw_t = (vh.astype(F32) * wa).T.astype(BF16)
        c_scr[h] = decay * c_old + jnp.dot(vw_t, kb, preferred_element_type=F32)
        n_scr[h:h + 1, :] = decay * n_old + jnp.sum(kh * wa, axis=0, keepdims=True)
        m_next = jnp.where(lane == h, m_new, m_next)
        z = _sigmoid(o_all[:, hs].astype(F32)) * ht
        z = z * lax.rsqrt(jnp.mean(z * z, axis=-1, keepdims=True) + EPS) * nw_ref[...]
        y_ref[0, :, hs] = z.astype(BF16)
    m_scr[...] = m_next


def _mlstm(p, gates, conv_w, conv_b, mlstm_norm, L):
    B, S, _ = p.shape
    return pl.pallas_call(
        functools.partial(_mlstm_kernel, L=L),
        grid=(B, S // L),
        in_specs=[pl.BlockSpec((1, L, 2 * ML_WIDTH), lambda b, i: (b, i, COL_MLQK // (2 * ML_WIDTH))),
                  pl.BlockSpec((1, L, ML_WIDTH), lambda b, i: (b, i, COL_MLV // ML_WIDTH)),
                  pl.BlockSpec((1, L, ML_WIDTH), lambda b, i: (b, i, COL_MLO // ML_WIDTH)),
                  pl.BlockSpec((1, L, LANES), lambda b, i: (b, i, 0)),
                  pl.BlockSpec((CONV_W, 2 * ML_WIDTH), lambda b, i: (0, 0)),
                  pl.BlockSpec((1, 2 * ML_WIDTH), lambda b, i: (0, 0)),
                  pl.BlockSpec((1, ML_HD), lambda b, i: (0, 0))],
        out_specs=pl.BlockSpec((1, L, ML_WIDTH), lambda b, i: (b, i, 0)),
        out_shape=jax.ShapeDtypeStruct((B, S, ML_WIDTH), BF16),
        scratch_shapes=[pltpu.VMEM((CONV_TAIL, 2 * ML_WIDTH), BF16),
                        pltpu.VMEM((ML_HEADS, ML_HD, ML_HD), F32),
                        pltpu.VMEM((SUBLANES, ML_HD), F32),
                        pltpu.VMEM((1, LANES), F32)],
        compiler_params=_cparams(("arbitrary", "arbitrary")),
    )(p, p, p, gates, conv_w, conv_b.reshape(1, -1), mlstm_norm.reshape(1, ML_HD))


def _post_kernel(ya_ref, ym_ref, ga_ref, gm_ref, x_ref, mod_ref, wa_ref, wm_ref, wo_ref, n2_ref,
                 wr_ref, br_ref, x1_ref, h2_ref, ri_ref, rw_ref, cnt_ref, run_scr, *, tm):
    first = jnp.logical_and(pl.program_id(0) == 0, pl.program_id(1) == 0)

    @pl.when(first)
    def _():
        run_scr[...] = jnp.zeros(run_scr.shape, F32)

    a = jnp.dot(ya_ref[0], wa_ref[...], preferred_element_type=F32)
    m = jnp.dot(ym_ref[0], wm_ref[...], preferred_element_type=F32)
    merged = _sigmoid(ga_ref[0]).astype(F32) * a + _sigmoid(gm_ref[0]).astype(F32) * m
    o = jnp.dot(merged.astype(BF16), wo_ref[...], preferred_element_type=F32)
    x1 = x_ref[0] + mod_ref[0, 2:3, :] * o
    x1_ref[0] = x1
    h2 = x1 * lax.rsqrt(jnp.mean(x1 * x1, axis=-1, keepdims=True) + EPS) * n2_ref[...]
    h2 = h2 * (1.0 + mod_ref[0, 4:5, :]) + mod_ref[0, 3:4, :]
    _store_rows(h2_ref, (0,), h2)

    hi = h2.astype(BF16)
    lo = (h2 - hi.astype(F32)).astype(BF16)
    logits = lax.dot_general(wr_ref[...], jnp.concatenate([hi, lo, hi], axis=1), (((1,), (1,)), ((), ())),
                             preferred_element_type=F32) + br_ref[...]
    sub = lax.broadcasted_iota(jnp.int32, (ROUTE_ROWS, tm), 0)
    big = jnp.int32(4 * ROUTE_ROWS)
    gl = jnp.where(sub < N_GROUPS, logits, NEG)
    gmax = jnp.max(gl, axis=0, keepdims=True)
    gsel = jnp.min(jnp.where(gl == gmax, sub, big), axis=0, keepdims=True)
    pgrp = 1.0 / jnp.sum(jnp.exp(gl - gmax), axis=0, keepdims=True)
    first = N_GROUPS + EXPERTS_PER_GROUP * gsel
    el = jnp.where(jnp.logical_and(sub >= first, sub < first + EXPERTS_PER_GROUP), logits, NEG)
    e1 = jnp.max(el, axis=0, keepdims=True)
    i1 = jnp.min(jnp.where(el == e1, sub, big), axis=0, keepdims=True)
    el2 = jnp.where(sub == i1, NEG, el)
    e2 = jnp.max(el2, axis=0, keepdims=True)
    i2 = jnp.min(jnp.where(el2 == e2, sub, big), axis=0, keepdims=True)
    tt = jnp.exp(e2 - e1)
    w1 = pgrp / (1.0 + tt)
    w2 = pgrp * tt / (1.0 + tt)

    oh1 = jnp.where(sub == i1, 1.0, 0.0).astype(F32)
    oh2 = jnp.where(sub == i2, 1.0, 0.0).astype(F32)
    cat = jnp.concatenate([oh1, oh2], axis=0).astype(BF16)
    src = lax.broadcasted_iota(jnp.int32, (tm, tm), 0)
    dst = lax.broadcasted_iota(jnp.int32, (tm, tm), 1)
    before = jnp.where(src < dst, 1.0, 0.0).astype(BF16)
    earlier = jnp.dot(cat, before, preferred_element_type=F32)
    c1 = jnp.sum(oh1, axis=1, keepdims=True)
    c2 = jnp.sum(oh2, axis=1, keepdims=True)
    run = run_scr[...]
    rank1 = jnp.sum((earlier[:ROUTE_ROWS] + run) * oh1, axis=0, keepdims=True)
    rank2 = jnp.sum((earlier[ROUTE_ROWS:] + run + c1) * oh2, axis=0, keepdims=True)
    run_new = run + c1 + c2
    run_scr[...] = run_new
    cnt_ref[...] = jnp.broadcast_to(run_new, cnt_ref.shape)
    row = lax.broadcasted_iota(jnp.int32, (SUBLANES, tm), 0)
    ri_ref[0] = jnp.where(row == 0, i1 - N_GROUPS,
                          jnp.where(row == 1, i2 - N_GROUPS,
                                    jnp.where(row == 2, rank1.astype(jnp.int32),
                                              jnp.where(row == 3, rank2.astype(jnp.int32), 0))))
    rw_ref[0] = jnp.where(row == 0, w1, jnp.where(row == 1, w2, 0.0))


def _post(ya, ym, p, x, mod, wa, wm, wo, norm2, wr, br, tm):
    B, S, D = x.shape
    tok = lambda b, i: (b, i, 0)
    const = lambda b, i: (0, 0)
    return pl.pallas_call(
        functools.partial(_post_kernel, tm=tm),
        grid=(B, S // tm),
        in_specs=[pl.BlockSpec((1, tm, DA_WIDTH), tok),
                  pl.BlockSpec((1, tm, ML_WIDTH), tok),
                  pl.BlockSpec((1, tm, D), lambda b, i: (b, i, COL_GA // D_MODEL)),
                  pl.BlockSpec((1, tm, D), lambda b, i: (b, i, COL_GM // D_MODEL)),
                  pl.BlockSpec((1, tm, D), tok),
                  pl.BlockSpec((1, 6, D), lambda b, i: (b, 0, 0)),
                  pl.BlockSpec((DA_WIDTH, D), const),
                  pl.BlockSpec((ML_WIDTH, D), const),
                  pl.BlockSpec((D, D), const),
                  pl.BlockSpec((1, D), const),
                  pl.BlockSpec((ROUTE_ROWS, 3 * D), const),
                  pl.BlockSpec((ROUTE_ROWS, 1), const)],
        out_specs=[pl.BlockSpec((1, tm, D), tok),
                   pl.BlockSpec((1, tm * ROW_PARTS, LANES), tok),
                   pl.BlockSpec((1, SUBLANES, tm), lambda b, i: (b, 0, i)),
                   pl.BlockSpec((1, SUBLANES, tm), lambda b, i: (b, 0, i)),
                   pl.BlockSpec((ROUTE_ROWS, LANES), const)],
        out_shape=[jax.ShapeDtypeStruct((B, S, D), F32),
                   jax.ShapeDtypeStruct((B, S * ROW_PARTS, LANES), F32),
                   jax.ShapeDtypeStruct((B, SUBLANES, S), jnp.int32),
                   jax.ShapeDtypeStruct((B, SUBLANES, S), F32),
                   jax.ShapeDtypeStruct((ROUTE_ROWS, LANES), F32)],
        scratch_shapes=[pltpu.VMEM((ROUTE_ROWS, 1), F32)],
        compiler_params=_cparams(("arbitrary", "arbitrary")),
    )(ya, ym, p, p, x, mod, wa, wm, wo, norm2.reshape(1, D), wr, br)


DMA_UNROLL = 8


def _dispatch_kernel(ends_ref, pc_ref, dest_ref, h2_ref, xs_hbm, zbuf, sem, zsem, *, td):
    i = pl.program_id(0)
    tile_rows = EXPERT_ROWS * ROW_PARTS

    def zero_tile(first_row):
        start = pl.multiple_of(first_row * ROW_PARTS, tile_rows)
        cp = pltpu.make_async_copy(zbuf, xs_hbm.at[pl.ds(start, tile_rows)], zsem)
        cp.start()
        cp.wait()

    @pl.when(i == 0)
    def _():
        zbuf[...] = jnp.zeros(zbuf.shape, zbuf.dtype)
        for e in range(N_EXPERTS):
            @pl.when(pc_ref[e] > 0)
            def _():
                zero_tile(ends_ref[e] - EXPERT_ROWS)

        def fill(tile, carry):
            zero_tile(tile * EXPERT_ROWS)
            return carry

        lax.fori_loop(ends_ref[N_EXPERTS - 1] // EXPERT_ROWS, xs_hbm.shape[0] // tile_rows, fill, 0)

    def issue(t, carry):
        src = h2_ref.at[pl.ds(pl.multiple_of(t * ROW_PARTS, ROW_PARTS), ROW_PARTS)]
        for s in range(2):
            dst = pl.multiple_of(dest_ref[2 * t + s] * ROW_PARTS, ROW_PARTS)
            pltpu.make_async_copy(src, xs_hbm.at[pl.ds(dst, ROW_PARTS)], sem).start(priority=s)
        return carry

    lax.fori_loop(0, td, issue, 0, unroll=DMA_UNROLL)
    for s in range(2):
        pltpu.make_async_copy(h2_ref, xs_hbm.at[pl.ds(0, td * ROW_PARTS)], sem).wait()


def _dispatch(ends, pc, dest_flat, h2, n_rows, td):
    T = h2.shape[0] // ROW_PARTS
    grid_spec = pltpu.PrefetchScalarGridSpec(
        num_scalar_prefetch=2,
        grid=(T // td,),
        in_specs=[pl.BlockSpec((2 * td,), lambda i, e, c: (i,), memory_space=pltpu.SMEM),
                  pl.BlockSpec((td * ROW_PARTS, LANES), lambda i, e, c: (i, 0))],
        out_specs=pl.BlockSpec(memory_space=pl.ANY),
        scratch_shapes=[pltpu.VMEM((EXPERT_ROWS * ROW_PARTS, LANES), h2.dtype),
                        pltpu.SemaphoreType.DMA(()), pltpu.SemaphoreType.DMA(())],
    )
    return pl.pallas_call(
        functools.partial(_dispatch_kernel, td=td),
        grid_spec=grid_spec,
        out_shape=jax.ShapeDtypeStruct((n_rows * ROW_PARTS, LANES), h2.dtype),
        compiler_params=_cparams(("arbitrary",)),
    )(ends, pc, dest_flat, h2)


def _experts_kernel(te_ref, nt_ref, xs_ref, w1_ref, w3_ref, w2_ref, ys_ref):
    i = pl.program_id(0)

    @pl.when(i < nt_ref[0])
    def _():
        x = _load_rows(xs_ref, (), EXPERT_ROWS).astype(BF16)
        a = jnp.dot(x, w1_ref[0].astype(BF16), preferred_element_type=F32)
        b = jnp.dot(x, w3_ref[0].astype(BF16), preferred_element_type=F32)
        hid = (a * _sigmoid(a) * b).astype(BF16)
        _store_rows(ys_ref, (), jnp.dot(hid, w2_ref[0].astype(BF16), preferred_element_type=F32))

    @pl.when(i >= nt_ref[0])
    def _():
        ys_ref[...] = jnp.zeros(ys_ref.shape, ys_ref.dtype)


def _experts(tile_e, n_tiles, xs, w1, w3, w2):
    n_rows = xs.shape[0] // ROW_PARTS
    D = D_MODEL
    nt = n_rows // EXPERT_ROWS
    tile = (EXPERT_ROWS * ROW_PARTS, LANES)
    rows = lambda i, te, n: (jnp.minimum(i, jnp.maximum(n[0] - 1, 0)), 0)
    wsel = lambda i, te, n: (te[i], 0, 0)
    grid_spec = pltpu.PrefetchScalarGridSpec(
        num_scalar_prefetch=2,
        grid=(nt,),
        in_specs=[pl.BlockSpec(tile, rows),
                  pl.BlockSpec((1, D, D_EXPERT), wsel),
                  pl.BlockSpec((1, D, D_EXPERT), wsel),
                  pl.BlockSpec((1, D_EXPERT, D), wsel)],
        out_specs=pl.BlockSpec(tile, lambda i, te, n: (i, 0)),
    )
    return pl.pallas_call(
        _experts_kernel,
        grid_spec=grid_spec,
        out_shape=jax.ShapeDtypeStruct((n_rows * ROW_PARTS, LANES), F32),
        compiler_params=_cparams(("arbitrary",)),
    )(tile_e, n_tiles, xs, w1, w3, w2)


def _combine_kernel(dcur_ref, dnxt_ref, ys_hbm, x1_ref, rw_ref, mod_ref, nf_ref, o_ref, ybuf, sem, *, tc, nsteps):
    g = pl.program_id(0)
    slot = g % 2

    def gather(dest_ref, sl):
        def issue(t, carry):
            dst = pl.ds(pl.multiple_of(t * ROW_PARTS, ROW_PARTS), ROW_PARTS)
            for s in range(2):
                src = pl.multiple_of(dest_ref[2 * t + s] * ROW_PARTS, ROW_PARTS)
                pltpu.make_async_copy(ys_hbm.at[pl.ds(src, ROW_PARTS)], ybuf.at[sl, s, dst],
                                      sem.at[sl]).start(priority=s)
            return carry

        lax.fori_loop(0, tc, issue, 0, unroll=DMA_UNROLL)

    @pl.when(g == 0)
    def _():
        gather(dcur_ref, 0)

    @pl.when(g + 1 < nsteps)
    def _():
        gather(dnxt_ref, 1 - slot)

    for s in range(2):
        pltpu.make_async_copy(ys_hbm.at[pl.ds(0, tc * ROW_PARTS)], ybuf.at[slot, s], sem.at[slot]).wait()

    rw = jnp.concatenate([rw_ref[0], jnp.zeros((LANES - SUBLANES, tc), F32)], axis=0).T
    y = rw[:, 0:1] * _load_rows(ybuf, (slot, 0), tc) + rw[:, 1:2] * _load_rows(ybuf, (slot, 1), tc)
    xo = x1_ref[0] + mod_ref[0, 5:6, :] * y
    o_ref[0] = xo * lax.rsqrt(jnp.mean(xo * xo, axis=-1, keepdims=True) + EPS) * nf_ref[...]


def _combine(dest_flat, ys, x1, rw, mod, norm_f, tc):
    B, S, D = x1.shape
    n = S // tc
    nsteps = B * n
    tok = lambda g: (g // n, g % n, 0)
    return pl.pallas_call(
        functools.partial(_combine_kernel, tc=tc, nsteps=nsteps),
        grid=(nsteps,),
        in_specs=[pl.BlockSpec((2 * tc,), lambda g: (g,), memory_space=pltpu.SMEM),
                  pl.BlockSpec((2 * tc,), lambda g: (jnp.minimum(g + 1, nsteps - 1),), memory_space=pltpu.SMEM),
                  pl.BlockSpec(memory_space=pl.ANY),
                  pl.BlockSpec((1, tc, D), tok),
                  pl.BlockSpec((1, SUBLANES, tc), lambda g: (g // n, 0, g % n)),
                  pl.BlockSpec((1, 6, D), lambda g: (g // n, 0, 0)),
                  pl.BlockSpec((1, D), lambda g: (0, 0))],
        out_specs=pl.BlockSpec((1, tc, D), tok),
        out_shape=jax.ShapeDtypeStruct((B, S, D), F32),
        scratch_shapes=[pltpu.VMEM((2, 2, tc * ROW_PARTS, LANES), F32), pltpu.SemaphoreType.DMA((2,))],
        compiler_params=_cparams(("arbitrary",)),
    )(dest_flat, dest_flat, ys, x1, rw, mod, norm_f.reshape(1, D))


def _pick(n, pref):
    t = min(n, pref)
    assert n % t == 0, (n, pref)
    return t


def kernel(x, c, w_ada, b_ada, norm1, w_in, b_if, conv_w, conv_b, lam_q1, lam_k1, lam_q2, lam_k2,
           diff_norm, mlstm_norm, w_br_a, w_br_m, w_out, norm2, w_rg, b_rg, w_re, b_re,
           w_e1, w_e3, w_e2, norm_f):
    B, S, D = x.shape
    assert D == D_MODEL and w_ada.shape[0] == 1
    T = B * S
    l = 0

    mod = _ada(c, w_ada[l], b_ada[l]).reshape(B, 6, D)

    w = w_in[l]
    o_q, o_k, o_v, o_qk, o_mv, o_mo, o_if, o_ga, o_gm = np.cumsum((0,) + (
        DA_WIDTH, DA_WIDTH, DA_WIDTH, 2 * ML_WIDTH, ML_WIDTH, ML_WIDTH, 2 * ML_HEADS, D_MODEL))
    w_perm = jnp.concatenate([
        w[:, o_qk:o_qk + 2 * ML_WIDTH], w[:, o_ga:o_ga + D], w[:, o_gm:o_gm + D],
        w[:, o_q:o_q + DA_WIDTH] * (DA_HD ** -0.5), w[:, o_k:o_k + DA_WIDTH], w[:, o_v:o_v + DA_WIDTH],
        w[:, o_mv:o_mv + ML_WIDTH], w[:, o_mo:o_mo + ML_WIDTH], w[:, o_if:o_if + 2 * ML_HEADS],
        jnp.zeros((D, LANES - 2 * ML_HEADS), F32)], axis=1).astype(BF16)
    bif = jnp.concatenate([b_if[l], jnp.zeros((LANES - 2 * ML_HEADS,), F32)]).reshape(1, LANES)

    p, gates = _proj(x, mod, norm1[l], bif, w_perm, _pick(S, PROJ_TM))

    lamv = jnp.stack([lam_q1[l], lam_k1[l], lam_q2[l], lam_k2[l]])
    ya = _attn(p, lamv, diff_norm[l], _pick(S, ATTN_TQ), _pick(S, ATTN_TK))
    ym = _mlstm(p, gates, conv_w[l], conv_b[l], mlstm_norm[l], _pick(S, MLSTM_L))

    pad_rows = ROUTE_ROWS - N_GROUPS - N_EXPERTS
    wr = jnp.concatenate([w_rg[l], w_re[l], jnp.zeros((D, pad_rows), F32)], axis=1).T
    wr_hi = wr.astype(BF16)
    wr_lo = (wr - wr_hi.astype(F32)).astype(BF16)
    wr = jnp.concatenate([wr_hi, wr_hi, wr_lo], axis=1)
    br = jnp.concatenate([b_rg[l], b_re[l], jnp.zeros((pad_rows,), F32)]).reshape(ROUTE_ROWS, 1)
    x1, h2, ri, rw, cnt = _post(ya, ym, p, x, mod, w_br_a[l].astype(BF16), w_br_m[l].astype(BF16),
                                w_out[l].astype(BF16), norm2[l], wr, br, _pick(S, POST_TM))

    counts = cnt[N_GROUPS:N_GROUPS + N_EXPERTS, 0].astype(jnp.int32)
    pc = ((counts + EXPERT_ROWS - 1) // EXPERT_ROWS) * EXPERT_ROWS
    ends = jnp.cumsum(pc)
    offs = ends - pc
    eid = ri[:, 0:2, :]
    rank = ri[:, 2:4, :]
    dest = rank + jnp.sum(jnp.where(eid[..., None] == jnp.arange(N_EXPERTS), offs, 0), axis=-1)
    dest_flat = dest.transpose(0, 2, 1).reshape(2 * T).astype(jnp.int32)
    n_rows = 2 * T + N_EXPERTS * EXPERT_ROWS
    n_tiles = n_rows // EXPERT_ROWS
    tile_e = jnp.sum(jnp.arange(n_tiles)[:, None] * EXPERT_ROWS >= ends[None, :], axis=1)
    tile_e = jnp.minimum(tile_e, N_EXPERTS - 1).astype(jnp.int32)
    used_tiles = (ends[-1:] // EXPERT_ROWS).astype(jnp.int32)

    xs = _dispatch(ends.astype(jnp.int32), pc.astype(jnp.int32), dest_flat, h2.reshape(T * ROW_PARTS, LANES), n_rows,
                   _pick(T, SMEM_BLOCK_1D))
    w1 = w_e1[l].reshape(N_EXPERTS, D, D_EXPERT)
    w3 = w_e3[l].reshape(N_EXPERTS, D, D_EXPERT)
    w2 = w_e2[l].reshape(N_EXPERTS, D_EXPERT, D)
    ys = _experts(tile_e, used_tiles, xs, w1, w3, w2)
    return _combine(dest_flat, ys, x1, rw, mod, norm_f, _pick(S, SMEM_BLOCK_1D // 2))
```

```python
import functools
import math

import jax
import jax.numpy as jnp
import numpy as np
from jax import lax
from jax.experimental import pallas as pl
from jax.experimental.pallas import tpu as pltpu

F32 = jnp.float32
BF16 = jnp.bfloat16

D_MODEL = 1024
DA_HEADS = 4
DA_HD = 64
DA_WIDTH = DA_HEADS * 2 * DA_HD
ML_HEADS = 4
ML_HD = 128
ML_WIDTH = ML_HEADS * ML_HD
CONV_W = 4
N_GROUPS = 4
EXPERTS_PER_GROUP = 8
N_EXPERTS = N_GROUPS * EXPERTS_PER_GROUP
D_EXPERT = 256
EPS = 1e-6
LAMBDA_INIT = 0.8 - 0.6 * math.exp(-0.3 * 0)

LANES = 128
SUBLANES = 8
SMEM_BLOCK_1D = 1024
NEG = -1e30
VMEM_LIMIT = 56 * 1024 * 1024

COL_MLQK = 0
COL_GA = 1024
COL_GM = 2048
COL_DAQ = 3072
COL_DAK = 3584
COL_DAV = 4096
COL_MLV = 4608
COL_MLO = 5120
COL_IF = 5632
N_PROJ = COL_IF + LANES
PROJ_TN = 1920
assert N_PROJ % PROJ_TN == 0

EXPERT_ROWS = 512
PROJ_TM = 1024
ATTN_TQ = 1024
ATTN_TK = 512
MLSTM_L = 256
CONV_TAIL = 16
POST_TM = 512
ROUTE_ROWS = 48


def _cparams(sem):
    return pltpu.CompilerParams(dimension_semantics=sem, vmem_limit_bytes=VMEM_LIMIT)


ROW_PARTS = D_MODEL // LANES
assert ROW_PARTS == SUBLANES


def _store_rows(ref, lead, x):
    n = x.shape[0]
    for a in range(ROW_PARTS):
        ref[lead + (pl.ds(a, n, stride=ROW_PARTS), slice(None))] = x[:, a * LANES:(a + 1) * LANES]


def _load_rows(ref, lead, n):
    return jnp.concatenate([ref[lead + (pl.ds(a, n, stride=ROW_PARTS), slice(None))] for a in range(ROW_PARTS)],
                           axis=1)


def _sigmoid(x):
    return 0.5 * jnp.tanh(0.5 * x) + 0.5


def _ada_kernel(c_ref, w_ref, b_ref, o_ref):
    c = c_ref[...]
    cs = c * _sigmoid(c)
    o_ref[...] = jnp.dot(cs, w_ref[...], precision=lax.Precision.HIGHEST,
                         preferred_element_type=F32) + b_ref[...]


def _ada(c, w, b):
    B, D = c.shape
    N = w.shape[1]
    tn = 1536
    return pl.pallas_call(
        _ada_kernel,
        grid=(N // tn,),
        in_specs=[pl.BlockSpec((B, D), lambda j: (0, 0)),
                  pl.BlockSpec((D, tn), lambda j: (0, j)),
                  pl.BlockSpec((1, tn), lambda j: (0, j))],
        out_specs=pl.BlockSpec((B, tn), lambda j: (0, j)),
        out_shape=jax.ShapeDtypeStruct((B, N), F32),
        compiler_params=_cparams(("arbitrary",)),
    )(c, w, b.reshape(1, N))


def _proj_kernel(x_ref, mod_ref, g_ref, bif_ref, w_ref, p_ref, gate_ref, h_scr, *, nj):
    j = pl.program_id(2)

    @pl.when(j == 0)
    def _():
        x = x_ref[0]
        y = x * lax.rsqrt(jnp.mean(x * x, axis=-1, keepdims=True) + EPS) * g_ref[...]
        h = y * (1.0 + mod_ref[0, 1:2, :]) + mod_ref[0, 0:1, :]
        h_scr[...] = h.astype(BF16)

    acc = jnp.dot(h_scr[...], w_ref[...], preferred_element_type=F32)
    p_ref[0] = acc.astype(BF16)

    @pl.when(j == nj - 1)
    def _():
        gate_ref[0] = acc[:, PROJ_TN - LANES:] + bif_ref[...]


def _proj(x, mod, norm1, bif, w_bf16, tm):
    B, S, D = x.shape
    nj = N_PROJ // PROJ_TN
    return pl.pallas_call(
        functools.partial(_proj_kernel, nj=nj),
        grid=(B, S // tm, nj),
        in_specs=[pl.BlockSpec((1, tm, D), lambda b, i, j: (b, i, 0)),
                  pl.BlockSpec((1, 6, D), lambda b, i, j: (b, 0, 0)),
                  pl.BlockSpec((1, D), lambda b, i, j: (0, 0)),
                  pl.BlockSpec((1, LANES), lambda b, i, j: (0, 0)),
                  pl.BlockSpec((D, PROJ_TN), lambda b, i, j: (0, j))],
        out_specs=[pl.BlockSpec((1, tm, PROJ_TN), lambda b, i, j: (b, i, j)),
                   pl.BlockSpec((1, tm, LANES), lambda b, i, j: (b, i, 0))],
        out_shape=[jax.ShapeDtypeStruct((B, S, N_PROJ), BF16),
                   jax.ShapeDtypeStruct((B, S, LANES), F32)],
        scratch_shapes=[pltpu.VMEM((tm, D), BF16)],
        compiler_params=_cparams(("arbitrary", "arbitrary", "arbitrary")),
    )(x, mod, norm1.reshape(1, D), bif, w_bf16)


ATTN_SUM_ROWS = 16

ATTN_HEADS = 2


def _attn_block(q_ref, k_ref, v_ref, eq_ref, ek_ref, m_scr, acc_scr, off, cbs, masked, q0, tq, tk):
    qs = slice(q0, tq)
    lane = lax.broadcasted_iota(jnp.int32, (1, LANES), 1)
    if masked:
        key = lax.broadcasted_iota(jnp.int32, (tk, tq - q0), 0)
        qry = lax.broadcasted_iota(jnp.int32, (tk, tq - q0), 1) + q0
        valid = (key - qry) <= off
    scores = []
    for u in range(ATTN_HEADS):
        hl = slice(u * LANES, (u + 1) * LANES)
        q = q_ref[0, qs, hl]
        k = k_ref[0, :, hl]
        for c in range(2):
            sel = (lane < DA_HD) if c == 0 else (lane >= DA_HD)
            s = lax.dot_general(jnp.where(sel, k, ek_ref[u]), jnp.where(sel, q, eq_ref[u, qs]),
                                (((1,), (1,)), ((), ())), preferred_element_type=F32)
            scores.append(jnp.where(valid, s, NEG) if masked else s)
    for u in range(ATTN_HEADS):
        vt = jnp.concatenate([v_ref[0, :, u * LANES:(u + 1) * LANES].T,
                              jnp.ones((ATTN_SUM_ROWS, tk), BF16)], axis=0)
        for c in range(2):
            n = 2 * u + c
            s = scores[n]
            m_prev = m_scr[n, :, qs]
            m_new = jnp.maximum(m_prev, jnp.max(s, axis=0, keepdims=True) - cbs[u])
            alpha = jnp.exp(m_prev - m_new)
            p = jnp.exp((s - (m_new + cbs[u])).astype(BF16))
            acc_scr[n, :, qs] = alpha * acc_scr[n, :, qs] + jnp.dot(vt, p, preferred_element_type=F32)
            m_scr[n, :, qs] = m_new


def _attn_kernel(it_ref, jt_ref, slope_ref, q_ref, k_ref, v_ref, eq_ref, ek_ref, lam_ref, dn_ref, o_ref,
                 m_scr, acc_scr, *, tq, tk):
    hp = pl.program_id(1)
    step = pl.program_id(2)
    i = it_ref[step]
    j = jt_ref[step]
    ratio = tq // tk

    @pl.when(j == 0)
    def _():
        m_scr[...] = jnp.full(m_scr.shape, NEG, F32)
        acc_scr[...] = jnp.zeros(acc_scr.shape, F32)

    off = i * tq - j * tk
    cbs = [slope_ref[hp * ATTN_HEADS + u] * off.astype(F32) for u in range(ATTN_HEADS)]
    args = (q_ref, k_ref, v_ref, eq_ref, ek_ref, m_scr, acc_scr, off, cbs)

    @pl.when(j < i * ratio)
    def _():
        _attn_block(*args, masked=False, q0=0, tq=tq, tk=tk)

    for r in range(ratio):
        @pl.when(j == i * ratio + r)
        def _():
            _attn_block(*args, masked=True, q0=r * tk, tq=tq, tk=tk)

    @pl.when(j == (i + 1) * ratio - 1)
    def _():
        lv = lam_ref[...]
        lam = (jnp.exp(jnp.sum(lv[0:1] * lv[1:2], axis=-1, keepdims=True))
               - jnp.exp(jnp.sum(lv[2:3] * lv[3:4], axis=-1, keepdims=True)) + LAMBDA_INIT)
        vd = 2 * DA_HD
        for u in range(ATTN_HEADS):
            a0, a1 = acc_scr[2 * u], acc_scr[2 * u + 1]
            o = a0[0:vd] / a0[vd:vd + 1] - lam * (a1[0:vd] / a1[vd:vd + 1])
            o = o * lax.rsqrt(jnp.mean(o * o, axis=0, keepdims=True) + EPS)
            o_ref[0, :, u * LANES:(u + 1) * LANES] = (o.T * dn_ref[...] * (1.0 - LAMBDA_INIT)).astype(BF16)


def _alibi_columns(n, slopes, q_side):
    assert n <= 256 * 256 and all(math.log2(s).is_integer() for s in slopes)
    pos = np.arange(n)
    lo, hi = (pos % 256).astype(np.float64), (pos // 256 * 256).astype(np.float64)
    out = np.zeros((len(slopes), n, LANES), np.float64)
    for h, s in enumerate(slopes):
        cols = (-s * lo, -s * hi, np.ones(n), np.ones(n)) if q_side else (np.ones(n), np.ones(n), s * lo, s * hi)
        for base in (0, DA_HD):
            for c, v in enumerate(cols):
                out[h, :, base + c] = v
    return jnp.asarray(out, dtype=BF16)


def _attn(p, lamv, diff_norm, tq, tk):
    B, S, _ = p.shape
    nq, ratio = S // tq, tq // tk
    slopes = [2.0 ** (-8.0 * (h + 1) / DA_HEADS) for h in range(DA_HEADS)]
    steps = [(i, j) for i in range(nq) for j in range((i + 1) * ratio)]
    it = jnp.asarray([s[0] for s in steps], jnp.int32)
    jt = jnp.asarray([s[1] for s in steps], jnp.int32)
    hw = ATTN_HEADS * LANES
    qb, kb, vb = COL_DAQ // hw, COL_DAK // hw, COL_DAV // hw
    grid_spec = pltpu.PrefetchScalarGridSpec(
        num_scalar_prefetch=3,
        grid=(B, DA_HEADS // ATTN_HEADS, len(steps)),
        in_specs=[pl.BlockSpec((1, tq, hw), lambda b, h, s, it, jt, sl: (b, it[s], qb + h)),
                  pl.BlockSpec((1, tk, hw), lambda b, h, s, it, jt, sl: (b, jt[s], kb + h)),
                  pl.BlockSpec((1, tk, hw), lambda b, h, s, it, jt, sl: (b, jt[s], vb + h)),
                  pl.BlockSpec((ATTN_HEADS, tq, LANES), lambda b, h, s, it, jt, sl: (h, 0, 0)),
                  pl.BlockSpec((ATTN_HEADS, tk, LANES), lambda b, h, s, it, jt, sl: (h, 0, 0)),
                  pl.BlockSpec((4, DA_HD), lambda b, h, s, it, jt, sl: (0, 0)),
                  pl.BlockSpec((1, 2 * DA_HD), lambda b, h, s, it, jt, sl: (0, 0))],
        out_specs=pl.BlockSpec((1, tq, hw), lambda b, h, s, it, jt, sl: (b, it[s], h)),
        scratch_shapes=[pltpu.VMEM((2 * ATTN_HEADS, 1, tq), F32),
                        pltpu.VMEM((2 * ATTN_HEADS, 2 * DA_HD + ATTN_SUM_ROWS, tq), F32)],
    )
    return pl.pallas_call(
        functools.partial(_attn_kernel, tq=tq, tk=tk),
        grid_spec=grid_spec,
        out_shape=jax.ShapeDtypeStruct((B, S, DA_WIDTH), BF16),
        compiler_params=_cparams(("arbitrary",) * 3),
    )(it, jt, jnp.asarray(slopes, F32), p, p, p, _alibi_columns(tq, slopes, True),
      _alibi_columns(tk, slopes, False), lamv, diff_norm.reshape(1, 2 * DA_HD))


def _mlstm_kernel(qk_ref, v_ref, o_ref, g_ref, cw_ref, cb_ref, nw_ref, y_ref,
                  ext_scr, c_scr, n_scr, m_scr, *, L):
    i = pl.program_id(1)

    @pl.when(i == 0)
    def _():
        ext_scr[...] = jnp.zeros(ext_scr.shape, BF16)
        c_scr[...] = jnp.zeros(c_scr.shape, F32)
        n_scr[...] = jnp.zeros(n_scr.shape, F32)
        m_scr[...] = jnp.zeros(m_scr.shape, F32)

    raw = qk_ref[0]
    tail = ext_scr[...]
    t_out = lax.broadcasted_iota(jnp.int32, (L, L), 0)
    t_in = lax.broadcasted_iota(jnp.int32, (L, L), 1)
    h_out = lax.broadcasted_iota(jnp.int32, (CONV_TAIL, CONV_TAIL), 0)
    h_in = lax.broadcasted_iota(jnp.int32, (CONV_TAIL, CONV_TAIL), 1)
    conv = cb_ref[...] + raw.astype(F32) * cw_ref[CONV_W - 1:CONV_W, :]
    for k in range(1, CONV_W):
        shift = jnp.where(t_in == t_out - k, 1.0, 0.0).astype(BF16)
        head = jnp.where(h_in == h_out - k + CONV_TAIL, 1.0, 0.0).astype(BF16)
        xk = jnp.dot(shift, raw, preferred_element_type=F32)
        fix = jnp.dot(head, tail, preferred_element_type=F32)
        xk = jnp.concatenate([xk[:CONV_TAIL] + fix, xk[CONV_TAIL:]], axis=0)
        conv = conv + xk * cw_ref[CONV_W - 1 - k:CONV_W - k, :]
    ext_scr[...] = raw[L - CONV_TAIL:L, :]
    qkc = conv * _sigmoid(conv)

    gts = g_ref[0]
    fpre = pltpu.roll(gts, LANES - ML_HEADS, axis=1)
    lf = jnp.minimum(fpre, 0.0) - jnp.log(1.0 + jnp.exp(-jnp.abs(fpre)))
    row = lax.broadcasted_iota(jnp.int32, (L, L), 0)
    col = lax.broadcasted_iota(jnp.int32, (L, L), 1)
    causal = col <= row
    bcum = jnp.dot(causal.astype(F32), lf, precision=lax.Precision.HIGHEST, preferred_element_type=F32)
    r = gts - bcum
    rt = r.T
    m_all = m_scr[...]
    lane = lax.broadcasted_iota(jnp.int32, (1, LANES), 1)
    m_next = m_all
    v_all = v_ref[0]
    o_all = o_ref[0]
    for h in range(ML_HEADS):
        hs = slice(h * ML_HD, (h + 1) * ML_HD)
        bcol = bcum[:, h:h + 1]
        rcol = r[:, h:h + 1]
        rrow = rt[h:h + 1, :]
        g = bcum[L - 1:L, h:h + 1]
        mh = m_all[:, h:h + 1]
        dm = jnp.where(causal, bcol + rrow, NEG)
        inter = bcol + mh
        mj = jnp.maximum(inter, jnp.max(dm, axis=-1, keepdims=True))
        w_intra = jnp.exp(dm - mj)
        w_inter = jnp.exp(inter - mj)
        qh = qkc[:, hs]
        kh = qkc[:, ML_WIDTH + h * ML_HD:ML_WIDTH + (h + 1) * ML_HD] * (ML_HD ** -0.5)
        vh = v_all[:, hs]
        qb = qh.astype(BF16)
        kb = kh.astype(BF16)
        s = lax.dot_general(qb, kb, (((1,), (1,)), ((), ())), preferred_element_type=F32) * w_intra
        c_old = c_scr[h]
        n_old = n_scr[h:h + 1, :]
        num = (jnp.dot(s.astype(BF16), vh, preferred_element_type=F32)
               + lax.dot_general(qb, c_old.astype(BF16), (((1,), (1,)), ((), ())),
                                 preferred_element_type=F32) * w_inter)
        den = (jnp.sum(s, axis=-1, keepdims=True)
               + w_inter * jnp.sum(qh * n_old, axis=-1, keepdims=True))
        denom = jnp.maximum(jnp.abs(den), jnp.exp(-mj))
        ht = num / denom
        a_col = g + rcol
        m_new = jnp.maximum(g + mh, jnp.max(a_col, axis=0, keepdims=True))
        wa = jnp.exp(a_col - m_new)
        decay = jnp.exp(g + mh - m_new)
        vw_t = (vh.astype(F32) * wa).T.astype(BF16)
        c_scr[h] = decay * c_old + jnp.dot(vw_t, kb, preferred_element_type=F32)
        n_scr[h:h + 1, :] = decay * n_old + jnp.sum(kh * wa, axis=0, keepdims=True)
        m_next = jnp.where(lane == h, m_new, m_next)
        z = _sigmoid(o_all[:, hs].astype(F32)) * ht
        z = z * lax.rsqrt(jnp.mean(z * z, axis=-1, keepdims=True) + EPS) * nw_ref[...]
        y_ref[0, :, hs] = z.astype(BF16)
    m_scr[...] = m_next


def _mlstm(p, gates, conv_w, conv_b, mlstm_norm, L):
    B, S, _ = p.shape
    return pl.pallas_call(
        functools.partial(_mlstm_kernel, L=L),
        grid=(B, S // L),
        in_specs=[pl.BlockSpec((1, L, 2 * ML_WIDTH), lambda b, i: (b, i, COL_MLQK // (2 * ML_WIDTH))),
                  pl.BlockSpec((1, L, ML_WIDTH), lambda b, i: (b, i, COL_MLV // ML_WIDTH)),
                  pl.BlockSpec((1, L, ML_WIDTH), lambda b, i: (b, i, COL_MLO // ML_WIDTH)),
                  pl.BlockSpec((1, L, LANES), lambda b, i: (b, i, 0)),
                  pl.BlockSpec((CONV_W, 2 * ML_WIDTH), lambda b, i: (0, 0)),
                  pl.BlockSpec((1, 2 * ML_WIDTH), lambda b, i: (0, 0)),
                  pl.BlockSpec((1, ML_HD), lambda b, i: (0, 0))],
        out_specs=pl.BlockSpec((1, L, ML_WIDTH), lambda b, i: (b, i, 0)),
        out_shape=jax.ShapeDtypeStruct((B, S, ML_WIDTH), BF16),
        scratch_shapes=[pltpu.VMEM((CONV_TAIL, 2 * ML_WIDTH), BF16),
                        pltpu.VMEM((ML_HEADS, ML_HD, ML_HD), F32),
                        pltpu.VMEM((SUBLANES, ML_HD), F32),
                        pltpu.VMEM((1, LANES), F32)],
        compiler_params=_cparams(("arbitrary", "arbitrary")),
    )(p, p, p, gates, conv_w, conv_b.reshape(1, -1), mlstm_norm.reshape(1, ML_HD))


def _post_kernel(ya_ref, ym_ref, ga_ref, gm_ref, x_ref, mod_ref, wa_ref, wm_ref, wo_ref, n2_ref,
                 wr_ref, br_ref, x1_ref, h2_ref, ri_ref, rw_ref, cnt_ref, run_scr, *, tm):
    first = jnp.logical_and(pl.program_id(0) == 0, pl.program_id(1) == 0)

    @pl.when(first)
    def _():
        run_scr[...] = jnp.zeros(run_scr.shape, F32)

    a = jnp.dot(ya_ref[0], wa_ref[...], preferred_element_type=F32)
    m = jnp.dot(ym_ref[0], wm_ref[...], preferred_element_type=F32)
    merged = _sigmoid(ga_ref[0]).astype(F32) * a + _sigmoid(gm_ref[0]).astype(F32) * m
    o = jnp.dot(merged.astype(BF16), wo_ref[...], preferred_element_type=F32)
    x1 = x_ref[0] + mod_ref[0, 2:3, :] * o
    x1_ref[0] = x1
    h2 = x1 * lax.rsqrt(jnp.mean(x1 * x1, axis=-1, keepdims=True) + EPS) * n2_ref[...]
    h2 = h2 * (1.0 + mod_ref[0, 4:5, :]) + mod_ref[0, 3:4, :]
    _store_rows(h2_ref, (0,), h2)

    hi = h2.astype(BF16)
    lo = (h2 - hi.astype(F32)).astype(BF16)
    logits = lax.dot_general(wr_ref[...], jnp.concatenate([hi, lo, hi], axis=1), (((1,), (1,)), ((), ())),
                             preferred_element_type=F32) + br_ref[...]
    sub = lax.broadcasted_iota(jnp.int32, (ROUTE_ROWS, tm), 0)
    big = jnp.int32(4 * ROUTE_ROWS)
    gl = jnp.where(sub < N_GROUPS, logits, NEG)
    gmax = jnp.max(gl, axis=0, keepdims=True)
    gsel = jnp.min(jnp.where(gl == gmax, sub, big), axis=0, keepdims=True)
    pgrp = 1.0 / jnp.sum(jnp.exp(gl - gmax), axis=0, keepdims=True)
    first = N_GROUPS + EXPERTS_PER_GROUP * gsel
    el = jnp.where(jnp.logical_and(sub >= first, sub < first + EXPERTS_PER_GROUP), logits, NEG)
    e1 = jnp.max(el, axis=0, keepdims=True)
    i1 = jnp.min(jnp.where(el == e1, sub, big), axis=0, keepdims=True)
    el2 = jnp.where(sub == i1, NEG, el)
    e2 = jnp.max(el2, axis=0, keepdims=True)
    i2 = jnp.min(jnp.where(el2 == e2, sub, big), axis=0, keepdims=True)
    tt = jnp.exp(e2 - e1)
    w1 = pgrp / (1.0 + tt)
    w2 = pgrp * tt / (1.0 + tt)

    oh1 = jnp.where(sub == i1, 1.0, 0.0).astype(F32)
    oh2 = jnp.where(sub == i2, 1.0, 0.0).astype(F32)
    cat = jnp.concatenate([oh1, oh2], axis=0).astype(BF16)
    src = lax.broadcasted_iota(jnp.int32, (tm, tm), 0)
    dst = lax.broadcasted_iota(jnp.int32, (tm, tm), 1)
    before = jnp.where(src < dst, 1.0, 0.0).astype(BF16)
    earlier = jnp.dot(cat, before, preferred_element_type=F32)
    c1 = jnp.sum(oh1, axis=1, keepdims=True)
    c2 = jnp.sum(oh2, axis=1, keepdims=True)
    run = run_scr[...]
    rank1 = jnp.sum((earlier[:ROUTE_ROWS] + run) * oh1, axis=0, keepdims=True)
    rank2 = jnp.sum((earlier[ROUTE_ROWS:] + run + c1) * oh2, axis=0, keepdims=True)
    run_new = run + c1 + c2
    run_scr[...] = run_new
    cnt_ref[...] = jnp.broadcast_to(run_new, cnt_ref.shape)
    row = lax.broadcasted_iota(jnp.int32, (SUBLANES, tm), 0)
    ri_ref[0] = jnp.where(row == 0, i1 - N_GROUPS,
                          jnp.where(row == 1, i2 - N_GROUPS,
                                    jnp.where(row == 2, rank1.astype(jnp.int32),
                                              jnp.where(row == 3, rank2.astype(jnp.int32), 0))))
    rw_ref[0] = jnp.where(row == 0, w1, jnp.where(row == 1, w2, 0.0))


def _post(ya, ym, p, x, mod, wa, wm, wo, norm2, wr, br, tm):
    B, S, D = x.shape
    tok = lambda b, i: (b, i, 0)
    const = lambda b, i: (0, 0)
    return pl.pallas_call(
        functools.partial(_post_kernel, tm=tm),
        grid=(B, S // tm),
        in_specs=[pl.BlockSpec((1, tm, DA_WIDTH), tok),
                  pl.BlockSpec((1, tm, ML_WIDTH), tok),
                  pl.BlockSpec((1, tm, D), lambda b, i: (b, i, COL_GA // D_MODEL)),
                  pl.BlockSpec((1, tm, D), lambda b, i: (b, i, COL_GM // D_MODEL)),
                  pl.BlockSpec((1, tm, D), tok),
                  pl.BlockSpec((1, 6, D), lambda b, i: (b, 0, 0)),
                  pl.BlockSpec((DA_WIDTH, D), const),
                  pl.BlockSpec((ML_WIDTH, D), const),
                  pl.BlockSpec((D, D), const),
                  pl.BlockSpec((1, D), const),
                  pl.BlockSpec((ROUTE_ROWS, 3 * D), const),
                  pl.BlockSpec((ROUTE_ROWS, 1), const)],
        out_specs=[pl.BlockSpec((1, tm, D), tok),
                   pl.BlockSpec((1, tm * ROW_PARTS, LANES), tok),
                   pl.BlockSpec((1, SUBLANES, tm), lambda b, i: (b, 0, i)),
                   pl.BlockSpec((1, SUBLANES, tm), lambda b, i: (b, 0, i)),
                   pl.BlockSpec((ROUTE_ROWS, LANES), const)],
        out_shape=[jax.ShapeDtypeStruct((B, S, D), F32),
                   jax.ShapeDtypeStruct((B, S * ROW_PARTS, LANES), F32),
                   jax.ShapeDtypeStruct((B, SUBLANES, S), jnp.int32),
                   jax.ShapeDtypeStruct((B, SUBLANES, S), F32),
                   jax.ShapeDtypeStruct((ROUTE_ROWS, LANES), F32)],
        scratch_shapes=[pltpu.VMEM((ROUTE_ROWS, 1), F32)],
        compiler_params=_cparams(("arbitrary", "arbitrary")),
    )(ya, ym, p, p, x, mod, wa, wm, wo, norm2.reshape(1, D), wr, br)


DMA_UNROLL = 8


def _dispatch_kernel(ends_ref, pc_ref, dest_ref, h2_ref, xs_hbm, zbuf, sem, zsem, *, td):
    i = pl.program_id(0)
    tile_rows = EXPERT_ROWS * ROW_PARTS

    def zero_tile(first_row):
        start = pl.multiple_of(first_row * ROW_PARTS, tile_rows)
        cp = pltpu.make_async_copy(zbuf, xs_hbm.at[pl.ds(start, tile_rows)], zsem)
        cp.start()
        cp.wait()

    @pl.when(i == 0)
    def _():
        zbuf[...] = jnp.zeros(zbuf.shape, zbuf.dtype)
        for e in range(N_EXPERTS):
            @pl.when(pc_ref[e] > 0)
            def _():
                zero_tile(ends_ref[e] - EXPERT_ROWS)

        def fill(tile, carry):
            zero_tile(tile * EXPERT_ROWS)
            return carry

        lax.fori_loop(ends_ref[N_EXPERTS - 1] // EXPERT_ROWS, xs_hbm.shape[0] // tile_rows, fill, 0)

    def issue(t, carry):
        src = h2_ref.at[pl.ds(pl.multiple_of(t * ROW_PARTS, ROW_PARTS), ROW_PARTS)]
        for s in range(2):
            dst = pl.multiple_of(dest_ref[2 * t + s] * ROW_PARTS, ROW_PARTS)
            pltpu.make_async_copy(src, xs_hbm.at[pl.ds(dst, ROW_PARTS)], sem).start(priority=s)
        return carry

    lax.fori_loop(0, td, issue, 0, unroll=DMA_UNROLL)
    for s in range(2):
        pltpu.make_async_copy(h2_ref, xs_hbm.at[pl.ds(0, td * ROW_PARTS)], sem).wait()


def _dispatch(ends, pc, dest_flat, h2, n_rows, td):
    T = h2.shape[0] // ROW_PARTS
    grid_spec = pltpu.PrefetchScalarGridSpec(
        num_scalar_prefetch=2,
        grid=(T // td,),
        in_specs=[pl.BlockSpec((2 * td,), lambda i, e, c: (i,), memory_space=pltpu.SMEM),
                  pl.BlockSpec((td * ROW_PARTS, LANES), lambda i, e, c: (i, 0))],
        out_specs=pl.BlockSpec(memory_space=pl.ANY),
        scratch_shapes=[pltpu.VMEM((EXPERT_ROWS * ROW_PARTS, LANES), h2.dtype),
                        pltpu.SemaphoreType.DMA(()), pltpu.SemaphoreType.DMA(())],
    )
    return pl.pallas_call(
        functools.partial(_dispatch_kernel, td=td),
        grid_spec=grid_spec,
        out_shape=jax.ShapeDtypeStruct((n_rows * ROW_PARTS, LANES), h2.dtype),
        compiler_params=_cparams(("arbitrary",)),
    )(ends, pc, dest_flat, h2)


def _experts_kernel(te_ref, nt_ref, xs_ref, w1_ref, w3_ref, w2_ref, ys_ref):
    i = pl.program_id(0)

    @pl.when(i < nt_ref[0])
    def _():
        x = _load_rows(xs_ref, (), EXPERT_ROWS).astype(BF16)
        a = jnp.dot(x, w1_ref[0].astype(BF16), preferred_element_type=F32)
        b = jnp.dot(x, w3_ref[0].astype(BF16), preferred_element_type=F32)
        hid = (a * _sigmoid(a) * b).astype(BF16)
        _store_rows(ys_ref, (), jnp.dot(hid, w2_ref[0].astype(BF16), preferred_element_type=F32))

    @pl.when(i >= nt_ref[0])
    def _():
        ys_ref[...] = jnp.zeros(ys_ref.shape, ys_ref.dtype)


def _experts(tile_e, n_tiles, xs, w1, w3, w2):
    n_rows = xs.shape[0] // ROW_PARTS
    D = D_MODEL
    nt = n_rows // EXPERT_ROWS
    tile = (EXPERT_ROWS * ROW_PARTS, LANES)
    rows = lambda i, te, n: (jnp.minimum(i, jnp.maximum(n[0] - 1, 0)), 0)
    wsel = lambda i, te, n: (te[i], 0, 0)
    grid_spec = pltpu.PrefetchScalarGridSpec(
        num_scalar_prefetch=2,
        grid=(nt,),
        in_specs=[pl.BlockSpec(tile, rows),
                  pl.BlockSpec((1, D, D_EXPERT), wsel),
                  pl.BlockSpec((1, D, D_EXPERT), wsel),
                  pl.BlockSpec((1, D_EXPERT, D), wsel)],
        out_specs=pl.BlockSpec(tile, lambda i, te, n: (i, 0)),
    )
    return pl.pallas_call(
        _experts_kernel,
        grid_spec=grid_spec,
        out_shape=jax.ShapeDtypeStruct((n_rows * ROW_PARTS, LANES), F32),
        compiler_params=_cparams(("arbitrary",)),
    )(tile_e, n_tiles, xs, w1, w3, w2)


def _combine_kernel(dcur_ref, dnxt_ref, ys_hbm, x1_ref, rw_ref, mod_ref, nf_ref, o_ref, ybuf, sem, *, tc, nsteps):
    g = pl.program_id(0)
    slot = g % 2

    def issue(dest_ref, sl, t):
        dst = pl.ds(pl.multiple_of(t * ROW_PARTS, ROW_PARTS), ROW_PARTS)
        for s in range(2):
            src = pl.multiple_of(dest_ref[2 * t + s] * ROW_PARTS, ROW_PARTS)
            pltpu.make_async_copy(ys_hbm.at[pl.ds(src, ROW_PARTS)], ybuf.at[sl, s, dst],
                                  sem.at[sl]).start(priority=s)

    def drain(sl):
        for s in range(2):
            pltpu.make_async_copy(ys_hbm.at[pl.ds(0, tc * ROW_PARTS)], ybuf.at[sl, s], sem.at[sl]).wait()

    @pl.when(g == 0)
    def _():
        lax.fori_loop(0, tc, lambda t, c: (issue(dcur_ref, 0, t), c)[1], 0, unroll=DMA_UNROLL)

    drain(slot)
    for t in range(tc):
        issue(dnxt_ref, 1 - slot, t)

    rw = jnp.concatenate([rw_ref[0], jnp.zeros((LANES - SUBLANES, tc), F32)], axis=0).T
    y = rw[:, 0:1] * _load_rows(ybuf, (slot, 0), tc) + rw[:, 1:2] * _load_rows(ybuf, (slot, 1), tc)
    xo = x1_ref[0] + mod_ref[0, 5:6, :] * y
    o_ref[0] = xo * lax.rsqrt(jnp.mean(xo * xo, axis=-1, keepdims=True) + EPS) * nf_ref[...]

    @pl.when(g == nsteps - 1)
    def _():
        drain(1 - slot)


def _combine(dest_flat, ys, x1, rw, mod, norm_f, tc):
    B, S, D = x1.shape
    n = S // tc
    nsteps = B * n
    tok = lambda g: (g // n, g % n, 0)
    return pl.pallas_call(
        functools.partial(_combine_kernel, tc=tc, nsteps=nsteps),
        grid=(nsteps,),
        in_specs=[pl.BlockSpec((2 * tc,), lambda g: (g,), memory_space=pltpu.SMEM),
                  pl.BlockSpec((2 * tc,), lambda g: (jnp.minimum(g + 1, nsteps - 1),), memory_space=pltpu.SMEM),
                  pl.BlockSpec(memory_space=pl.ANY),
                  pl.BlockSpec((1, tc, D), tok),
                  pl.BlockSpec((1, SUBLANES, tc), lambda g: (g // n, 0, g % n)),
                  pl.BlockSpec((1, 6, D), lambda g: (g // n, 0, 0)),
                  pl.BlockSpec((1, D), lambda g: (0, 0))],
        out_specs=pl.BlockSpec((1, tc, D), tok),
        out_shape=jax.ShapeDtypeStruct((B, S, D), F32),
        scratch_shapes=[pltpu.VMEM((2, 2, tc * ROW_PARTS, LANES), F32), pltpu.SemaphoreType.DMA((2,))],
        compiler_params=_cparams(("arbitrary",)),
    )(dest_flat, dest_flat, ys, x1, rw, mod, norm_f.reshape(1, D))


def _pick(n, pref):
    t = min(n, pref)
    assert n % t == 0, (n, pref)
    return t


def kernel(x, c, w_ada, b_ada, norm1, w_in, b_if, conv_w, conv_b, lam_q1, lam_k1, lam_q2, lam_k2,
           diff_norm, mlstm_norm, w_br_a, w_br_m, w_out, norm2, w_rg, b_rg, w_re, b_re,
           w_e1, w_e3, w_e2, norm_f):
    B, S, D = x.shape
    assert D == D_MODEL and w_ada.shape[0] == 1
    T = B * S
    l = 0

    mod = _ada(c, w_ada[l], b_ada[l]).reshape(B, 6, D)

    w = w_in[l]
    o_q, o_k, o_v, o_qk, o_mv, o_mo, o_if, o_ga, o_gm = np.cumsum((0,) + (
        DA_WIDTH, DA_WIDTH, DA_WIDTH, 2 * ML_WIDTH, ML_WIDTH, ML_WIDTH, 2 * ML_HEADS, D_MODEL))
    w_perm = jnp.concatenate([
        w[:, o_qk:o_qk + 2 * ML_WIDTH], w[:, o_ga:o_ga + D], w[:, o_gm:o_gm + D],
        w[:, o_q:o_q + DA_WIDTH] * (DA_HD ** -0.5), w[:, o_k:o_k + DA_WIDTH], w[:, o_v:o_v + DA_WIDTH],
        w[:, o_mv:o_mv + ML_WIDTH], w[:, o_mo:o_mo + ML_WIDTH], w[:, o_if:o_if + 2 * ML_HEADS],
        jnp.zeros((D, LANES - 2 * ML_HEADS), F32)], axis=1).astype(BF16)
    bif = jnp.concatenate([b_if[l], jnp.zeros((LANES - 2 * ML_HEADS,), F32)]).reshape(1, LANES)

    p, gates = _proj(x, mod, norm1[l], bif, w_perm, _pick(S, PROJ_TM))

    lamv = jnp.stack([lam_q1[l], lam_k1[l], lam_q2[l], lam_k2[l]])
    ya = _attn(p, lamv, diff_norm[l], _pick(S, ATTN_TQ), _pick(S, ATTN_TK))
    ym = _mlstm(p, gates, conv_w[l], conv_b[l], mlstm_norm[l], _pick(S, MLSTM_L))

    pad_rows = ROUTE_ROWS - N_GROUPS - N_EXPERTS
    wr = jnp.concatenate([w_rg[l], w_re[l], jnp.zeros((D, pad_rows), F32)], axis=1).T
    wr_hi = wr.astype(BF16)
    wr_lo = (wr - wr_hi.astype(F32)).astype(BF16)
    wr = jnp.concatenate([wr_hi, wr_hi, wr_lo], axis=1)
    br = jnp.concatenate([b_rg[l], b_re[l], jnp.zeros((pad_rows,), F32)]).reshape(ROUTE_ROWS, 1)
    x1, h2, ri, rw, cnt = _post(ya, ym, p, x, mod, w_br_a[l].astype(BF16), w_br_m[l].astype(BF16),
                                w_out[l].astype(BF16), norm2[l], wr, br, _pick(S, POST_TM))

    counts = cnt[N_GROUPS:N_GROUPS + N_EXPERTS, 0].astype(jnp.int32)
    pc = ((counts + EXPERT_ROWS - 1) // EXPERT_ROWS) * EXPERT_ROWS
    ends = jnp.cumsum(pc)
    offs = ends - pc
    eid = ri[:, 0:2, :]
    rank = ri[:, 2:4, :]
    dest = rank + jnp.sum(jnp.where(eid[..., None] == jnp.arange(N_EXPERTS), offs, 0), axis=-1)
    dest_flat = dest.transpose(0, 2, 1).reshape(2 * T).astype(jnp.int32)
    n_rows = 2 * T + N_EXPERTS * EXPERT_ROWS
    n_tiles = n_rows // EXPERT_ROWS
    tile_e = jnp.sum(jnp.arange(n_tiles)[:, None] * EXPERT_ROWS >= ends[None, :], axis=1)
    tile_e = jnp.minimum(tile_e, N_EXPERTS - 1).astype(jnp.int32)
    used_tiles = (ends[-1:] // EXPERT_ROWS).astype(jnp.int32)

    xs = _dispatch(ends.astype(jnp.int32), pc.astype(jnp.int32), dest_flat, h2.reshape(T * ROW_PARTS, LANES), n_rows,
                   _pick(T, SMEM_BLOCK_1D))
    w1 = w_e1[l].reshape(N_EXPERTS, D, D_EXPERT)
    w3 = w_e3[l].reshape(N_EXPERTS, D, D_EXPERT)
    w2 = w_e2[l].reshape(N_EXPERTS, D_EXPERT, D)
    ys = _experts(tile_e, used_tiles, xs, w1, w3, w2)
    return _combine(dest_flat, ys, x1, rw, mod, norm_f, _pick(S, SMEM_BLOCK_1D // 2))
```

```python
import functools
import math

import jax
import jax.numpy as jnp
import numpy as np
from jax import lax
from jax.experimental import pallas as pl
from jax.experimental.pallas import tpu as pltpu

F32 = jnp.float32
BF16 = jnp.bfloat16

D_MODEL = 1024
DA_HEADS = 4
DA_HD = 64
DA_WIDTH = DA_HEADS * 2 * DA_HD
ML_HEADS = 4
ML_HD = 128
ML_WIDTH = ML_HEADS * ML_HD
CONV_W = 4
N_GROUPS = 4
EXPERTS_PER_GROUP = 8
N_EXPERTS = N_GROUPS * EXPERTS_PER_GROUP
D_EXPERT = 256
EPS = 1e-6
LAMBDA_INIT = 0.8 - 0.6 * math.exp(-0.3 * 0)

LANES = 128
SUBLANES = 8
SMEM_BLOCK_1D = 1024
NEG = -1e30
VMEM_LIMIT = 56 * 1024 * 1024

COL_MLQK = 0
COL_GA = 1024
COL_GM = 2048
COL_DAQ = 3072
COL_DAK = 3584
COL_DAV = 4096
COL_MLV = 4608
COL_MLO = 5120
COL_IF = 5632
N_PROJ = COL_IF + LANES
PROJ_TN = 1920
assert N_PROJ % PROJ_TN == 0

EXPERT_ROWS = 512
PROJ_TM = 1024
ATTN_TQ = 1024
ATTN_TK = 512
MLSTM_L = 256
CONV_TAIL = 16
POST_TM = 512
ROUTE_ROWS = 48


def _cparams(sem):
    return pltpu.CompilerParams(dimension_semantics=sem, vmem_limit_bytes=VMEM_LIMIT)


ROW_PARTS = D_MODEL // LANES
assert ROW_PARTS == SUBLANES


def _store_rows(ref, lead, x):
    n = x.shape[0]
    for a in range(ROW_PARTS):
        ref[lead + (pl.ds(a, n, stride=ROW_PARTS), slice(None))] = x[:, a * LANES:(a + 1) * LANES]


def _load_rows(ref, lead, n):
    return jnp.concatenate([ref[lead + (pl.ds(a, n, stride=ROW_PARTS), slice(None))] for a in range(ROW_PARTS)],
                           axis=1)


def _sigmoid(x):
    return 0.5 * jnp.tanh(0.5 * x) + 0.5


def _ada_kernel(c_ref, w_ref, b_ref, o_ref):
    c = c_ref[...]
    cs = c * _sigmoid(c)
    o_ref[...] = jnp.dot(cs, w_ref[0], precision=lax.Precision.HIGHEST,
                         preferred_element_type=F32) + b_ref[...]


def _ada(c, w, b):
    B, D = c.shape
    N = w.shape[2]
    tn = 1536
    return pl.pallas_call(
        _ada_kernel,
        grid=(N // tn,),
        in_specs=[pl.BlockSpec((B, D), lambda j: (0, 0)),
                  pl.BlockSpec((1, D, tn), lambda j: (0, 0, j)),
                  pl.BlockSpec((1, tn), lambda j: (0, j))],
        out_specs=pl.BlockSpec((B, tn), lambda j: (0, j)),
        out_shape=jax.ShapeDtypeStruct((B, N), F32),
        compiler_params=_cparams(("arbitrary",)),
    )(c, w, b)


def _proj_kernel(x_ref, mod_ref, g_ref, bif_ref, w_ref, p_ref, gate_ref, h_scr, *, nj):
    j = pl.program_id(2)

    @pl.when(j == 0)
    def _():
        x = x_ref[0]
        y = x * lax.rsqrt(jnp.mean(x * x, axis=-1, keepdims=True) + EPS) * g_ref[...]
        h = y * (1.0 + mod_ref[0, 1:2, :]) + mod_ref[0, 0:1, :]
        h_scr[...] = h.astype(BF16)

    acc = jnp.dot(h_scr[...], w_ref[...], preferred_element_type=F32)
    p_ref[0] = acc.astype(BF16)

    @pl.when(j == nj - 1)
    def _():
        gate_ref[0] = acc[:, PROJ_TN - LANES:] + bif_ref[...]


def _proj(x, mod, norm1, bif, w_bf16, tm):
    B, S, D = x.shape
    nj = N_PROJ // PROJ_TN
    return pl.pallas_call(
        functools.partial(_proj_kernel, nj=nj),
        grid=(B, S // tm, nj),
        in_specs=[pl.BlockSpec((1, tm, D), lambda b, i, j: (b, i, 0)),
                  pl.BlockSpec((1, 6, D), lambda b, i, j: (b, 0, 0)),
                  pl.BlockSpec((1, D), lambda b, i, j: (0, 0)),
                  pl.BlockSpec((1, LANES), lambda b, i, j: (0, 0)),
                  pl.BlockSpec((D, PROJ_TN), lambda b, i, j: (0, j))],
        out_specs=[pl.BlockSpec((1, tm, PROJ_TN), lambda b, i, j: (b, i, j)),
                   pl.BlockSpec((1, tm, LANES), lambda b, i, j: (b, i, 0))],
        out_shape=[jax.ShapeDtypeStruct((B, S, N_PROJ), BF16),
                   jax.ShapeDtypeStruct((B, S, LANES), F32)],
        scratch_shapes=[pltpu.VMEM((tm, D), BF16)],
        compiler_params=_cparams(("arbitrary", "arbitrary", "arbitrary")),
    )(x, mod, norm1.reshape(1, D), bif, w_bf16)


ATTN_SUM_ROWS = 16

ATTN_HEADS = 2


def _attn_block(q_ref, k_ref, v_ref, eq_ref, ek_ref, m_scr, acc_scr, off, cbs, masked, q0, tq, tk):
    qs = slice(q0, tq)
    lane = lax.broadcasted_iota(jnp.int32, (1, LANES), 1)
    if masked:
        key = lax.broadcasted_iota(jnp.int32, (tk, tq - q0), 0)
        qry = lax.broadcasted_iota(jnp.int32, (tk, tq - q0), 1) + q0
        valid = (key - qry) <= off
    scores = []
    for u in range(ATTN_HEADS):
        hl = slice(u * LANES, (u + 1) * LANES)
        q = q_ref[0, qs, hl]
        k = k_ref[0, :, hl]
        for c in range(2):
            sel = (lane < DA_HD) if c == 0 else (lane >= DA_HD)
            s = lax.dot_general(jnp.where(sel, k, ek_ref[u]), jnp.where(sel, q, eq_ref[u, qs]),
                                (((1,), (1,)), ((), ())), preferred_element_type=F32)
            scores.append(jnp.where(valid, s, NEG) if masked else s)
    for u in range(ATTN_HEADS):
        vt = jnp.concatenate([v_ref[0, :, u * LANES:(u + 1) * LANES].T,
                              jnp.ones((ATTN_SUM_ROWS, tk), BF16)], axis=0)
        for c in range(2):
            n = 2 * u + c
            s = scores[n]
            m_prev = m_scr[n, :, qs]
            m_new = jnp.maximum(m_prev, jnp.max(s, axis=0, keepdims=True) - cbs[u])
            alpha = jnp.exp(m_prev - m_new)
            p = jnp.exp((s - (m_new + cbs[u])).astype(BF16))
            acc_scr[n, :, qs] = alpha * acc_scr[n, :, qs] + jnp.dot(vt, p, preferred_element_type=F32)
            m_scr[n, :, qs] = m_new


def _attn_kernel(it_ref, jt_ref, slope_ref, q_ref, k_ref, v_ref, eq_ref, ek_ref, lam_ref, dn_ref, o_ref,
                 m_scr, acc_scr, *, tq, tk):
    hp = pl.program_id(1)
    step = pl.program_id(2)
    i = it_ref[step]
    j = jt_ref[step]
    ratio = tq // tk

    @pl.when(j == 0)
    def _():
        m_scr[...] = jnp.full(m_scr.shape, NEG, F32)
        acc_scr[...] = jnp.zeros(acc_scr.shape, F32)

    off = i * tq - j * tk
    cbs = [slope_ref[hp * ATTN_HEADS + u] * off.astype(F32) for u in range(ATTN_HEADS)]
    args = (q_ref, k_ref, v_ref, eq_ref, ek_ref, m_scr, acc_scr, off, cbs)

    @pl.when(j < i * ratio)
    def _():
        _attn_block(*args, masked=False, q0=0, tq=tq, tk=tk)

    for r in range(ratio):
        @pl.when(j == i * ratio + r)
        def _():
            _attn_block(*args, masked=True, q0=r * tk, tq=tq, tk=tk)

    @pl.when(j == (i + 1) * ratio - 1)
    def _():
        lv = lam_ref[...]
        lam = (jnp.exp(jnp.sum(lv[0:1] * lv[1:2], axis=-1, keepdims=True))
               - jnp.exp(jnp.sum(lv[2:3] * lv[3:4], axis=-1, keepdims=True)) + LAMBDA_INIT)
        vd = 2 * DA_HD
        for u in range(ATTN_HEADS):
            a0, a1 = acc_scr[2 * u], acc_scr[2 * u + 1]
            o = a0[0:vd] / a0[vd:vd + 1] - lam * (a1[0:vd] / a1[vd:vd + 1])
            o = o * lax.rsqrt(jnp.mean(o * o, axis=0, keepdims=True) + EPS)
            o_ref[0, :, u * LANES:(u + 1) * LANES] = (o.T * dn_ref[...] * (1.0 - LAMBDA_INIT)).astype(BF16)


def _alibi_columns(n, slopes, q_side):
    assert n <= 256 * 256 and all(math.log2(s).is_integer() for s in slopes)
    pos = np.arange(n)
    lo, hi = (pos % 256).astype(np.float64), (pos // 256 * 256).astype(np.float64)
    out = np.zeros((len(slopes), n, LANES), np.float64)
    for h, s in enumerate(slopes):
        cols = (-s * lo, -s * hi, np.ones(n), np.ones(n)) if q_side else (np.ones(n), np.ones(n), s * lo, s * hi)
        for base in (0, DA_HD):
            for c, v in enumerate(cols):
                out[h, :, base + c] = v
    return jnp.asarray(out, dtype=BF16)


def _attn(p, lamv, diff_norm, tq, tk):
    B, S, _ = p.shape
    nq, ratio = S // tq, tq // tk
    slopes = [2.0 ** (-8.0 * (h + 1) / DA_HEADS) for h in range(DA_HEADS)]
    steps = [(i, j) for i in range(nq) for j in range((i + 1) * ratio)]
    it = jnp.asarray([s[0] for s in steps], jnp.int32)
    jt = jnp.asarray([s[1] for s in steps], jnp.int32)
    hw = ATTN_HEADS * LANES
    qb, kb, vb = COL_DAQ // hw, COL_DAK // hw, COL_DAV // hw
    grid_spec = pltpu.PrefetchScalarGridSpec(
        num_scalar_prefetch=3,
        grid=(B, DA_HEADS // ATTN_HEADS, len(steps)),
        in_specs=[pl.BlockSpec((1, tq, hw), lambda b, h, s, it, jt, sl: (b, it[s], qb + h)),
                  pl.BlockSpec((1, tk, hw), lambda b, h, s, it, jt, sl: (b, jt[s], kb + h)),
                  pl.BlockSpec((1, tk, hw), lambda b, h, s, it, jt, sl: (b, jt[s], vb + h)),
                  pl.BlockSpec((ATTN_HEADS, tq, LANES), lambda b, h, s, it, jt, sl: (h, 0, 0)),
                  pl.BlockSpec((ATTN_HEADS, tk, LANES), lambda b, h, s, it, jt, sl: (h, 0, 0)),
                  pl.BlockSpec((4, DA_HD), lambda b, h, s, it, jt, sl: (0, 0)),
                  pl.BlockSpec((1, 2 * DA_HD), lambda b, h, s, it, jt, sl: (0, 0))],
        out_specs=pl.BlockSpec((1, tq, hw), lambda b, h, s, it, jt, sl: (b, it[s], h)),
        scratch_shapes=[pltpu.VMEM((2 * ATTN_HEADS, 1, tq), F32),
                        pltpu.VMEM((2 * ATTN_HEADS, 2 * DA_HD + ATTN_SUM_ROWS, tq), F32)],
    )
    return pl.pallas_call(
        functools.partial(_attn_kernel, tq=tq, tk=tk),
        grid_spec=grid_spec,
        out_shape=jax.ShapeDtypeStruct((B, S, DA_WIDTH), BF16),
        compiler_params=_cparams(("arbitrary",) * 3),
    )(it, jt, jnp.asarray(slopes, F32), p, p, p, _alibi_columns(tq, slopes, True),
      _alibi_columns(tk, slopes, False), lamv, diff_norm.reshape(1, 2 * DA_HD))


def _mlstm_kernel(qk_ref, v_ref, o_ref, g_ref, cw_ref, cb_ref, nw_ref, y_ref,
                  ext_scr, c_scr, n_scr, m_scr, *, L):
    i = pl.program_id(1)

    @pl.when(i == 0)
    def _():
        ext_scr[...] = jnp.zeros(ext_scr.shape, BF16)
        c_scr[...] = jnp.zeros(c_scr.shape, F32)
        n_scr[...] = jnp.zeros(n_scr.shape, F32)
        m_scr[...] = jnp.zeros(m_scr.shape, F32)

    raw = qk_ref[0]
    tail = ext_scr[...]
    t_out = lax.broadcasted_iota(jnp.int32, (L, L), 0)
    t_in = lax.broadcasted_iota(jnp.int32, (L, L), 1)
    h_out = lax.broadcasted_iota(jnp.int32, (CONV_TAIL, CONV_TAIL), 0)
    h_in = lax.broadcasted_iota(jnp.int32, (CONV_TAIL, CONV_TAIL), 1)
    conv = cb_ref[...] + raw.astype(F32) * cw_ref[CONV_W - 1:CONV_W, :]
    for k in range(1, CONV_W):
        shift = jnp.where(t_in == t_out - k, 1.0, 0.0).astype(BF16)
        head = jnp.where(h_in == h_out - k + CONV_TAIL, 1.0, 0.0).astype(BF16)
        xk = jnp.dot(shift, raw, preferred_element_type=F32)
        fix = jnp.dot(head, tail, preferred_element_type=F32)
        xk = jnp.concatenate([xk[:CONV_TAIL] + fix, xk[CONV_TAIL:]], axis=0)
        conv = conv + xk * cw_ref[CONV_W - 1 - k:CONV_W - k, :]
    ext_scr[...] = raw[L - CONV_TAIL:L, :]
    qkc = conv * _sigmoid(conv)

    gts = g_ref[0]
    fpre = pltpu.roll(gts, LANES - ML_HEADS, axis=1)
    lf = jnp.minimum(fpre, 0.0) - jnp.log(1.0 + jnp.exp(-jnp.abs(fpre)))
    row = lax.broadcasted_iota(jnp.int32, (L, L), 0)
    col = lax.broadcasted_iota(jnp.int32, (L, L), 1)
    causal = col <= row
    bcum = jnp.dot(causal.astype(F32), lf, precision=lax.Precision.HIGHEST, preferred_element_type=F32)
    r = gts - bcum
    rt = r.T
    m_all = m_scr[...]
    lane = lax.broadcasted_iota(jnp.int32, (1, LANES), 1)
    m_next = m_all
    v_all = v_ref[0]
    o_all = o_ref[0]
    for h in range(ML_HEADS):
        hs = slice(h * ML_HD, (h + 1) * ML_HD)
        bcol = bcum[:, h:h + 1]
        rcol = r[:, h:h + 1]
        rrow = rt[h:h + 1, :]
        g = bcum[L - 1:L, h:h + 1]
        mh = m_all[:, h:h + 1]
        dm = jnp.where(causal, bcol + rrow, NEG)
        inter = bcol + mh
        mj = jnp.maximum(inter, jnp.max(dm, axis=-1, keepdims=True))
        w_intra = jnp.exp(dm - mj)
        w_inter = jnp.exp(inter - mj)
        qh = qkc[:, hs]
        kh = qkc[:, ML_WIDTH + h * ML_HD:ML_WIDTH + (h + 1) * ML_HD] * (ML_HD ** -0.5)
        vh = v_all[:, hs]
        qb = qh.astype(BF16)
        kb = kh.astype(BF16)
        s = lax.dot_general(qb, kb, (((1,), (1,)), ((), ())), preferred_element_type=F32) * w_intra
        c_old = c_scr[h]
        n_old = n_scr[h:h + 1, :]
        num = (jnp.dot(s.astype(BF16), vh, preferred_element_type=F32)
               + lax.dot_general(qb, c_old.astype(BF16), (((1,), (1,)), ((), ())),
                                 preferred_element_type=F32) * w_inter)
        den = (jnp.sum(s, axis=-1, keepdims=True)
               + w_inter * jnp.sum(qh * n_old, axis=-1, keepdims=True))
        denom = jnp.maximum(jnp.abs(den), jnp.exp(-mj))
        ht = num / denom
        a_col = g + rcol
        m_new = jnp.maximum(g + mh, jnp.max(a_col, axis=0, keepdims=True))
        wa = jnp.exp(a_col - m_new)
        decay = jnp.exp(g + mh - m_new)
        vw_t = (vh.astype(F32) * wa).T.astype(BF16)
        c_scr[h] = decay * c_old + jnp.dot(vw_t, kb, preferred_element_type=F32)
        n_scr[h:h + 1, :] = decay * n_old + jnp.sum(kh * wa, axis=0, keepdims=True)
        m_next = jnp.where(lane == h, m_new, m_next)
        z = _sigmoid(o_all[:, hs].astype(F32)) * ht
        z = z * lax.rsqrt(jnp.mean(z * z, axis=-1, keepdims=True) + EPS) * nw_ref[...]
        y_ref[0, :, hs] = z.astype(BF16)
    m_scr[...] = m_next


def _mlstm(p, gates, conv_w, conv_b, mlstm_norm, L):
    B, S, _ = p.shape
    return pl.pallas_call(
        functools.partial(_mlstm_kernel, L=L),
        grid=(B, S // L),
        in_specs=[pl.BlockSpec((1, L, 2 * ML_WIDTH), lambda b, i: (b, i, COL_MLQK // (2 * ML_WIDTH))),
                  pl.BlockSpec((1, L, ML_WIDTH), lambda b, i: (b, i, COL_MLV // ML_WIDTH)),
                  pl.BlockSpec((1, L, ML_WIDTH), lambda b, i: (b, i, COL_MLO // ML_WIDTH)),
                  pl.BlockSpec((1, L, LANES), lambda b, i: (b, i, 0)),
                  pl.BlockSpec((CONV_W, 2 * ML_WIDTH), lambda b, i: (0, 0)),
                  pl.BlockSpec((1, 2 * ML_WIDTH), lambda b, i: (0, 0)),
                  pl.BlockSpec((1, ML_HD), lambda b, i: (0, 0))],
        out_specs=pl.BlockSpec((1, L, ML_WIDTH), lambda b, i: (b, i, 0)),
        out_shape=jax.ShapeDtypeStruct((B, S, ML_WIDTH), BF16),
        scratch_shapes=[pltpu.VMEM((CONV_TAIL, 2 * ML_WIDTH), BF16),
                        pltpu.VMEM((ML_HEADS, ML_HD, ML_HD), F32),
                        pltpu.VMEM((SUBLANES, ML_HD), F32),
                        pltpu.VMEM((1, LANES), F32)],
        compiler_params=_cparams(("arbitrary", "arbitrary")),
    )(p, p, p, gates, conv_w, conv_b.reshape(1, -1), mlstm_norm.reshape(1, ML_HD))


def _post_kernel(ya_ref, ym_ref, ga_ref, gm_ref, x_ref, mod_ref, wa_ref, wm_ref, wo_ref, n2_ref,
                 wr_ref, br_ref, x1_ref, h2_ref, ri_ref, rw_ref, cnt_ref, run_scr, *, tm):
    first = jnp.logical_and(pl.program_id(0) == 0, pl.program_id(1) == 0)

    @pl.when(first)
    def _():
        run_scr[...] = jnp.zeros(run_scr.shape, F32)

    a = jnp.dot(ya_ref[0], wa_ref[...], preferred_element_type=F32)
    m = jnp.dot(ym_ref[0], wm_ref[...], preferred_element_type=F32)
    merged = _sigmoid(ga_ref[0]).astype(F32) * a + _sigmoid(gm_ref[0]).astype(F32) * m
    o = jnp.dot(merged.astype(BF16), wo_ref[...], preferred_element_type=F32)
    x1 = x_ref[0] + mod_ref[0, 2:3, :] * o
    x1_ref[0] = x1
    h2 = x1 * lax.rsqrt(jnp.mean(x1 * x1, axis=-1, keepdims=True) + EPS) * n2_ref[...]
    h2 = h2 * (1.0 + mod_ref[0, 4:5, :]) + mod_ref[0, 3:4, :]
    _store_rows(h2_ref, (0,), h2)

    hi = h2.astype(BF16)
    lo = (h2 - hi.astype(F32)).astype(BF16)
    logits = lax.dot_general(wr_ref[...], jnp.concatenate([hi, lo, hi], axis=1), (((1,), (1,)), ((), ())),
                             preferred_element_type=F32) + br_ref[...]
    sub = lax.broadcasted_iota(jnp.int32, (ROUTE_ROWS, tm), 0)
    big = jnp.int32(4 * ROUTE_ROWS)
    gl = jnp.where(sub < N_GROUPS, logits, NEG)
    gmax = jnp.max(gl, axis=0, keepdims=True)
    gsel = jnp.min(jnp.where(gl == gmax, sub, big), axis=0, keepdims=True)
    pgrp = 1.0 / jnp.sum(jnp.exp(gl - gmax), axis=0, keepdims=True)
    first = N_GROUPS + EXPERTS_PER_GROUP * gsel
    el = jnp.where(jnp.logical_and(sub >= first, sub < first + EXPERTS_PER_GROUP), logits, NEG)
    e1 = jnp.max(el, axis=0, keepdims=True)
    i1 = jnp.min(jnp.where(el == e1, sub, big), axis=0, keepdims=True)
    el2 = jnp.where(sub == i1, NEG, el)
    e2 = jnp.max(el2, axis=0, keepdims=True)
    i2 = jnp.min(jnp.where(el2 == e2, sub, big), axis=0, keepdims=True)
    tt = jnp.exp(e2 - e1)
    w1 = pgrp / (1.0 + tt)
    w2 = pgrp * tt / (1.0 + tt)

    oh1 = jnp.where(sub == i1, 1.0, 0.0).astype(F32)
    oh2 = jnp.where(sub == i2, 1.0, 0.0).astype(F32)
    cat = jnp.concatenate([oh1, oh2], axis=0).astype(BF16)
    src = lax.broadcasted_iota(jnp.int32, (tm, tm), 0)
    dst = lax.broadcasted_iota(jnp.int32, (tm, tm), 1)
    before = jnp.where(src < dst, 1.0, 0.0).astype(BF16)
    earlier = jnp.dot(cat, before, preferred_element_type=F32)
    c1 = jnp.sum(oh1, axis=1, keepdims=True)
    c2 = jnp.sum(oh2, axis=1, keepdims=True)
    run = run_scr[...]
    rank1 = jnp.sum((earlier[:ROUTE_ROWS] + run) * oh1, axis=0, keepdims=True)
    rank2 = jnp.sum((earlier[ROUTE_ROWS:] + run + c1) * oh2, axis=0, keepdims=True)
    run_new = run + c1 + c2
    run_scr[...] = run_new
    cnt_ref[...] = jnp.broadcast_to(run_new, cnt_ref.shape)
    row = lax.broadcasted_iota(jnp.int32, (SUBLANES, tm), 0)
    ri_ref[0] = jnp.where(row == 0, i1 - N_GROUPS,
                          jnp.where(row == 1, i2 - N_GROUPS,
                                    jnp.where(row == 2, rank1.astype(jnp.int32),
                                              jnp.where(row == 3, rank2.astype(jnp.int32), 0))))
    rw_ref[0] = jnp.where(row == 0, w1, jnp.where(row == 1, w2, 0.0))


def _post(ya, ym, p, x, mod, wa, wm, wo, norm2, wr, br, tm):
    B, S, D = x.shape
    tok = lambda b, i: (b, i, 0)
    const = lambda b, i: (0, 0)
    return pl.pallas_call(
        functools.partial(_post_kernel, tm=tm),
        grid=(B, S // tm),
        in_specs=[pl.BlockSpec((1, tm, DA_WIDTH), tok),
                  pl.BlockSpec((1, tm, ML_WIDTH), tok),
                  pl.BlockSpec((1, tm, D), lambda b, i: (b, i, COL_GA // D_MODEL)),
                  pl.BlockSpec((1, tm, D), lambda b, i: (b, i, COL_GM // D_MODEL)),
                  pl.BlockSpec((1, tm, D), tok),
                  pl.BlockSpec((1, 6, D), lambda b, i: (b, 0, 0)),
                  pl.BlockSpec((DA_WIDTH, D), const),
                  pl.BlockSpec((ML_WIDTH, D), const),
                  pl.BlockSpec((D, D), const),
                  pl.BlockSpec((1, D), const),
                  pl.BlockSpec((ROUTE_ROWS, 3 * D), const),
                  pl.BlockSpec((ROUTE_ROWS, 1), const)],
        out_specs=[pl.BlockSpec((1, tm, D), tok),
                   pl.BlockSpec((1, tm * ROW_PARTS, LANES), tok),
                   pl.BlockSpec((1, SUBLANES, tm), lambda b, i: (b, 0, i)),
                   pl.BlockSpec((1, SUBLANES, tm), lambda b, i: (b, 0, i)),
                   pl.BlockSpec((ROUTE_ROWS, LANES), const)],
        out_shape=[jax.ShapeDtypeStruct((B, S, D), F32),
                   jax.ShapeDtypeStruct((B, S * ROW_PARTS, LANES), F32),
                   jax.ShapeDtypeStruct((B, SUBLANES, S), jnp.int32),
                   jax.ShapeDtypeStruct((B, SUBLANES, S), F32),
                   jax.ShapeDtypeStruct((ROUTE_ROWS, LANES), F32)],
        scratch_shapes=[pltpu.VMEM((ROUTE_ROWS, 1), F32)],
        compiler_params=_cparams(("arbitrary", "arbitrary")),
    )(ya, ym, p, p, x, mod, wa, wm, wo, norm2.reshape(1, D), wr, br)


DMA_UNROLL = 8


def _dispatch_kernel(ends_ref, pc_ref, dest_ref, h2_ref, xs_hbm, zbuf, sem, zsem, *, td):
    i = pl.program_id(0)
    tile_rows = EXPERT_ROWS * ROW_PARTS

    def zero_tile(first_row):
        start = pl.multiple_of(first_row * ROW_PARTS, tile_rows)
        cp = pltpu.make_async_copy(zbuf, xs_hbm.at[pl.ds(start, tile_rows)], zsem)
        cp.start()
        cp.wait()

    @pl.when(i == 0)
    def _():
        zbuf[...] = jnp.zeros(zbuf.shape, zbuf.dtype)
        for e in range(N_EXPERTS):
            @pl.when(pc_ref[e] > 0)
            def _():
                zero_tile(ends_ref[e] - EXPERT_ROWS)

        def fill(tile, carry):
            zero_tile(tile * EXPERT_ROWS)
            return carry

        lax.fori_loop(ends_ref[N_EXPERTS - 1] // EXPERT_ROWS, xs_hbm.shape[0] // tile_rows, fill, 0)

    def issue(t, carry):
        src = h2_ref.at[pl.ds(pl.multiple_of(t * ROW_PARTS, ROW_PARTS), ROW_PARTS)]
        for s in range(2):
            dst = pl.multiple_of(dest_ref[2 * t + s] * ROW_PARTS, ROW_PARTS)
            pltpu.make_async_copy(src, xs_hbm.at[pl.ds(dst, ROW_PARTS)], sem).start(priority=s)
        return carry

    lax.fori_loop(0, td, issue, 0, unroll=DMA_UNROLL)
    for s in range(2):
        pltpu.make_async_copy(h2_ref, xs_hbm.at[pl.ds(0, td * ROW_PARTS)], sem).wait()


def _dispatch(ends, pc, dest_flat, h2, n_rows, td):
    T = h2.shape[0] // ROW_PARTS
    grid_spec = pltpu.PrefetchScalarGridSpec(
        num_scalar_prefetch=2,
        grid=(T // td,),
        in_specs=[pl.BlockSpec((2 * td,), lambda i, e, c: (i,), memory_space=pltpu.SMEM),
                  pl.BlockSpec((td * ROW_PARTS, LANES), lambda i, e, c: (i, 0))],
        out_specs=pl.BlockSpec(memory_space=pl.ANY),
        scratch_shapes=[pltpu.VMEM((EXPERT_ROWS * ROW_PARTS, LANES), h2.dtype),
                        pltpu.SemaphoreType.DMA(()), pltpu.SemaphoreType.DMA(())],
    )
    return pl.pallas_call(
        functools.partial(_dispatch_kernel, td=td),
        grid_spec=grid_spec,
        out_shape=jax.ShapeDtypeStruct((n_rows * ROW_PARTS, LANES), h2.dtype),
        compiler_params=_cparams(("arbitrary",)),
    )(ends, pc, dest_flat, h2)


def _experts_kernel(te_ref, nt_ref, xs_ref, w1_ref, w3_ref, w2_ref, ys_ref):
    i = pl.program_id(0)

    @pl.when(i < nt_ref[0])
    def _():
        x = _load_rows(xs_ref, (), EXPERT_ROWS).astype(BF16)
        a = jnp.dot(x, w1_ref[0, 0, 0].astype(BF16), preferred_element_type=F32)
        b = jnp.dot(x, w3_ref[0, 0, 0].astype(BF16), preferred_element_type=F32)
        hid = (a * _sigmoid(a) * b).astype(BF16)
        _store_rows(ys_ref, (), jnp.dot(hid, w2_ref[0, 0, 0].astype(BF16), preferred_element_type=F32))

    @pl.when(i >= nt_ref[0])
    def _():
        ys_ref[...] = jnp.zeros(ys_ref.shape, ys_ref.dtype)


def _experts(tile_e, n_tiles, xs, w1, w3, w2):
    n_rows = xs.shape[0] // ROW_PARTS
    D = D_MODEL
    nt = n_rows // EXPERT_ROWS
    tile = (EXPERT_ROWS * ROW_PARTS, LANES)
    rows = lambda i, te, n: (jnp.minimum(i, jnp.maximum(n[0] - 1, 0)), 0)
    wsel = lambda i, te, n: (0, te[i] // EXPERTS_PER_GROUP, te[i] % EXPERTS_PER_GROUP, 0, 0)
    grid_spec = pltpu.PrefetchScalarGridSpec(
        num_scalar_prefetch=2,
        grid=(nt,),
        in_specs=[pl.BlockSpec(tile, rows),
                  pl.BlockSpec((1, 1, 1, D, D_EXPERT), wsel),
                  pl.BlockSpec((1, 1, 1, D, D_EXPERT), wsel),
                  pl.BlockSpec((1, 1, 1, D_EXPERT, D), wsel)],
        out_specs=pl.BlockSpec(tile, lambda i, te, n: (i, 0)),
    )
    return pl.pallas_call(
        _experts_kernel,
        grid_spec=grid_spec,
        out_shape=jax.ShapeDtypeStruct((n_rows * ROW_PARTS, LANES), F32),
        compiler_params=_cparams(("arbitrary",)),
    )(tile_e, n_tiles, xs, w1, w3, w2)


def _combine_kernel(dcur_ref, dnxt_ref, ys_hbm, x1_ref, rw_ref, mod_ref, nf_ref, o_ref, ybuf, sem, *, tc, nsteps):
    g = pl.program_id(0)
    slot = g % 2

    def issue(dest_ref, sl, t):
        dst = pl.ds(pl.multiple_of(t * ROW_PARTS, ROW_PARTS), ROW_PARTS)
        for s in range(2):
            src = pl.multiple_of(dest_ref[2 * t + s] * ROW_PARTS, ROW_PARTS)
            pltpu.make_async_copy(ys_hbm.at[pl.ds(src, ROW_PARTS)], ybuf.at[sl, s, dst],
                                  sem.at[sl]).start(priority=s)

    def drain(sl):
        for s in range(2):
            pltpu.make_async_copy(ys_hbm.at[pl.ds(0, tc * ROW_PARTS)], ybuf.at[sl, s], sem.at[sl]).wait()

    @pl.when(g == 0)
    def _():
        lax.fori_loop(0, tc, lambda t, c: (issue(dcur_ref, 0, t), c)[1], 0, unroll=DMA_UNROLL)

    drain(slot)
    for t in range(tc):
        issue(dnxt_ref, 1 - slot, t)

    rw = jnp.concatenate([rw_ref[0], jnp.zeros((LANES - SUBLANES, tc), F32)], axis=0).T
    y = rw[:, 0:1] * _load_rows(ybuf, (slot, 0), tc) + rw[:, 1:2] * _load_rows(ybuf, (slot, 1), tc)
    xo = x1_ref[0] + mod_ref[0, 5:6, :] * y
    o_ref[0] = xo * lax.rsqrt(jnp.mean(xo * xo, axis=-1, keepdims=True) + EPS) * nf_ref[...]

    @pl.when(g == nsteps - 1)
    def _():
        drain(1 - slot)


def _combine(dest_flat, ys, x1, rw, mod, norm_f, tc):
    B, S, D = x1.shape
    n = S // tc
    nsteps = B * n
    tok = lambda g: (g // n, g % n, 0)
    return pl.pallas_call(
        functools.partial(_combine_kernel, tc=tc, nsteps=nsteps),
        grid=(nsteps,),
        in_specs=[pl.BlockSpec((2 * tc,), lambda g: (g,), memory_space=pltpu.SMEM),
                  pl.BlockSpec((2 * tc,), lambda g: (jnp.minimum(g + 1, nsteps - 1),), memory_space=pltpu.SMEM),
                  pl.BlockSpec(memory_space=pl.ANY),
                  pl.BlockSpec((1, tc, D), tok),
                  pl.BlockSpec((1, SUBLANES, tc), lambda g: (g // n, 0, g % n)),
                  pl.BlockSpec((1, 6, D), lambda g: (g // n, 0, 0)),
                  pl.BlockSpec((1, D), lambda g: (0, 0))],
        out_specs=pl.BlockSpec((1, tc, D), tok),
        out_shape=jax.ShapeDtypeStruct((B, S, D), F32),
        scratch_shapes=[pltpu.VMEM((2, 2, tc * ROW_PARTS, LANES), F32), pltpu.SemaphoreType.DMA((2,))],
        compiler_params=_cparams(("arbitrary",)),
    )(dest_flat, dest_flat, ys, x1, rw, mod, norm_f.reshape(1, D))


def _pick(n, pref):
    t = min(n, pref)
    assert n % t == 0, (n, pref)
    return t


def kernel(x, c, w_ada, b_ada, norm1, w_in, b_if, conv_w, conv_b, lam_q1, lam_k1, lam_q2, lam_k2,
           diff_norm, mlstm_norm, w_br_a, w_br_m, w_out, norm2, w_rg, b_rg, w_re, b_re,
           w_e1, w_e3, w_e2, norm_f):
    B, S, D = x.shape
    assert D == D_MODEL and w_ada.shape[0] == 1
    T = B * S
    l = 0

    mod = _ada(c, w_ada, b_ada).reshape(B, 6, D)

    w = w_in[l]
    o_q, o_k, o_v, o_qk, o_mv, o_mo, o_if, o_ga, o_gm = np.cumsum((0,) + (
        DA_WIDTH, DA_WIDTH, DA_WIDTH, 2 * ML_WIDTH, ML_WIDTH, ML_WIDTH, 2 * ML_HEADS, D_MODEL))
    w_perm = jnp.concatenate([
        w[:, o_qk:o_qk + 2 * ML_WIDTH], w[:, o_ga:o_ga + D], w[:, o_gm:o_gm + D],
        w[:, o_q:o_q + DA_WIDTH] * (DA_HD ** -0.5), w[:, o_k:o_k + DA_WIDTH], w[:, o_v:o_v + DA_WIDTH],
        w[:, o_mv:o_mv + ML_WIDTH], w[:, o_mo:o_mo + ML_WIDTH], w[:, o_if:o_if + 2 * ML_HEADS],
        jnp.zeros((D, LANES - 2 * ML_HEADS), F32)], axis=1).astype(BF16)
    bif = jnp.concatenate([b_if[l], jnp.zeros((LANES - 2 * ML_HEADS,), F32)]).reshape(1, LANES)

    p, gates = _proj(x, mod, norm1[l], bif, w_perm, _pick(S, PROJ_TM))

    lamv = jnp.stack([lam_q1[l], lam_k1[l], lam_q2[l], lam_k2[l]])
    ya = _attn(p, lamv, diff_norm[l], _pick(S, ATTN_TQ), _pick(S, ATTN_TK))
    ym = _mlstm(p, gates, conv_w[l], conv_b[l], mlstm_norm[l], _pick(S, MLSTM_L))

    pad_rows = ROUTE_ROWS - N_GROUPS - N_EXPERTS
    wr = jnp.concatenate([w_rg[l], w_re[l], jnp.zeros((D, pad_rows), F32)], axis=1).T
    wr_hi = wr.astype(BF16)
    wr_lo = (wr - wr_hi.astype(F32)).astype(BF16)
    wr = jnp.concatenate([wr_hi, wr_hi, wr_lo], axis=1)
    br = jnp.concatenate([b_rg[l], b_re[l], jnp.zeros((pad_rows,), F32)]).reshape(ROUTE_ROWS, 1)
    x1, h2, ri, rw, cnt = _post(ya, ym, p, x, mod, w_br_a[l].astype(BF16), w_br_m[l].astype(BF16),
                                w_out[l].astype(BF16), norm2[l], wr, br, _pick(S, POST_TM))

    counts = cnt[N_GROUPS:N_GROUPS + N_EXPERTS, 0].astype(jnp.int32)
    pc = ((counts + EXPERT_ROWS - 1) // EXPERT_ROWS) * EXPERT_ROWS
    ends = jnp.cumsum(pc)
    offs = ends - pc
    eid = ri[:, 0:2, :]
    rank = ri[:, 2:4, :]
    dest = rank + jnp.sum(jnp.where(eid[..., None] == jnp.arange(N_EXPERTS), offs, 0), axis=-1)
    dest_flat = dest.transpose(0, 2, 1).reshape(2 * T).astype(jnp.int32)
    n_rows = 2 * T + N_EXPERTS * EXPERT_ROWS
    n_tiles = n_rows // EXPERT_ROWS
    tile_e = jnp.sum(jnp.arange(n_tiles)[:, None] * EXPERT_ROWS >= ends[None, :], axis=1)
    tile_e = jnp.minimum(tile_e, N_EXPERTS - 1).astype(jnp.int32)
    used_tiles = (ends[-1:] // EXPERT_ROWS).astype(jnp.int32)

    xs = _dispatch(ends.astype(jnp.int32), pc.astype(jnp.int32), dest_flat, h2.reshape(T * ROW_PARTS, LANES), n_rows,
                   _pick(T, SMEM_BLOCK_1D))
    ys = _experts(tile_e, used_tiles, xs, w_e1, w_e3, w_e2)
    return _combine(dest_flat, ys, x1, rw, mod, norm_f, _pick(S, SMEM_BLOCK_1D // 2))
```

```python
import functools
import math

import jax
import jax.numpy as jnp
import numpy as np
from jax import lax
from jax.experimental import pallas as pl
from jax.experimental.pallas import tpu as pltpu

F32 = jnp.float32
BF16 = jnp.bfloat16

D_MODEL = 1024
DA_HEADS = 4
DA_HD = 64
DA_WIDTH = DA_HEADS * 2 * DA_HD
ML_HEADS = 4
ML_HD = 128
ML_WIDTH = ML_HEADS * ML_HD
CONV_W = 4
N_GROUPS = 4
EXPERTS_PER_GROUP = 8
N_EXPERTS = N_GROUPS * EXPERTS_PER_GROUP
D_EXPERT = 256
EPS = 1e-6
LAMBDA_INIT = 0.8 - 0.6 * math.exp(-0.3 * 0)

LANES = 128
SUBLANES = 8
SMEM_BLOCK_1D = 1024
NEG = -1e30
VMEM_LIMIT = 56 * 1024 * 1024

COL_MLQK = 0
COL_GA = 1024
COL_GM = 2048
COL_DAQ = 3072
COL_DAK = 3584
COL_DAV = 4096
COL_MLV = 4608
COL_MLO = 5120
COL_IF = 5632
N_PROJ = COL_IF + LANES
PROJ_TN = 1920
assert N_PROJ % PROJ_TN == 0

EXPERT_ROWS = 512
PROJ_TM = 1024
ATTN_TQ = 1024
ATTN_TK = 512
MLSTM_L = 256
CONV_TAIL = 16
POST_TM = 512
ROUTE_ROWS = 48


def _cparams(sem):
    return pltpu.CompilerParams(dimension_semantics=sem, vmem_limit_bytes=VMEM_LIMIT)


ROW_PARTS = D_MODEL // LANES
assert ROW_PARTS == SUBLANES


def _store_rows(ref, lead, x):
    n = x.shape[0]
    for a in range(ROW_PARTS):
        ref[lead + (pl.ds(a, n, stride=ROW_PARTS), slice(None))] = x[:, a * LANES:(a + 1) * LANES]


def _load_rows(ref, lead, n):
    return jnp.concatenate([ref[lead + (pl.ds(a, n, stride=ROW_PARTS), slice(None))] for a in range(ROW_PARTS)],
                           axis=1)


def _sigmoid(x):
    return 0.5 * jnp.tanh(0.5 * x) + 0.5


def _ada_kernel(c_ref, w_ref, b_ref, o_ref):
    c = c_ref[...]
    cs = c * _sigmoid(c)
    o_ref[...] = jnp.dot(cs, w_ref[0], precision=lax.Precision.HIGHEST,
                         preferred_element_type=F32) + b_ref[...]


def _ada(c, w, b):
    B, D = c.shape
    N = w.shape[2]
    tn = 1536
    return pl.pallas_call(
        _ada_kernel,
        grid=(N // tn,),
        in_specs=[pl.BlockSpec((B, D), lambda j: (0, 0)),
                  pl.BlockSpec((1, D, tn), lambda j: (0, 0, j)),
                  pl.BlockSpec((1, tn), lambda j: (0, j))],
        out_specs=pl.BlockSpec((B, tn), lambda j: (0, j)),
        out_shape=jax.ShapeDtypeStruct((B, N), F32),
        compiler_params=_cparams(("arbitrary",)),
    )(c, w, b)


def _w_in_kernel(w_ref, o_ref):
    w = w_ref[0]
    o_q, o_k, o_v, o_qk, o_mv, o_mo, o_if, o_ga, o_gm = np.cumsum((0,) + (
        DA_WIDTH, DA_WIDTH, DA_WIDTH, 2 * ML_WIDTH, ML_WIDTH, ML_WIDTH, 2 * ML_HEADS, D_MODEL))
    rows = w.shape[0]
    o_ref[...] = jnp.concatenate([
        w[:, o_qk:o_qk + 2 * ML_WIDTH], w[:, o_ga:o_ga + D_MODEL], w[:, o_gm:o_gm + D_MODEL],
        w[:, o_q:o_q + DA_WIDTH] * (DA_HD ** -0.5), w[:, o_k:o_k + DA_WIDTH], w[:, o_v:o_v + DA_WIDTH],
        w[:, o_mv:o_mv + ML_WIDTH], w[:, o_mo:o_mo + ML_WIDTH], w[:, o_if:o_if + 2 * ML_HEADS],
        jnp.zeros((rows, LANES - 2 * ML_HEADS), F32)], axis=1).astype(BF16)


def _permute_w_in(w_in):
    _, D, n_in = w_in.shape
    tr = 128
    return pl.pallas_call(
        _w_in_kernel,
        grid=(D // tr,),
        in_specs=[pl.BlockSpec((1, tr, n_in), lambda i: (0, i, 0))],
        out_specs=pl.BlockSpec((tr, N_PROJ), lambda i: (i, 0)),
        out_shape=jax.ShapeDtypeStruct((D, N_PROJ), BF16),
        compiler_params=_cparams(("arbitrary",)),
    )(w_in)


def _proj_kernel(x_ref, mod_ref, g_ref, bif_ref, w_ref, p_ref, gate_ref, h_scr, *, nj):
    j = pl.program_id(2)

    @pl.when(j == 0)
    def _():
        x = x_ref[0]
        y = x * lax.rsqrt(jnp.mean(x * x, axis=-1, keepdims=True) + EPS) * g_ref[...]
        h = y * (1.0 + mod_ref[0, 1:2, :]) + mod_ref[0, 0:1, :]
        h_scr[...] = h.astype(BF16)

    acc = jnp.dot(h_scr[...], w_ref[...], preferred_element_type=F32)
    p_ref[0] = acc.astype(BF16)

    @pl.when(j == nj - 1)
    def _():
        gate_ref[0] = acc[:, PROJ_TN - LANES:] + bif_ref[...]


def _proj(x, mod, norm1, bif, w_bf16, tm):
    B, S, D = x.shape
    nj = N_PROJ // PROJ_TN
    return pl.pallas_call(
        functools.partial(_proj_kernel, nj=nj),
        grid=(B, S // tm, nj),
        in_specs=[pl.BlockSpec((1, tm, D), lambda b, i, j: (b, i, 0)),
                  pl.BlockSpec((1, 6, D), lambda b, i, j: (b, 0, 0)),
                  pl.BlockSpec((1, D), lambda b, i, j: (0, 0)),
                  pl.BlockSpec((1, LANES), lambda b, i, j: (0, 0)),
                  pl.BlockSpec((D, PROJ_TN), lambda b, i, j: (0, j))],
        out_specs=[pl.BlockSpec((1, tm, PROJ_TN), lambda b, i, j: (b, i, j)),
                   pl.BlockSpec((1, tm, LANES), lambda b, i, j: (b, i, 0))],
        out_shape=[jax.ShapeDtypeStruct((B, S, N_PROJ), BF16),
                   jax.ShapeDtypeStruct((B, S, LANES), F32)],
        scratch_shapes=[pltpu.VMEM((tm, D), BF16)],
        compiler_params=_cparams(("arbitrary", "arbitrary", "arbitrary")),
    )(x, mod, norm1.reshape(1, D), bif, w_bf16)


ATTN_SUM_ROWS = 16

ATTN_HEADS = 2


def _attn_block(q_ref, k_ref, v_ref, eq_ref, ek_ref, m_scr, acc_scr, off, cbs, masked, q0, tq, tk):
    qs = slice(q0, tq)
    lane = lax.broadcasted_iota(jnp.int32, (1, LANES), 1)
    if masked:
        key = lax.broadcasted_iota(jnp.int32, (tk, tq - q0), 0)
        qry = lax.broadcasted_iota(jnp.int32, (tk, tq - q0), 1) + q0
        valid = (key - qry) <= off
    scores = []
    for u in range(ATTN_HEADS):
        hl = slice(u * LANES, (u + 1) * LANES)
        q = q_ref[0, qs, hl]
        k = k_ref[0, :, hl]
        for c in range(2):
            sel = (lane < DA_HD) if c == 0 else (lane >= DA_HD)
            s = lax.dot_general(jnp.where(sel, k, ek_ref[u]), jnp.where(sel, q, eq_ref[u, qs]),
                                (((1,), (1,)), ((), ())), preferred_element_type=F32)
            scores.append(jnp.where(valid, s, NEG) if masked else s)
    for u in range(ATTN_HEADS):
        vt = jnp.concatenate([v_ref[0, :, u * LANES:(u + 1) * LANES].T,
                              jnp.ones((ATTN_SUM_ROWS, tk), BF16)], axis=0)
        for c in range(2):
            n = 2 * u + c
            s = scores[n]
            m_prev = m_scr[n, :, qs]
            m_new = jnp.maximum(m_prev, jnp.max(s, axis=0, keepdims=True) - cbs[u])
            alpha = jnp.exp(m_prev - m_new)
            p = jnp.exp((s - (m_new + cbs[u])).astype(BF16))
            acc_scr[n, :, qs] = alpha * acc_scr[n, :, qs] + jnp.dot(vt, p, preferred_element_type=F32)
            m_scr[n, :, qs] = m_new


def _attn_kernel(it_ref, jt_ref, slope_ref, q_ref, k_ref, v_ref, eq_ref, ek_ref, lam_ref, dn_ref, o_ref,
                 m_scr, acc_scr, *, tq, tk):
    hp = pl.program_id(1)
    step = pl.program_id(2)
    i = it_ref[step]
    j = jt_ref[step]
    ratio = tq // tk

    @pl.when(j == 0)
    def _():
        m_scr[...] = jnp.full(m_scr.shape, NEG, F32)
        acc_scr[...] = jnp.zeros(acc_scr.shape, F32)

    off = i * tq - j * tk
    cbs = [slope_ref[hp * ATTN_HEADS + u] * off.astype(F32) for u in range(ATTN_HEADS)]
    args = (q_ref, k_ref, v_ref, eq_ref, ek_ref, m_scr, acc_scr, off, cbs)

    @pl.when(j < i * ratio)
    def _():
        _attn_block(*args, masked=False, q0=0, tq=tq, tk=tk)

    for r in range(ratio):
        @pl.when(j == i * ratio + r)
        def _():
            _attn_block(*args, masked=True, q0=r * tk, tq=tq, tk=tk)

    @pl.when(j == (i + 1) * ratio - 1)
    def _():
        lv = lam_ref[...]
        lam = (jnp.exp(jnp.sum(lv[0:1] * lv[1:2], axis=-1, keepdims=True))
               - jnp.exp(jnp.sum(lv[2:3] * lv[3:4], axis=-1, keepdims=True)) + LAMBDA_INIT)
        vd = 2 * DA_HD
        for u in range(ATTN_HEADS):
            a0, a1 = acc_scr[2 * u], acc_scr[2 * u + 1]
            o = a0[0:vd] / a0[vd:vd + 1] - lam * (a1[0:vd] / a1[vd:vd + 1])
            o = o * lax.rsqrt(jnp.mean(o * o, axis=0, keepdims=True) + EPS)
            o_ref[0, :, u * LANES:(u + 1) * LANES] = (o.T * dn_ref[...] * (1.0 - LAMBDA_INIT)).astype(BF16)


def _alibi_columns(n, slopes, q_side):
    assert n <= 256 * 256 and all(math.log2(s).is_integer() for s in slopes)
    pos = np.arange(n)
    lo, hi = (pos % 256).astype(np.float64), (pos // 256 * 256).astype(np.float64)
    out = np.zeros((len(slopes), n, LANES), np.float64)
    for h, s in enumerate(slopes):
        cols = (-s * lo, -s * hi, np.ones(n), np.ones(n)) if q_side else (np.ones(n), np.ones(n), s * lo, s * hi)
        for base in (0, DA_HD):
            for c, v in enumerate(cols):
                out[h, :, base + c] = v
    return jnp.asarray(out, dtype=BF16)


def _attn(p, lamv, diff_norm, tq, tk):
    B, S, _ = p.shape
    nq, ratio = S // tq, tq // tk
    slopes = [2.0 ** (-8.0 * (h + 1) / DA_HEADS) for h in range(DA_HEADS)]
    steps = [(i, j) for i in range(nq) for j in range((i + 1) * ratio)]
    it = jnp.asarray([s[0] for s in steps], jnp.int32)
    jt = jnp.asarray([s[1] for s in steps], jnp.int32)
    hw = ATTN_HEADS * LANES
    qb, kb, vb = COL_DAQ // hw, COL_DAK // hw, COL_DAV // hw
    grid_spec = pltpu.PrefetchScalarGridSpec(
        num_scalar_prefetch=3,
        grid=(B, DA_HEADS // ATTN_HEADS, len(steps)),
        in_specs=[pl.BlockSpec((1, tq, hw), lambda b, h, s, it, jt, sl: (b, it[s], qb + h)),
                  pl.BlockSpec((1, tk, hw), lambda b, h, s, it, jt, sl: (b, jt[s], kb + h)),
                  pl.BlockSpec((1, tk, hw), lambda b, h, s, it, jt, sl: (b, jt[s], vb + h)),
                  pl.BlockSpec((ATTN_HEADS, tq, LANES), lambda b, h, s, it, jt, sl: (h, 0, 0)),
                  pl.BlockSpec((ATTN_HEADS, tk, LANES), lambda b, h, s, it, jt, sl: (h, 0, 0)),
                  pl.BlockSpec((4, DA_HD), lambda b, h, s, it, jt, sl: (0, 0)),
                  pl.BlockSpec((1, 2 * DA_HD), lambda b, h, s, it, jt, sl: (0, 0))],
        out_specs=pl.BlockSpec((1, tq, hw), lambda b, h, s, it, jt, sl: (b, it[s], h)),
        scratch_shapes=[pltpu.VMEM((2 * ATTN_HEADS, 1, tq), F32),
                        pltpu.VMEM((2 * ATTN_HEADS, 2 * DA_HD + ATTN_SUM_ROWS, tq), F32)],
    )
    return pl.pallas_call(
        functools.partial(_attn_kernel, tq=tq, tk=tk),
        grid_spec=grid_spec,
        out_shape=jax.ShapeDtypeStruct((B, S, DA_WIDTH), BF16),
        compiler_params=_cparams(("arbitrary",) * 3),
    )(it, jt, jnp.asarray(slopes, F32), p, p, p, _alibi_columns(tq, slopes, True),
      _alibi_columns(tk, slopes, False), lamv, diff_norm.reshape(1, 2 * DA_HD))


def _mlstm_kernel(qk_ref, v_ref, o_ref, g_ref, cw_ref, cb_ref, nw_ref, y_ref,
                  ext_scr, c_scr, n_scr, m_scr, *, L):
    i = pl.program_id(1)

    @pl.when(i == 0)
    def _():
        ext_scr[...] = jnp.zeros(ext_scr.shape, BF16)
        c_scr[...] = jnp.zeros(c_scr.shape, F32)
        n_scr[...] = jnp.zeros(n_scr.shape, F32)
        m_scr[...] = jnp.zeros(m_scr.shape, F32)

    raw = qk_ref[0]
    tail = ext_scr[...]
    t_out = lax.broadcasted_iota(jnp.int32, (L, L), 0)
    t_in = lax.broadcasted_iota(jnp.int32, (L, L), 1)
    h_out = lax.broadcasted_iota(jnp.int32, (CONV_TAIL, CONV_TAIL), 0)
    h_in = lax.broadcasted_iota(jnp.int32, (CONV_TAIL, CONV_TAIL), 1)
    conv = cb_ref[...] + raw.astype(F32) * cw_ref[CONV_W - 1:CONV_W, :]
    for k in range(1, CONV_W):
        shift = jnp.where(t_in == t_out - k, 1.0, 0.0).astype(BF16)
        head = jnp.where(h_in == h_out - k + CONV_TAIL, 1.0, 0.0).astype(BF16)
        xk = jnp.dot(shift, raw, preferred_element_type=F32)
        fix = jnp.dot(head, tail, preferred_element_type=F32)
        xk = jnp.concatenate([xk[:CONV_TAIL] + fix, xk[CONV_TAIL:]], axis=0)
        conv = conv + xk * cw_ref[CONV_W - 1 - k:CONV_W - k, :]
    ext_scr[...] = raw[L - CONV_TAIL:L, :]
    qkc = conv * _sigmoid(conv)

    gts = g_ref[0]
    fpre = pltpu.roll(gts, LANES - ML_HEADS, axis=1)
    lf = jnp.minimum(fpre, 0.0) - jnp.log(1.0 + jnp.exp(-jnp.abs(fpre)))
    row = lax.broadcasted_iota(jnp.int32, (L, L), 0)
    col = lax.broadcasted_iota(jnp.int32, (L, L), 1)
    causal = col <= row
    bcum = jnp.dot(causal.astype(F32), lf, precision=lax.Precision.HIGHEST, preferred_element_type=F32)
    r = gts - bcum
    rt = r.T
    m_all = m_scr[...]
    lane = lax.broadcasted_iota(jnp.int32, (1, LANES), 1)
    m_next = m_all
    v_all = v_ref[0]
    o_all = o_ref[0]
    for h in range(ML_HEADS):
        hs = slice(h * ML_HD, (h + 1) * ML_HD)
        bcol = bcum[:, h:h + 1]
        rcol = r[:, h:h + 1]
        rrow = rt[h:h + 1, :]
        g = bcum[L - 1:L, h:h + 1]
        mh = m_all[:, h:h + 1]
        dm = jnp.where(causal, bcol + rrow, NEG)
        inter = bcol + mh
        mj = jnp.maximum(inter, jnp.max(dm, axis=-1, keepdims=True))
        w_intra = jnp.exp(dm - mj)
        w_inter = jnp.exp(inter - mj)
        qh = qkc[:, hs]
        kh = qkc[:, ML_WIDTH + h * ML_HD:ML_WIDTH + (h + 1) * ML_HD] * (ML_HD ** -0.5)
        vh = v_all[:, hs]
        qb = qh.astype(BF16)
        kb = kh.astype(BF16)
        s = lax.dot_general(qb, kb, (((1,), (1,)), ((), ())), preferred_element_type=F32) * w_intra
        c_old = c_scr[h]
        n_old = n_scr[h:h + 1, :]
        num = (jnp.dot(s.astype(BF16), vh, preferred_element_type=F32)
               + lax.dot_general(qb, c_old.astype(BF16), (((1,), (1,)), ((), ())),
                                 preferred_element_type=F32) * w_inter)
        den = (jnp.sum(s, axis=-1, keepdims=True)
               + w_inter * jnp.sum(qh * n_old, axis=-1, keepdims=True))
        denom = jnp.maximum(jnp.abs(den), jnp.exp(-mj))
        ht = num / denom
        a_col = g + rcol
        m_new = jnp.maximum(g + mh, jnp.max(a_col, axis=0, keepdims=True))
        wa = jnp.exp(a_col - m_new)
        decay = jnp.exp(g + mh - m_new)
        vw_t = (vh.astype(F32) * wa).T.astype(BF16)
        c_scr[h] = decay * c_old + jnp.dot(vw_t, kb, preferred_element_type=F32)
        n_scr[h:h + 1, :] = decay * n_old + jnp.sum(kh * wa, axis=0, keepdims=True)
        m_next = jnp.where(lane == h, m_new, m_next)
        z = _sigmoid(o_all[:, hs].astype(F32)) * ht
        z = z * lax.rsqrt(jnp.mean(z * z, axis=-1, keepdims=True) + EPS) * nw_ref[...]
        y_ref[0, :, hs] = z.astype(BF16)
    m_scr[...] = m_next


def _mlstm(p, gates, conv_w, conv_b, mlstm_norm, L):
    B, S, _ = p.shape
    return pl.pallas_call(
        functools.partial(_mlstm_kernel, L=L),
        grid=(B, S // L),
        in_specs=[pl.BlockSpec((1, L, 2 * ML_WIDTH), lambda b, i: (b, i, COL_MLQK // (2 * ML_WIDTH))),
                  pl.BlockSpec((1, L, ML_WIDTH), lambda b, i: (b, i, COL_MLV // ML_WIDTH)),
                  pl.BlockSpec((1, L, ML_WIDTH), lambda b, i: (b, i, COL_MLO // ML_WIDTH)),
                  pl.BlockSpec((1, L, LANES), lambda b, i: (b, i, 0)),
                  pl.BlockSpec((CONV_W, 2 * ML_WIDTH), lambda b, i: (0, 0)),
                  pl.BlockSpec((1, 2 * ML_WIDTH), lambda b, i: (0, 0)),
                  pl.BlockSpec((1, ML_HD), lambda b, i: (0, 0))],
        out_specs=pl.BlockSpec((1, L, ML_WIDTH), lambda b, i: (b, i, 0)),
        out_shape=jax.ShapeDtypeStruct((B, S, ML_WIDTH), BF16),
        scratch_shapes=[pltpu.VMEM((CONV_TAIL, 2 * ML_WIDTH), BF16),
                        pltpu.VMEM((ML_HEADS, ML_HD, ML_HD), F32),
                        pltpu.VMEM((SUBLANES, ML_HD), F32),
                        pltpu.VMEM((1, LANES), F32)],
        compiler_params=_cparams(("arbitrary", "arbitrary")),
    )(p, p, p, gates, conv_w, conv_b.reshape(1, -1), mlstm_norm.reshape(1, ML_HD))


def _post_kernel(ya_ref, ym_ref, ga_ref, gm_ref, x_ref, mod_ref, wa_ref, wm_ref, wo_ref, n2_ref,
                 wr_ref, br_ref, x1_ref, h2_ref, ri_ref, rw_ref, cnt_ref, run_scr, *, tm):
    first = jnp.logical_and(pl.program_id(0) == 0, pl.program_id(1) == 0)

    @pl.when(first)
    def _():
        run_scr[...] = jnp.zeros(run_scr.shape, F32)

    a = jnp.dot(ya_ref[0], wa_ref[...], preferred_element_type=F32)
    m = jnp.dot(ym_ref[0], wm_ref[...], preferred_element_type=F32)
    merged = _sigmoid(ga_ref[0]).astype(F32) * a + _sigmoid(gm_ref[0]).astype(F32) * m
    o = jnp.dot(merged.astype(BF16), wo_ref[...], preferred_element_type=F32)
    x1 = x_ref[0] + mod_ref[0, 2:3, :] * o
    x1_ref[0] = x1
    h2 = x1 * lax.rsqrt(jnp.mean(x1 * x1, axis=-1, keepdims=True) + EPS) * n2_ref[...]
    h2 = h2 * (1.0 + mod_ref[0, 4:5, :]) + mod_ref[0, 3:4, :]
    _store_rows(h2_ref, (0,), h2)

    hi = h2.astype(BF16)
    lo = (h2 - hi.astype(F32)).astype(BF16)
    logits = lax.dot_general(wr_ref[...], jnp.concatenate([hi, lo, hi], axis=1), (((1,), (1,)), ((), ())),
                             preferred_element_type=F32) + br_ref[...]
    sub = lax.broadcasted_iota(jnp.int32, (ROUTE_ROWS, tm), 0)
    big = jnp.int32(4 * ROUTE_ROWS)
    gl = jnp.where(sub < N_GROUPS, logits, NEG)
    gmax = jnp.max(gl, axis=0, keepdims=True)
    gsel = jnp.min(jnp.where(gl == gmax, sub, big), axis=0, keepdims=True)
    pgrp = 1.0 / jnp.sum(jnp.exp(gl - gmax), axis=0, keepdims=True)
    first = N_GROUPS + EXPERTS_PER_GROUP * gsel
    el = jnp.where(jnp.logical_and(sub >= first, sub < first + EXPERTS_PER_GROUP), logits, NEG)
    e1 = jnp.max(el, axis=0, keepdims=True)
    i1 = jnp.min(jnp.where(el == e1, sub, big), axis=0, keepdims=True)
    el2 = jnp.where(sub == i1, NEG, el)
    e2 = jnp.max(el2, axis=0, keepdims=True)
    i2 = jnp.min(jnp.where(el2 == e2, sub, big), axis=0, keepdims=True)
    tt = jnp.exp(e2 - e1)
    w1 = pgrp / (1.0 + tt)
    w2 = pgrp * tt / (1.0 + tt)

    oh1 = jnp.where(sub == i1, 1.0, 0.0).astype(F32)
    oh2 = jnp.where(sub == i2, 1.0, 0.0).astype(F32)
    cat = jnp.concatenate([oh1, oh2], axis=0).astype(BF16)
    src = lax.broadcasted_iota(jnp.int32, (tm, tm), 0)
    dst = lax.broadcasted_iota(jnp.int32, (tm, tm), 1)
    before = jnp.where(src < dst, 1.0, 0.0).astype(BF16)
    earlier = jnp.dot(cat, before, preferred_element_type=F32)
    c1 = jnp.sum(oh1, axis=1, keepdims=True)
    c2 = jnp.sum(oh2, axis=1, keepdims=True)
    run = run_scr[...]
    rank1 = jnp.sum((earlier[:ROUTE_ROWS] + run) * oh1, axis=0, keepdims=True)
    rank2 = jnp.sum((earlier[ROUTE_ROWS:] + run + c1) * oh2, axis=0, keepdims=True)
    run_new = run + c1 + c2
    run_scr[...] = run_new
    cnt_ref[...] = jnp.broadcast_to(run_new, cnt_ref.shape)
    row = lax.broadcasted_iota(jnp.int32, (SUBLANES, tm), 0)
    ri_ref[0] = jnp.where(row == 0, i1 - N_GROUPS,
                          jnp.where(row == 1, i2 - N_GROUPS,
                                    jnp.where(row == 2, rank1.astype(jnp.int32),
                                              jnp.where(row == 3, rank2.astype(jnp.int32), 0))))
    rw_ref[0] = jnp.where(row == 0, w1, jnp.where(row == 1, w2, 0.0))


def _post(ya, ym, p, x, mod, wa, wm, wo, norm2, wr, br, tm):
    B, S, D = x.shape
    tok = lambda b, i: (b, i, 0)
    const = lambda b, i: (0, 0)
    return pl.pallas_call(
        functools.partial(_post_kernel, tm=tm),
        grid=(B, S // tm),
        in_specs=[pl.BlockSpec((1, tm, DA_WIDTH), tok),
                  pl.BlockSpec((1, tm, ML_WIDTH), tok),
                  pl.BlockSpec((1, tm, D), lambda b, i: (b, i, COL_GA // D_MODEL)),
                  pl.BlockSpec((1, tm, D), lambda b, i: (b, i, COL_GM // D_MODEL)),
                  pl.BlockSpec((1, tm, D), tok),
                  pl.BlockSpec((1, 6, D), lambda b, i: (b, 0, 0)),
                  pl.BlockSpec((DA_WIDTH, D), const),
                  pl.BlockSpec((ML_WIDTH, D), const),
                  pl.BlockSpec((D, D), const),
                  pl.BlockSpec((1, D), const),
                  pl.BlockSpec((ROUTE_ROWS, 3 * D), const),
                  pl.BlockSpec((ROUTE_ROWS, 1), const)],
        out_specs=[pl.BlockSpec((1, tm, D), tok),
                   pl.BlockSpec((1, tm * ROW_PARTS, LANES), tok),
                   pl.BlockSpec((1, SUBLANES, tm), lambda b, i: (b, 0, i)),
                   pl.BlockSpec((1, SUBLANES, tm), lambda b, i: (b, 0, i)),
                   pl.BlockSpec((ROUTE_ROWS, LANES), const)],
        out_shape=[jax.ShapeDtypeStruct((B, S, D), F32),
                   jax.ShapeDtypeStruct((B, S * ROW_PARTS, LANES), F32),
                   jax.ShapeDtypeStruct((B, SUBLANES, S), jnp.int32),
                   jax.ShapeDtypeStruct((B, SUBLANES, S), F32),
                   jax.ShapeDtypeStruct((ROUTE_ROWS, LANES), F32)],
        scratch_shapes=[pltpu.VMEM((ROUTE_ROWS, 1), F32)],
        compiler_params=_cparams(("arbitrary", "arbitrary")),
    )(ya, ym, p, p, x, mod, wa, wm, wo, norm2.reshape(1, D), wr, br)


DMA_UNROLL = 8


def _dispatch_kernel(ends_ref, pc_ref, dest_ref, h2_ref, xs_hbm, zbuf, sem, zsem, *, td):
    i = pl.program_id(0)
    tile_rows = EXPERT_ROWS * ROW_PARTS

    def zero_tile(first_row):
        start = pl.multiple_of(first_row * ROW_PARTS, tile_rows)
        cp = pltpu.make_async_copy(zbuf, xs_hbm.at[pl.ds(start, tile_rows)], zsem)
        cp.start()
        cp.wait()

    @pl.when(i == 0)
    def _():
        zbuf[...] = jnp.zeros(zbuf.shape, zbuf.dtype)
        for e in range(N_EXPERTS):
            @pl.when(pc_ref[e] > 0)
            def _():
                zero_tile(ends_ref[e] - EXPERT_ROWS)

        def fill(tile, carry):
            zero_tile(tile * EXPERT_ROWS)
            return carry

        lax.fori_loop(ends_ref[N_EXPERTS - 1] // EXPERT_ROWS, xs_hbm.shape[0] // tile_rows, fill, 0)

    def issue(t, carry):
        src = h2_ref.at[pl.ds(pl.multiple_of(t * ROW_PARTS, ROW_PARTS), ROW_PARTS)]
        for s in range(2):
            dst = pl.multiple_of(dest_ref[2 * t + s] * ROW_PARTS, ROW_PARTS)
            pltpu.make_async_copy(src, xs_hbm.at[pl.ds(dst, ROW_PARTS)], sem).start(priority=s)
        return carry

    lax.fori_loop(0, td, issue, 0, unroll=DMA_UNROLL)
    for s in range(2):
        pltpu.make_async_copy(h2_ref, xs_hbm.at[pl.ds(0, td * ROW_PARTS)], sem).wait()


def _dispatch(ends, pc, dest_flat, h2, n_rows, td):
    T = h2.shape[0] // ROW_PARTS
    grid_spec = pltpu.PrefetchScalarGridSpec(
        num_scalar_prefetch=2,
        grid=(T // td,),
        in_specs=[pl.BlockSpec((2 * td,), lambda i, e, c: (i,), memory_space=pltpu.SMEM),
                  pl.BlockSpec((td * ROW_PARTS, LANES), lambda i, e, c: (i, 0))],
        out_specs=pl.BlockSpec(memory_space=pl.ANY),
        scratch_shapes=[pltpu.VMEM((EXPERT_ROWS * ROW_PARTS, LANES), h2.dtype),
                        pltpu.SemaphoreType.DMA(()), pltpu.SemaphoreType.DMA(())],
    )
    return pl.pallas_call(
        functools.partial(_dispatch_kernel, td=td),
        grid_spec=grid_spec,
        out_shape=jax.ShapeDtypeStruct((n_rows * ROW_PARTS, LANES), h2.dtype),
        compiler_params=_cparams(("arbitrary",)),
    )(ends, pc, dest_flat, h2)


def _experts_kernel(te_ref, nt_ref, xs_ref, w1_ref, w3_ref, w2_ref, ys_ref):
    i = pl.program_id(0)

    @pl.when(i < nt_ref[0])
    def _():
        x = _load_rows(xs_ref, (), EXPERT_ROWS).astype(BF16)
        a = jnp.dot(x, w1_ref[0, 0, 0].astype(BF16), preferred_element_type=F32)
        b = jnp.dot(x, w3_ref[0, 0, 0].astype(BF16), preferred_element_type=F32)
        hid = (a * _sigmoid(a) * b).astype(BF16)
        _store_rows(ys_ref, (), jnp.dot(hid, w2_ref[0, 0, 0].astype(BF16), preferred_element_type=F32))

    @pl.when(i >= nt_ref[0])
    def _():
        ys_ref[...] = jnp.zeros(ys_ref.shape, ys_ref.dtype)


def _experts(tile_e, n_tiles, xs, w1, w3, w2):
    n_rows = xs.shape[0] // ROW_PARTS
    D = D_MODEL
    nt = n_rows // EXPERT_ROWS
    tile = (EXPERT_ROWS * ROW_PARTS, LANES)
    rows = lambda i, te, n: (jnp.minimum(i, jnp.maximum(n[0] - 1, 0)), 0)
    wsel = lambda i, te, n: (0, te[i] // EXPERTS_PER_GROUP, te[i] % EXPERTS_PER_GROUP, 0, 0)
    grid_spec = pltpu.PrefetchScalarGridSpec(
        num_scalar_prefetch=2,
        grid=(nt,),
        in_specs=[pl.BlockSpec(tile, rows),
                  pl.BlockSpec((1, 1, 1, D, D_EXPERT), wsel),
                  pl.BlockSpec((1, 1, 1, D, D_EXPERT), wsel),
                  pl.BlockSpec((1, 1, 1, D_EXPERT, D), wsel)],
        out_specs=pl.BlockSpec(tile, lambda i, te, n: (i, 0)),
    )
    return pl.pallas_call(
        _experts_kernel,
        grid_spec=grid_spec,
        out_shape=jax.ShapeDtypeStruct((n_rows * ROW_PARTS, LANES), F32),
        compiler_params=_cparams(("arbitrary",)),
    )(tile_e, n_tiles, xs, w1, w3, w2)


def _combine_kernel(dcur_ref, dnxt_ref, ys_hbm, x1_ref, rw_ref, mod_ref, nf_ref, o_ref, ybuf, sem, *, tc, nsteps):
    g = pl.program_id(0)
    slot = g % 2

    def issue(dest_ref, sl, t):
        dst = pl.ds(pl.multiple_of(t * ROW_PARTS, ROW_PARTS), ROW_PARTS)
        for s in range(2):
            src = pl.multiple_of(dest_ref[2 * t + s] * ROW_PARTS, ROW_PARTS)
            pltpu.make_async_copy(ys_hbm.at[pl.ds(src, ROW_PARTS)], ybuf.at[sl, s, dst],
                                  sem.at[sl]).start(priority=s)

    def drain(sl):
        for s in range(2):
            pltpu.make_async_copy(ys_hbm.at[pl.ds(0, tc * ROW_PARTS)], ybuf.at[sl, s], sem.at[sl]).wait()

    @pl.when(g == 0)
    def _():
        lax.fori_loop(0, tc, lambda t, c: (issue(dcur_ref, 0, t), c)[1], 0, unroll=DMA_UNROLL)

    drain(slot)
    for t in range(tc):
        issue(dnxt_ref, 1 - slot, t)

    rw = jnp.concatenate([rw_ref[0], jnp.zeros((LANES - SUBLANES, tc), F32)], axis=0).T
    y = rw[:, 0:1] * _load_rows(ybuf, (slot, 0), tc) + rw[:, 1:2] * _load_rows(ybuf, (slot, 1), tc)
    xo = x1_ref[0] + mod_ref[0, 5:6, :] * y
    o_ref[0] = xo * lax.rsqrt(jnp.mean(xo * xo, axis=-1, keepdims=True) + EPS) * nf_ref[...]

    @pl.when(g == nsteps - 1)
    def _():
        drain(1 - slot)


def _combine(dest_flat, ys, x1, rw, mod, norm_f, tc):
    B, S, D = x1.shape
    n = S // tc
    nsteps = B * n
    tok = lambda g: (g // n, g % n, 0)
    return pl.pallas_call(
        functools.partial(_combine_kernel, tc=tc, nsteps=nsteps),
        grid=(nsteps,),
        in_specs=[pl.BlockSpec((2 * tc,), lambda g: (g,), memory_space=pltpu.SMEM),
                  pl.BlockSpec((2 * tc,), lambda g: (jnp.minimum(g + 1, nsteps - 1),), memory_space=pltpu.SMEM),
                  pl.BlockSpec(memory_space=pl.ANY),
                  pl.BlockSpec((1, tc, D), tok),
                  pl.BlockSpec((1, SUBLANES, tc), lambda g: (g // n, 0, g % n)),
                  pl.BlockSpec((1, 6, D), lambda g: (g // n, 0, 0)),
                  pl.BlockSpec((1, D), lambda g: (0, 0))],
        out_specs=pl.BlockSpec((1, tc, D), tok),
        out_shape=jax.ShapeDtypeStruct((B, S, D), F32),
        scratch_shapes=[pltpu.VMEM((2, 2, tc * ROW_PARTS, LANES), F32), pltpu.SemaphoreType.DMA((2,))],
        compiler_params=_cparams(("arbitrary",)),
    )(dest_flat, dest_flat, ys, x1, rw, mod, norm_f.reshape(1, D))


def _pick(n, pref):
    t = min(n, pref)
    assert n % t == 0, (n, pref)
    return t


def kernel(x, c, w_ada, b_ada, norm1, w_in, b_if, conv_w, conv_b, lam_q1, lam_k1, lam_q2, lam_k2,
           diff_norm, mlstm_norm, w_br_a, w_br_m, w_out, norm2, w_rg, b_rg, w_re, b_re,
           w_e1, w_e3, w_e2, norm_f):
    B, S, D = x.shape
    assert D == D_MODEL and w_ada.shape[0] == 1
    T = B * S
    l = 0

    mod = _ada(c, w_ada, b_ada).reshape(B, 6, D)

    w_perm = _permute_w_in(w_in)
    bif = jnp.concatenate([b_if[l], jnp.zeros((LANES - 2 * ML_HEADS,), F32)]).reshape(1, LANES)

    p, gates = _proj(x, mod, norm1[l], bif, w_perm, _pick(S, PROJ_TM))

    lamv = jnp.stack([lam_q1[l], lam_k1[l], lam_q2[l], lam_k2[l]])
    ya = _attn(p, lamv, diff_norm[l], _pick(S, ATTN_TQ), _pick(S, ATTN_TK))
    ym = _mlstm(p, gates, conv_w[l], conv_b[l], mlstm_norm[l], _pick(S, MLSTM_L))

    pad_rows = ROUTE_ROWS - N_GROUPS - N_EXPERTS
    wr = jnp.concatenate([w_rg[l], w_re[l], jnp.zeros((D, pad_rows), F32)], axis=1).T
    wr_hi = wr.astype(BF16)
    wr_lo = (wr - wr_hi.astype(F32)).astype(BF16)
    wr = jnp.concatenate([wr_hi, wr_hi, wr_lo], axis=1)
    br = jnp.concatenate([b_rg[l], b_re[l], jnp.zeros((pad_rows,), F32)]).reshape(ROUTE_ROWS, 1)
    x1, h2, ri, rw, cnt = _post(ya, ym, p, x, mod, w_br_a[l].astype(BF16), w_br_m[l].astype(BF16),
                                w_out[l].astype(BF16), norm2[l], wr, br, _pick(S, POST_TM))

    counts = cnt[N_GROUPS:N_GROUPS + N_EXPERTS, 0].astype(jnp.int32)
    pc = ((counts + EXPERT_ROWS - 1) // EXPERT_ROWS) * EXPERT_ROWS
    ends = jnp.cumsum(pc)
    offs = ends - pc
    eid = ri[:, 0:2, :]
    rank = ri[:, 2:4, :]
    dest = rank + jnp.sum(jnp.where(eid[..., None] == jnp.arange(N_EXPERTS), offs, 0), axis=-1)
    dest_flat = dest.transpose(0, 2, 1).reshape(2 * T).astype(jnp.int32)
    n_rows = 2 * T + N_EXPERTS * EXPERT_ROWS
    n_tiles = n_rows // EXPERT_ROWS
    tile_e = jnp.sum(jnp.arange(n_tiles)[:, None] * EXPERT_ROWS >= ends[None, :], axis=1)
    tile_e = jnp.minimum(tile_e, N_EXPERTS - 1).astype(jnp.int32)
    used_tiles = (ends[-1:] // EXPERT_ROWS).astype(jnp.int32)

    xs = _dispatch(ends.astype(jnp.int32), pc.astype(jnp.int32), dest_flat, h2.reshape(T * ROW_PARTS, LANES), n_rows,
                   _pick(T, SMEM_BLOCK_1D))
    ys = _experts(tile_e, used_tiles, xs, w_e1, w_e3, w_e2)
    return _combine(dest_flat, ys, x1, rw, mod, norm_f, _pick(S, SMEM_BLOCK_1D // 2))
```

```python
import functools
import math

import jax
import jax.numpy as jnp
import numpy as np
from jax import lax
from jax.experimental import pallas as pl
from jax.experimental.pallas import tpu as pltpu

F32 = jnp.float32
BF16 = jnp.bfloat16

D_MODEL = 1024
DA_HEADS = 4
DA_HD = 64
DA_WIDTH = DA_HEADS * 2 * DA_HD
ML_HEADS = 4
ML_HD = 128
ML_WIDTH = ML_HEADS * ML_HD
CONV_W = 4
N_GROUPS = 4
EXPERTS_PER_GROUP = 8
N_EXPERTS = N_GROUPS * EXPERTS_PER_GROUP
D_EXPERT = 256
EPS = 1e-6
LAMBDA_INIT = 0.8 - 0.6 * math.exp(-0.3 * 0)

LANES = 128
SUBLANES = 8
SMEM_BLOCK_1D = 1024
NEG = -1e30
VMEM_LIMIT = 56 * 1024 * 1024

COL_MLQK = 0
COL_GA = 1024
COL_GM = 2048
COL_DAQ = 3072
COL_DAK = 3584
COL_DAV = 4096
COL_MLV = 4608
COL_MLO = 5120
COL_IF = 5632
N_PROJ = COL_IF + LANES
PROJ_TN = 2048

EXPERT_ROWS = 512
PROJ_TM = 512
ATTN_TQ = 1024
ATTN_TK = 512
MLSTM_L = 256
CONV_TAIL = 16
POST_TM = 512
ROUTE_ROWS = 48


def _cparams(sem):
    return pltpu.CompilerParams(dimension_semantics=sem, vmem_limit_bytes=VMEM_LIMIT)


ROW_PARTS = D_MODEL // LANES
assert ROW_PARTS == SUBLANES


def _store_rows(ref, lead, x):
    n = x.shape[0]
    for a in range(ROW_PARTS):
        ref[lead + (pl.ds(a, n, stride=ROW_PARTS), slice(None))] = x[:, a * LANES:(a + 1) * LANES]


def _load_rows(ref, lead, n):
    return jnp.concatenate([ref[lead + (pl.ds(a, n, stride=ROW_PARTS), slice(None))] for a in range(ROW_PARTS)],
                           axis=1)


def _sigmoid(x):
    return 0.5 * jnp.tanh(0.5 * x) + 0.5


def _ada_kernel(c_ref, w_ref, b_ref, o_ref):
    c = c_ref[...]
    cs = c * _sigmoid(c)
    o_ref[...] = jnp.dot(cs, w_ref[0], precision=lax.Precision.HIGHEST,
                         preferred_element_type=F32) + b_ref[...]


def _ada(c, w, b):
    B, D = c.shape
    N = w.shape[2]
    tn = 1536
    return pl.pallas_call(
        _ada_kernel,
        grid=(N // tn,),
        in_specs=[pl.BlockSpec((B, D), lambda j: (0, 0)),
                  pl.BlockSpec((1, D, tn), lambda j: (0, 0, j)),
                  pl.BlockSpec((1, tn), lambda j: (0, j))],
        out_specs=pl.BlockSpec((B, tn), lambda j: (0, j)),
        out_shape=jax.ShapeDtypeStruct((B, N), F32),
        compiler_params=_cparams(("arbitrary",)),
    )(c, w, b)


def _w_in_kernel(w_ref, o_ref):
    w = w_ref[0]
    o_q, o_k, o_v, o_qk, o_mv, o_mo, o_if, o_ga, o_gm = np.cumsum((0,) + (
        DA_WIDTH, DA_WIDTH, DA_WIDTH, 2 * ML_WIDTH, ML_WIDTH, ML_WIDTH, 2 * ML_HEADS, D_MODEL))
    rows = w.shape[0]
    o_ref[...] = jnp.concatenate([
        w[:, o_qk:o_qk + 2 * ML_WIDTH], w[:, o_ga:o_ga + D_MODEL], w[:, o_gm:o_gm + D_MODEL],
        w[:, o_q:o_q + DA_WIDTH] * (DA_HD ** -0.5), w[:, o_k:o_k + DA_WIDTH], w[:, o_v:o_v + DA_WIDTH],
        w[:, o_mv:o_mv + ML_WIDTH], w[:, o_mo:o_mo + ML_WIDTH], w[:, o_if:o_if + 2 * ML_HEADS],
        jnp.zeros((rows, LANES - 2 * ML_HEADS), F32)], axis=1).astype(BF16)


def _permute_w_in(w_in):
    _, D, n_in = w_in.shape
    tr = 128
    return pl.pallas_call(
        _w_in_kernel,
        grid=(D // tr,),
        in_specs=[pl.BlockSpec((1, tr, n_in), lambda i: (0, i, 0))],
        out_specs=pl.BlockSpec((tr, N_PROJ), lambda i: (i, 0)),
        out_shape=jax.ShapeDtypeStruct((D, N_PROJ), BF16),
        compiler_params=_cparams(("arbitrary",)),
    )(w_in)


def _proj_kernel(x_ref, mod_ref, g_ref, bif_ref, w_ref, p_ref, gate_ref):
    x = x_ref[0]
    y = x * lax.rsqrt(jnp.mean(x * x, axis=-1, keepdims=True) + EPS) * g_ref[...]
    h = (y * (1.0 + mod_ref[0, 1:2, :]) + mod_ref[0, 0:1, :]).astype(BF16)
    for lo in range(0, N_PROJ, PROJ_TN):
        hi = min(lo + PROJ_TN, N_PROJ)
        acc = jnp.dot(h, w_ref[:, lo:hi], preferred_element_type=F32)
        p_ref[0, :, lo:hi] = acc.astype(BF16)
        if hi == N_PROJ:
            gate_ref[0] = acc[:, hi - lo - LANES:] + bif_ref[...]


def _proj(x, mod, norm1, bif, w_bf16, tm):
    B, S, D = x.shape
    return pl.pallas_call(
        _proj_kernel,
        grid=(B, S // tm),
        in_specs=[pl.BlockSpec((1, tm, D), lambda b, i: (b, i, 0)),
                  pl.BlockSpec((1, 6, D), lambda b, i: (b, 0, 0)),
                  pl.BlockSpec((1, D), lambda b, i: (0, 0)),
                  pl.BlockSpec((1, LANES), lambda b, i: (0, 0)),
                  pl.BlockSpec((D, N_PROJ), lambda b, i: (0, 0), pipeline_mode=pl.Buffered(1))],
        out_specs=[pl.BlockSpec((1, tm, N_PROJ), lambda b, i: (b, i, 0)),
                   pl.BlockSpec((1, tm, LANES), lambda b, i: (b, i, 0))],
        out_shape=[jax.ShapeDtypeStruct((B, S, N_PROJ), BF16),
                   jax.ShapeDtypeStruct((B, S, LANES), F32)],
        compiler_params=_cparams(("arbitrary", "arbitrary")),
    )(x, mod, norm1.reshape(1, D), bif, w_bf16)


ATTN_SUM_ROWS = 16

ATTN_HEADS = 2


def _attn_block(q_ref, k_ref, v_ref, eq_ref, ek_ref, m_scr, acc_scr, off, cbs, masked, q0, tq, tk):
    qs = slice(q0, tq)
    lane = lax.broadcasted_iota(jnp.int32, (1, LANES), 1)
    if masked:
        key = lax.broadcasted_iota(jnp.int32, (tk, tq - q0), 0)
        qry = lax.broadcasted_iota(jnp.int32, (tk, tq - q0), 1) + q0
        valid = (key - qry) <= off
    scores = []
    for u in range(ATTN_HEADS):
        hl = slice(u * LANES, (u + 1) * LANES)
        q = q_ref[0, qs, hl]
        k = k_ref[0, :, hl]
        for c in range(2):
            sel = (lane < DA_HD) if c == 0 else (lane >= DA_HD)
            s = lax.dot_general(jnp.where(sel, k, ek_ref[u]), jnp.where(sel, q, eq_ref[u, qs]),
                                (((1,), (1,)), ((), ())), preferred_element_type=F32)
            scores.append(jnp.where(valid, s, NEG) if masked else s)
    for u in range(ATTN_HEADS):
        vt = jnp.concatenate([v_ref[0, :, u * LANES:(u + 1) * LANES].T,
                              jnp.ones((ATTN_SUM_ROWS, tk), BF16)], axis=0)
        for c in range(2):
            n = 2 * u + c
            s = scores[n]
            m_prev = m_scr[n, :, qs]
            m_new = jnp.maximum(m_prev, jnp.max(s, axis=0, keepdims=True) - cbs[u])
            alpha = jnp.exp(m_prev - m_new)
            p = jnp.exp((s - (m_new + cbs[u])).astype(BF16))
            acc_scr[n, :, qs] = alpha * acc_scr[n, :, qs] + jnp.dot(vt, p, preferred_element_type=F32)
            m_scr[n, :, qs] = m_new


def _attn_kernel(it_ref, jt_ref, slope_ref, q_ref, k_ref, v_ref, eq_ref, ek_ref, lam_ref, dn_ref, o_ref,
                 m_scr, acc_scr, *, tq, tk):
    hp = pl.program_id(1)
    step = pl.program_id(2)
    i = it_ref[step]
    j = jt_ref[step]
    ratio = tq // tk

    @pl.when(j == 0)
    def _():
        m_scr[...] = jnp.full(m_scr.shape, NEG, F32)
        acc_scr[...] = jnp.zeros(acc_scr.shape, F32)

    off = i * tq - j * tk
    cbs = [slope_ref[hp * ATTN_HEADS + u] * off.astype(F32) for u in range(ATTN_HEADS)]
    args = (q_ref, k_ref, v_ref, eq_ref, ek_ref, m_scr, acc_scr, off, cbs)

    @pl.when(j < i * ratio)
    def _():
        _attn_block(*args, masked=False, q0=0, tq=tq, tk=tk)

    for r in range(ratio):
        @pl.when(j == i * ratio + r)
        def _():
            _attn_block(*args, masked=True, q0=r * tk, tq=tq, tk=tk)

    @pl.when(j == (i + 1) * ratio - 1)
    def _():
        lv = lam_ref[...]
        lam = (jnp.exp(jnp.sum(lv[0:1] * lv[1:2], axis=-1, keepdims=True))
               - jnp.exp(jnp.sum(lv[2:3] * lv[3:4], axis=-1, keepdims=True)) + LAMBDA_INIT)
        vd = 2 * DA_HD
        for u in range(ATTN_HEADS):
            a0, a1 = acc_scr[2 * u], acc_scr[2 * u + 1]
            o = a0[0:vd] / a0[vd:vd + 1] - lam * (a1[0:vd] / a1[vd:vd + 1])
            o = o * lax.rsqrt(jnp.mean(o * o, axis=0, keepdims=True) + EPS)
            o_ref[0, :, u * LANES:(u + 1) * LANES] = (o.T * dn_ref[...] * (1.0 - LAMBDA_INIT)).astype(BF16)


def _alibi_columns(n, slopes, q_side):
    assert n <= 256 * 256 and all(math.log2(s).is_integer() for s in slopes)
    pos = np.arange(n)
    lo, hi = (pos % 256).astype(np.float64), (pos // 256 * 256).astype(np.float64)
    out = np.zeros((len(slopes), n, LANES), np.float64)
    for h, s in enumerate(slopes):
        cols = (-s * lo, -s * hi, np.ones(n), np.ones(n)) if q_side else (np.ones(n), np.ones(n), s * lo, s * hi)
        for base in (0, DA_HD):
            for c, v in enumerate(cols):
                out[h, :, base + c] = v
    return jnp.asarray(out, dtype=BF16)


def _attn(p, lamv, diff_norm, tq, tk):
    B, S, _ = p.shape
    nq, ratio = S // tq, tq // tk
    slopes = [2.0 ** (-8.0 * (h + 1) / DA_HEADS) for h in range(DA_HEADS)]
    steps = [(i, j) for i in range(nq) for j in range((i + 1) * ratio)]
    it = jnp.asarray([s[0] for s in steps], jnp.int32)
    jt = jnp.asarray([s[1] for s in steps], jnp.int32)
    hw = ATTN_HEADS * LANES
    qb, kb, vb = COL_DAQ // hw, COL_DAK // hw, COL_DAV // hw
    grid_spec = pltpu.PrefetchScalarGridSpec(
        num_scalar_prefetch=3,
        grid=(B, DA_HEADS // ATTN_HEADS, len(steps)),
        in_specs=[pl.BlockSpec((1, tq, hw), lambda b, h, s, it, jt, sl: (b, it[s], qb + h)),
                  pl.BlockSpec((1, tk, hw), lambda b, h, s, it, jt, sl: (b, jt[s], kb + h)),
                  pl.BlockSpec((1, tk, hw), lambda b, h, s, it, jt, sl: (b, jt[s], vb + h)),
                  pl.BlockSpec((ATTN_HEADS, tq, LANES), lambda b, h, s, it, jt, sl: (h, 0, 0)),
                  pl.BlockSpec((ATTN_HEADS, tk, LANES), lambda b, h, s, it, jt, sl: (h, 0, 0)),
                  pl.BlockSpec((4, DA_HD), lambda b, h, s, it, jt, sl: (0, 0)),
                  pl.BlockSpec((1, 2 * DA_HD), lambda b, h, s, it, jt, sl: (0, 0))],
        out_specs=pl.BlockSpec((1, tq, hw), lambda b, h, s, it, jt, sl: (b, it[s], h)),
        scratch_shapes=[pltpu.VMEM((2 * ATTN_HEADS, 1, tq), F32),
                        pltpu.VMEM((2 * ATTN_HEADS, 2 * DA_HD + ATTN_SUM_ROWS, tq), F32)],
    )
    return pl.pallas_call(
        functools.partial(_attn_kernel, tq=tq, tk=tk),
        grid_spec=grid_spec,
        out_shape=jax.ShapeDtypeStruct((B, S, DA_WIDTH), BF16),
        compiler_params=_cparams(("arbitrary",) * 3),
    )(it, jt, jnp.asarray(slopes, F32), p, p, p, _alibi_columns(tq, slopes, True),
      _alibi_columns(tk, slopes, False), lamv, diff_norm.reshape(1, 2 * DA_HD))


def _mlstm_kernel(qk_ref, v_ref, o_ref, g_ref, cw_ref, cb_ref, nw_ref, y_ref,
                  ext_scr, c_scr, n_scr, m_scr, *, L):
    i = pl.program_id(1)

    @pl.when(i == 0)
    def _():
        ext_scr[...] = jnp.zeros(ext_scr.shape, BF16)
        c_scr[...] = jnp.zeros(c_scr.shape, F32)
        n_scr[...] = jnp.zeros(n_scr.shape, F32)
        m_scr[...] = jnp.zeros(m_scr.shape, F32)

    raw = qk_ref[0]
    tail = ext_scr[...]
    t_out = lax.broadcasted_iota(jnp.int32, (L, L), 0)
    t_in = lax.broadcasted_iota(jnp.int32, (L, L), 1)
    h_out = lax.broadcasted_iota(jnp.int32, (CONV_TAIL, CONV_TAIL), 0)
    h_in = lax.broadcasted_iota(jnp.int32, (CONV_TAIL, CONV_TAIL), 1)
    conv = cb_ref[...] + raw.astype(F32) * cw_ref[CONV_W - 1:CONV_W, :]
    for k in range(1, CONV_W):
        shift = jnp.where(t_in == t_out - k, 1.0, 0.0).astype(BF16)
        head = jnp.where(h_in == h_out - k + CONV_TAIL, 1.0, 0.0).astype(BF16)
        xk = jnp.dot(shift, raw, preferred_element_type=F32)
        fix = jnp.dot(head, tail, preferred_element_type=F32)
        xk = jnp.concatenate([xk[:CONV_TAIL] + fix, xk[CONV_TAIL:]], axis=0)
        conv = conv + xk * cw_ref[CONV_W - 1 - k:CONV_W - k, :]
    ext_scr[...] = raw[L - CONV_TAIL:L, :]
    qkc = conv * _sigmoid(conv)

    gts = g_ref[0]
    fpre = pltpu.roll(gts, LANES - ML_HEADS, axis=1)
    lf = jnp.minimum(fpre, 0.0) - jnp.log(1.0 + jnp.exp(-jnp.abs(fpre)))
    row = lax.broadcasted_iota(jnp.int32, (L, L), 0)
    col = lax.broadcasted_iota(jnp.int32, (L, L), 1)
    causal = col <= row
    bcum = jnp.dot(causal.astype(F32), lf, precision=lax.Precision.HIGHEST, preferred_element_type=F32)
    r = gts - bcum
    rt = r.T
    m_all = m_scr[...]
    lane = lax.broadcasted_iota(jnp.int32, (1, LANES), 1)
    m_next = m_all
    v_all = v_ref[0]
    o_all = o_ref[0]
    for h in range(ML_HEADS):
        hs = slice(h * ML_HD, (h + 1) * ML_HD)
        bcol = bcum[:, h:h + 1]
        rcol = r[:, h:h + 1]
        rrow = rt[h:h + 1, :]
        g = bcum[L - 1:L, h:h + 1]
        mh = m_all[:, h:h + 1]
        dm = jnp.where(causal, bcol + rrow, NEG)
        inter = bcol + mh
        mj = jnp.maximum(inter, jnp.max(dm, axis=-1, keepdims=True))
        w_intra = jnp.exp(dm - mj)
        w_inter = jnp.exp(inter - mj)
        qh = qkc[:, hs]
        kh = qkc[:, ML_WIDTH + h * ML_HD:ML_WIDTH + (h + 1) * ML_HD] * (ML_HD ** -0.5)
        vh = v_all[:, hs]
        qb = qh.astype(BF16)
        kb = kh.astype(BF16)
        s = lax.dot_general(qb, kb, (((1,), (1,)), ((), ())), preferred_element_type=F32) * w_intra
        c_old = c_scr[h]
        n_old = n_scr[h:h + 1, :]
        num = (jnp.dot(s.astype(BF16), vh, preferred_element_type=F32)
               + lax.dot_general(qb, c_old.astype(BF16), (((1,), (1,)), ((), ())),
                                 preferred_element_type=F32) * w_inter)
        den = (jnp.sum(s, axis=-1, keepdims=True)
               + w_inter * jnp.sum(qh * n_old, axis=-1, keepdims=True))
        denom = jnp.maximum(jnp.abs(den), jnp.exp(-mj))
        ht = num / denom
        a_col = g + rcol
        m_new = jnp.maximum(g + mh, jnp.max(a_col, axis=0, keepdims=True))
        wa = jnp.exp(a_col - m_new)
        decay = jnp.exp(g + mh - m_new)
        vw_t = (vh.astype(F32) * wa).T.astype(BF16)
        c_scr[h] = decay * c_old + jnp.dot(vw_t, kb, preferred_element_type=F32)
        n_scr[h:h + 1, :] = decay * n_old + jnp.sum(kh * wa, axis=0, keepdims=True)
        m_next = jnp.where(lane == h, m_new, m_next)
        z = _sigmoid(o_all[:, hs].astype(F32)) * ht
        z = z * lax.rsqrt(jnp.mean(z * z, axis=-1, keepdims=True) + EPS) * nw_ref[...]
        y_ref[0, :, hs] = z.astype(BF16)
    m_scr[...] = m_next


def _mlstm(p, gates, conv_w, conv_b, mlstm_norm, L):
    B, S, _ = p.shape
    return pl.pallas_call(
        functools.partial(_mlstm_kernel, L=L),
        grid=(B, S // L),
        in_specs=[pl.BlockSpec((1, L, 2 * ML_WIDTH), lambda b, i: (b, i, COL_MLQK // (2 * ML_WIDTH))),
                  pl.BlockSpec((1, L, ML_WIDTH), lambda b, i: (b, i, COL_MLV // ML_WIDTH)),
                  pl.BlockSpec((1, L, ML_WIDTH), lambda b, i: (b, i, COL_MLO // ML_WIDTH)),
                  pl.BlockSpec((1, L, LANES), lambda b, i: (b, i, 0)),
                  pl.BlockSpec((CONV_W, 2 * ML_WIDTH), lambda b, i: (0, 0)),
                  pl.BlockSpec((1, 2 * ML_WIDTH), lambda b, i: (0, 0)),
                  pl.BlockSpec((1, ML_HD), lambda b, i: (0, 0))],
        out_specs=pl.BlockSpec((1, L, ML_WIDTH), lambda b, i: (b, i, 0)),
        out_shape=jax.ShapeDtypeStruct((B, S, ML_WIDTH), BF16),
        scratch_shapes=[pltpu.VMEM((CONV_TAIL, 2 * ML_WIDTH), BF16),
                        pltpu.VMEM((ML_HEADS, ML_HD, ML_HD), F32),
                        pltpu.VMEM((SUBLANES, ML_HD), F32),
                        pltpu.VMEM((1, LANES), F32)],
        compiler_params=_cparams(("arbitrary", "arbitrary")),
    )(p, p, p, gates, conv_w, conv_b.reshape(1, -1), mlstm_norm.reshape(1, ML_HD))


def _post_kernel(ya_ref, ym_ref, ga_ref, gm_ref, x_ref, mod_ref, wa_ref, wm_ref, wo_ref, n2_ref,
                 wr_ref, br_ref, x1_ref, h2_ref, ri_ref, rw_ref, cnt_ref, run_scr, *, tm):
    first = jnp.logical_and(pl.program_id(0) == 0, pl.program_id(1) == 0)

    @pl.when(first)
    def _():
        run_scr[...] = jnp.zeros(run_scr.shape, F32)

    a = jnp.dot(ya_ref[0], wa_ref[...], preferred_element_type=F32)
    m = jnp.dot(ym_ref[0], wm_ref[...], preferred_element_type=F32)
    merged = _sigmoid(ga_ref[0]).astype(F32) * a + _sigmoid(gm_ref[0]).astype(F32) * m
    o = jnp.dot(merged.astype(BF16), wo_ref[...], preferred_element_type=F32)
    x1 = x_ref[0] + mod_ref[0, 2:3, :] * o
    x1_ref[0] = x1
    h2 = x1 * lax.rsqrt(jnp.mean(x1 * x1, axis=-1, keepdims=True) + EPS) * n2_ref[...]
    h2 = h2 * (1.0 + mod_ref[0, 4:5, :]) + mod_ref[0, 3:4, :]
    _store_rows(h2_ref, (0,), h2)

    hi = h2.astype(BF16)
    lo = (h2 - hi.astype(F32)).astype(BF16)
    logits = lax.dot_general(wr_ref[...], jnp.concatenate([hi, lo, hi], axis=1), (((1,), (1,)), ((), ())),
                             preferred_element_type=F32) + br_ref[...]
    sub = lax.broadcasted_iota(jnp.int32, (ROUTE_ROWS, tm), 0)
    big = jnp.int32(4 * ROUTE_ROWS)
    gl = jnp.where(sub < N_GROUPS, logits, NEG)
    gmax = jnp.max(gl, axis=0, keepdims=True)
    gsel = jnp.min(jnp.where(gl == gmax, sub, big), axis=0, keepdims=True)
    pgrp = 1.0 / jnp.sum(jnp.exp(gl - gmax), axis=0, keepdims=True)
    first = N_GROUPS + EXPERTS_PER_GROUP * gsel
    el = jnp.where(jnp.logical_and(sub >= first, sub < first + EXPERTS_PER_GROUP), logits, NEG)
    e1 = jnp.max(el, axis=0, keepdims=True)
    i1 = jnp.min(jnp.where(el == e1, sub, big), axis=0, keepdims=True)
    el2 = jnp.where(sub == i1, NEG, el)
    e2 = jnp.max(el2, axis=0, keepdims=True)
    i2 = jnp.min(jnp.where(el2 == e2, sub, big), axis=0, keepdims=True)
    tt = jnp.exp(e2 - e1)
    w1 = pgrp / (1.0 + tt)
    w2 = pgrp * tt / (1.0 + tt)

    oh1 = jnp.where(sub == i1, 1.0, 0.0).astype(F32)
    oh2 = jnp.where(sub == i2, 1.0, 0.0).astype(F32)
    cat = jnp.concatenate([oh1, oh2], axis=0).astype(BF16)
    src = lax.broadcasted_iota(jnp.int32, (tm, tm), 0)
    dst = lax.broadcasted_iota(jnp.int32, (tm, tm), 1)
    before = jnp.where(src < dst, 1.0, 0.0).astype(BF16)
    earlier = jnp.dot(cat, before, preferred_element_type=F32)
    c1 = jnp.sum(oh1, axis=1, keepdims=True)
    c2 = jnp.sum(oh2, axis=1, keepdims=True)
    run = run_scr[...]
    rank1 = jnp.sum((earlier[:ROUTE_ROWS] + run) * oh1, axis=0, keepdims=True)
    rank2 = jnp.sum((earlier[ROUTE_ROWS:] + run + c1) * oh2, axis=0, keepdims=True)
    run_new = run + c1 + c2
    run_scr[...] = run_new
    cnt_ref[...] = jnp.broadcast_to(run_new, cnt_ref.shape)
    row = lax.broadcasted_iota(jnp.int32, (SUBLANES, tm), 0)
    ri_ref[0] = jnp.where(row == 0, i1 - N_GROUPS,
                          jnp.where(row == 1, i2 - N_GROUPS,
                                    jnp.where(row == 2, rank1.astype(jnp.int32),
                                              jnp.where(row == 3, rank2.astype(jnp.int32), 0))))
    rw_ref[0] = jnp.where(row == 0, w1, jnp.where(row == 1, w2, 0.0))


def _post(ya, ym, p, x, mod, wa, wm, wo, norm2, wr, br, tm):
    B, S, D = x.shape
    tok = lambda b, i: (b, i, 0)
    const = lambda b, i: (0, 0)
    return pl.pallas_call(
        functools.partial(_post_kernel, tm=tm),
        grid=(B, S // tm),
        in_specs=[pl.BlockSpec((1, tm, DA_WIDTH), tok),
                  pl.BlockSpec((1, tm, ML_WIDTH), tok),
                  pl.BlockSpec((1, tm, D), lambda b, i: (b, i, COL_GA // D_MODEL)),
                  pl.BlockSpec((1, tm, D), lambda b, i: (b, i, COL_GM // D_MODEL)),
                  pl.BlockSpec((1, tm, D), tok),
                  pl.BlockSpec((1, 6, D), lambda b, i: (b, 0, 0)),
                  pl.BlockSpec((DA_WIDTH, D), const),
                  pl.BlockSpec((ML_WIDTH, D), const),
                  pl.BlockSpec((D, D), const),
                  pl.BlockSpec((1, D), const),
                  pl.BlockSpec((ROUTE_ROWS, 3 * D), const),
                  pl.BlockSpec((ROUTE_ROWS, 1), const)],
        out_specs=[pl.BlockSpec((1, tm, D), tok),
                   pl.BlockSpec((1, tm * ROW_PARTS, LANES), tok),
                   pl.BlockSpec((1, SUBLANES, tm), lambda b, i: (b, 0, i)),
                   pl.BlockSpec((1, SUBLANES, tm), lambda b, i: (b, 0, i)),
                   pl.BlockSpec((ROUTE_ROWS, LANES), const)],
        out_shape=[jax.ShapeDtypeStruct((B, S, D), F32),
                   jax.ShapeDtypeStruct((B, S * ROW_PARTS, LANES), F32),
                   jax.ShapeDtypeStruct((B, SUBLANES, S), jnp.int32),
                   jax.ShapeDtypeStruct((B, SUBLANES, S), F32),
                   jax.ShapeDtypeStruct((ROUTE_ROWS, LANES), F32)],
        scratch_shapes=[pltpu.VMEM((ROUTE_ROWS, 1), F32)],
        compiler_params=_cparams(("arbitrary", "arbitrary")),
    )(ya, ym, p, p, x, mod, wa, wm, wo, norm2.reshape(1, D), wr, br)


DMA_UNROLL = 8


def _dispatch_kernel(ends_ref, pc_ref, dest_ref, h2_ref, xs_hbm, zbuf, sem, zsem, *, td):
    i = pl.program_id(0)
    tile_rows = EXPERT_ROWS * ROW_PARTS

    def zero_tile(first_row):
        start = pl.multiple_of(first_row * ROW_PARTS, tile_rows)
        cp = pltpu.make_async_copy(zbuf, xs_hbm.at[pl.ds(start, tile_rows)], zsem)
        cp.start()
        cp.wait()

    @pl.when(i == 0)
    def _():
        zbuf[...] = jnp.zeros(zbuf.shape, zbuf.dtype)
        for e in range(N_EXPERTS):
            @pl.when(pc_ref[e] > 0)
            def _():
                zero_tile(ends_ref[e] - EXPERT_ROWS)

        def fill(tile, carry):
            zero_tile(tile * EXPERT_ROWS)
            return carry

        lax.fori_loop(ends_ref[N_EXPERTS - 1] // EXPERT_ROWS, xs_hbm.shape[0] // tile_rows, fill, 0)

    def issue(t, carry):
        src = h2_ref.at[pl.ds(pl.multiple_of(t * ROW_PARTS, ROW_PARTS), ROW_PARTS)]
        for s in range(2):
            dst = pl.multiple_of(dest_ref[2 * t + s] * ROW_PARTS, ROW_PARTS)
            pltpu.make_async_copy(src, xs_hbm.at[pl.ds(dst, ROW_PARTS)], sem).start(priority=s)
        return carry

    lax.fori_loop(0, td, issue, 0, unroll=DMA_UNROLL)
    for s in range(2):
        pltpu.make_async_copy(h2_ref, xs_hbm.at[pl.ds(0, td * ROW_PARTS)], sem).wait()


def _dispatch(ends, pc, dest_flat, h2, n_rows, td):
    T = h2.shape[0] // ROW_PARTS
    grid_spec = pltpu.PrefetchScalarGridSpec(
        num_scalar_prefetch=2,
        grid=(T // td,),
        in_specs=[pl.BlockSpec((2 * td,), lambda i, e, c: (i,), memory_space=pltpu.SMEM),
                  pl.BlockSpec((td * ROW_PARTS, LANES), lambda i, e, c: (i, 0))],
        out_specs=pl.BlockSpec(memory_space=pl.ANY),
        scratch_shapes=[pltpu.VMEM((EXPERT_ROWS * ROW_PARTS, LANES), h2.dtype),
                        pltpu.SemaphoreType.DMA(()), pltpu.SemaphoreType.DMA(())],
    )
    return pl.pallas_call(
        functools.partial(_dispatch_kernel, td=td),
        grid_spec=grid_spec,
        out_shape=jax.ShapeDtypeStruct((n_rows * ROW_PARTS, LANES), h2.dtype),
        compiler_params=_cparams(("arbitrary",)),
    )(ends, pc, dest_flat, h2)


def _experts_kernel(te_ref, nt_ref, xs_ref, w1_ref, w3_ref, w2_ref, ys_ref):
    i = pl.program_id(0)

    @pl.when(i < nt_ref[0])
    def _():
        x = _load_rows(xs_ref, (), EXPERT_ROWS).astype(BF16)
        a = jnp.dot(x, w1_ref[0, 0, 0].astype(BF16), preferred_element_type=F32)
        b = jnp.dot(x, w3_ref[0, 0, 0].astype(BF16), preferred_element_type=F32)
        hid = (a * _sigmoid(a) * b).astype(BF16)
        _store_rows(ys_ref, (), jnp.dot(hid, w2_ref[0, 0, 0].astype(BF16), preferred_element_type=F32))

    @pl.when(i >= nt_ref[0])
    def _():
        ys_ref[...] = jnp.zeros(ys_ref.shape, ys_ref.dtype)


def _experts(tile_e, n_tiles, xs, w1, w3, w2):
    n_rows = xs.shape[0] // ROW_PARTS
    D = D_MODEL
    nt = n_rows // EXPERT_ROWS
    tile = (EXPERT_ROWS * ROW_PARTS, LANES)
    rows = lambda i, te, n: (jnp.minimum(i, jnp.maximum(n[0] - 1, 0)), 0)
    wsel = lambda i, te, n: (0, te[i] // EXPERTS_PER_GROUP, te[i] % EXPERTS_PER_GROUP, 0, 0)
    grid_spec = pltpu.PrefetchScalarGridSpec(
        num_scalar_prefetch=2,
        grid=(nt,),
        in_specs=[pl.BlockSpec(tile, rows),
                  pl.BlockSpec((1, 1, 1, D, D_EXPERT), wsel),
                  pl.BlockSpec((1, 1, 1, D, D_EXPERT), wsel),
                  pl.BlockSpec((1, 1, 1, D_EXPERT, D), wsel)],
        out_specs=pl.BlockSpec(tile, lambda i, te, n: (i, 0)),
    )
    return pl.pallas_call(
        _experts_kernel,
        grid_spec=grid_spec,
        out_shape=jax.ShapeDtypeStruct((n_rows * ROW_PARTS, LANES), F32),
        compiler_params=_cparams(("arbitrary",)),
    )(tile_e, n_tiles, xs, w1, w3, w2)


def _combine_kernel(dcur_ref, dnxt_ref, ys_hbm, x1_ref, rw_ref, mod_ref, nf_ref, o_ref, ybuf, sem, *, tc, nsteps):
    g = pl.program_id(0)
    slot = g % 2

    def issue(dest_ref, sl, t):
        dst = pl.ds(pl.multiple_of(t * ROW_PARTS, ROW_PARTS), ROW_PARTS)
        for s in range(2):
            src = pl.multiple_of(dest_ref[2 * t + s] * ROW_PARTS, ROW_PARTS)
            pltpu.make_async_copy(ys_hbm.at[pl.ds(src, ROW_PARTS)], ybuf.at[sl, s, dst],
                                  sem.at[sl]).start(priority=s)

    def drain(sl):
        for s in range(2):
            pltpu.make_async_copy(ys_hbm.at[pl.ds(0, tc * ROW_PARTS)], ybuf.at[sl, s], sem.at[sl]).wait()

    @pl.when(g == 0)
    def _():
        lax.fori_loop(0, tc, lambda t, c: (issue(dcur_ref, 0, t), c)[1], 0, unroll=DMA_UNROLL)

    drain(slot)
    for t in range(tc):
        issue(dnxt_ref, 1 - slot, t)

    rw = jnp.concatenate([rw_ref[0], jnp.zeros((LANES - SUBLANES, tc), F32)], axis=0).T
    y = rw[:, 0:1] * _load_rows(ybuf, (slot, 0), tc) + rw[:, 1:2] * _load_rows(ybuf, (slot, 1), tc)
    xo = x1_ref[0] + mod_ref[0, 5:6, :] * y
    o_ref[0] = xo * lax.rsqrt(jnp.mean(xo * xo, axis=-1, keepdims=True) + EPS) * nf_ref[...]

    @pl.when(g == nsteps - 1)
    def _():
        drain(1 - slot)


def _combine(dest_flat, ys, x1, rw, mod, norm_f, tc):
    B, S, D = x1.shape
    n = S // tc
    nsteps = B * n
    tok = lambda g: (g // n, g % n, 0)
    return pl.pallas_call(
        functools.partial(_combine_kernel, tc=tc, nsteps=nsteps),
        grid=(nsteps,),
        in_specs=[pl.BlockSpec((2 * tc,), lambda g: (g,), memory_space=pltpu.SMEM),
                  pl.BlockSpec((2 * tc,), lambda g: (jnp.minimum(g + 1, nsteps - 1),), memory_space=pltpu.SMEM),
                  pl.BlockSpec(memory_space=pl.ANY),
                  pl.BlockSpec((1, tc, D), tok),
                  pl.BlockSpec((1, SUBLANES, tc), lambda g: (g // n, 0, g % n)),
                  pl.BlockSpec((1, 6, D), lambda g: (g // n, 0, 0)),
                  pl.BlockSpec((1, D), lambda g: (0, 0))],
        out_specs=pl.BlockSpec((1, tc, D), tok),
        out_shape=jax.ShapeDtypeStruct((B, S, D), F32),
        scratch_shapes=[pltpu.VMEM((2, 2, tc * ROW_PARTS, LANES), F32), pltpu.SemaphoreType.DMA((2,))],
        compiler_params=_cparams(("arbitrary",)),
    )(dest_flat, dest_flat, ys, x1, rw, mod, norm_f.reshape(1, D))


def _pick(n, pref):
    t = min(n, pref)
    assert n % t == 0, (n, pref)
    return t


def kernel(x, c, w_ada, b_ada, norm1, w_in, b_if, conv_w, conv_b, lam_q1, lam_k1, lam_q2, lam_k2,
           diff_norm, mlstm_norm, w_br_a, w_br_m, w_out, norm2, w_rg, b_rg, w_re, b_re,
           w_e1, w_e3, w_e2, norm_f):
    B, S, D = x.shape
    assert D == D_MODEL and w_ada.shape[0] == 1
    T = B * S
    l = 0

    mod = _ada(c, w_ada, b_ada).reshape(B, 6, D)

    w_perm = _permute_w_in(w_in)
    bif = jnp.concatenate([b_if[l], jnp.zeros((LANES - 2 * ML_HEADS,), F32)]).reshape(1, LANES)

    p, gates = _proj(x, mod, norm1[l], bif, w_perm, _pick(S, PROJ_TM))

    lamv = jnp.stack([lam_q1[l], lam_k1[l], lam_q2[l], lam_k2[l]])
    ya = _attn(p, lamv, diff_norm[l], _pick(S, ATTN_TQ), _pick(S, ATTN_TK))
    ym = _mlstm(p, gates, conv_w[l], conv_b[l], mlstm_norm[l], _pick(S, MLSTM_L))

    pad_rows = ROUTE_ROWS - N_GROUPS - N_EXPERTS
    wr = jnp.concatenate([w_rg[l], w_re[l], jnp.zeros((D, pad_rows), F32)], axis=1).T
    wr_hi = wr.astype(BF16)
    wr_lo = (wr - wr_hi.astype(F32)).astype(BF16)
    wr = jnp.concatenate([wr_hi, wr_hi, wr_lo], axis=1)
    br = jnp.concatenate([b_rg[l], b_re[l], jnp.zeros((pad_rows,), F32)]).reshape(ROUTE_ROWS, 1)
    x1, h2, ri, rw, cnt = _post(ya, ym, p, x, mod, w_br_a[l].astype(BF16), w_br_m[l].astype(BF16),
                                w_out[l].astype(BF16), norm2[l], wr, br, _pick(S, POST_TM))

    counts = cnt[N_GROUPS:N_GROUPS + N_EXPERTS, 0].astype(jnp.int32)
    pc = ((counts + EXPERT_ROWS - 1) // EXPERT_ROWS) * EXPERT_ROWS
    ends = jnp.cumsum(pc)
    offs = ends - pc
    eid = ri[:, 0:2, :]
    rank = ri[:, 2:4, :]
    dest = rank + jnp.sum(jnp.where(eid[..., None] == jnp.arange(N_EXPERTS), offs, 0), axis=-1)
    dest_flat = dest.transpose(0, 2, 1).reshape(2 * T).astype(jnp.int32)
    n_rows = 2 * T + N_EXPERTS * EXPERT_ROWS
    n_tiles = n_rows // EXPERT_ROWS
    tile_e = jnp.sum(jnp.arange(n_tiles)[:, None] * EXPERT_ROWS >= ends[None, :], axis=1)
    tile_e = jnp.minimum(tile_e, N_EXPERTS - 1).astype(jnp.int32)
    used_tiles = (ends[-1:] // EXPERT_ROWS).astype(jnp.int32)

    xs = _dispatch(ends.astype(jnp.int32), pc.astype(jnp.int32), dest_flat, h2.reshape(T * ROW_PARTS, LANES), n_rows,
                   _pick(T, SMEM_BLOCK_1D))
    ys = _experts(tile_e, used_tiles, xs, w_e1, w_e3, w_e2)
    return _combine(dest_flat, ys, x1, rw, mod, norm_f, _pick(S, SMEM_BLOCK_1D // 2))
```

```python
import functools
import math

import jax
import jax.numpy as jnp
import numpy as np
from jax import lax
from jax.experimental import pallas as pl
from jax.experimental.pallas import tpu as pltpu

F32 = jnp.float32
BF16 = jnp.bfloat16

D_MODEL = 1024
DA_HEADS = 4
DA_HD = 64
DA_WIDTH = DA_HEADS * 2 * DA_HD
ML_HEADS = 4
ML_HD = 128
ML_WIDTH = ML_HEADS * ML_HD
CONV_W = 4
N_GROUPS = 4
EXPERTS_PER_GROUP = 8
N_EXPERTS = N_GROUPS * EXPERTS_PER_GROUP
D_EXPERT = 256
EPS = 1e-6
LAMBDA_INIT = 0.8 - 0.6 * math.exp(-0.3 * 0)

LANES = 128
SUBLANES = 8
SMEM_BLOCK_1D = 1024
NEG = -1e30
VMEM_LIMIT = 56 * 1024 * 1024

COL_MLQK = 0
COL_GA = 1024
COL_GM = 2048
COL_DAQ = 3072
COL_DAK = 3584
COL_DAV = 4096
COL_MLV = 4608
COL_MLO = 5120
COL_IF = 5632
N_PROJ = COL_IF + LANES
PROJ_TN = 2048

EXPERT_ROWS = 512
PROJ_TM = 512
ATTN_TQ = 1024
ATTN_TK = 512
MLSTM_L = 256
CONV_TAIL = 16
POST_TM = 512
ROUTE_ROWS = 48


def _cparams(sem):
    return pltpu.CompilerParams(dimension_semantics=sem, vmem_limit_bytes=VMEM_LIMIT)


ROW_PARTS = D_MODEL // LANES
assert ROW_PARTS == SUBLANES


def _store_rows(ref, lead, x):
    n = x.shape[0]
    for a in range(ROW_PARTS):
        ref[lead + (pl.ds(a, n, stride=ROW_PARTS), slice(None))] = x[:, a * LANES:(a + 1) * LANES]


def _load_rows(ref, lead, n):
    return jnp.concatenate([ref[lead + (pl.ds(a, n, stride=ROW_PARTS), slice(None))] for a in range(ROW_PARTS)],
                           axis=1)


def _sigmoid(x):
    return 0.5 * jnp.tanh(0.5 * x) + 0.5


def _ada_kernel(c_ref, w_ref, b_ref, o_ref):
    c = c_ref[...]
    cs = c * _sigmoid(c)
    o_ref[...] = jnp.dot(cs, w_ref[0], precision=lax.Precision.HIGHEST,
                         preferred_element_type=F32) + b_ref[...]


def _ada(c, w, b):
    B, D = c.shape
    N = w.shape[2]
    tn = 1536
    return pl.pallas_call(
        _ada_kernel,
        grid=(N // tn,),
        in_specs=[pl.BlockSpec((B, D), lambda j: (0, 0)),
                  pl.BlockSpec((1, D, tn), lambda j: (0, 0, j)),
                  pl.BlockSpec((1, tn), lambda j: (0, j))],
        out_specs=pl.BlockSpec((B, tn), lambda j: (0, j)),
        out_shape=jax.ShapeDtypeStruct((B, N), F32),
        compiler_params=_cparams(("arbitrary",)),
    )(c, w, b)


def _w_in_kernel(w_ref, o_ref):
    w = w_ref[0]
    o_q, o_k, o_v, o_qk, o_mv, o_mo, o_if, o_ga, o_gm = np.cumsum((0,) + (
        DA_WIDTH, DA_WIDTH, DA_WIDTH, 2 * ML_WIDTH, ML_WIDTH, ML_WIDTH, 2 * ML_HEADS, D_MODEL))
    rows = w.shape[0]
    o_ref[...] = jnp.concatenate([
        w[:, o_qk:o_qk + 2 * ML_WIDTH], w[:, o_ga:o_ga + D_MODEL], w[:, o_gm:o_gm + D_MODEL],
        w[:, o_q:o_q + DA_WIDTH] * (DA_HD ** -0.5), w[:, o_k:o_k + DA_WIDTH], w[:, o_v:o_v + DA_WIDTH],
        w[:, o_mv:o_mv + ML_WIDTH], w[:, o_mo:o_mo + ML_WIDTH], w[:, o_if:o_if + 2 * ML_HEADS],
        jnp.zeros((rows, LANES - 2 * ML_HEADS), F32)], axis=1).astype(BF16)


def _permute_w_in(w_in):
    _, D, n_in = w_in.shape
    tr = 128
    return pl.pallas_call(
        _w_in_kernel,
        grid=(D // tr,),
        in_specs=[pl.BlockSpec((1, tr, n_in), lambda i: (0, i, 0))],
        out_specs=pl.BlockSpec((tr, N_PROJ), lambda i: (i, 0)),
        out_shape=jax.ShapeDtypeStruct((D, N_PROJ), BF16),
        compiler_params=_cparams(("arbitrary",)),
    )(w_in)


def _proj_kernel(x_ref, mod_ref, g_ref, bif_ref, w_ref, p_ref, gate_ref):
    x = x_ref[0]
    y = x * lax.rsqrt(jnp.mean(x * x, axis=-1, keepdims=True) + EPS) * g_ref[...]
    h = (y * (1.0 + mod_ref[0, 1:2, :]) + mod_ref[0, 0:1, :]).astype(BF16)
    for lo in range(0, N_PROJ, PROJ_TN):
        hi = min(lo + PROJ_TN, N_PROJ)
        acc = jnp.dot(h, w_ref[:, lo:hi], preferred_element_type=F32)
        p_ref[0, :, lo:hi] = acc.astype(BF16)
        if hi == N_PROJ:
            gate_ref[0] = acc[:, hi - lo - LANES:] + bif_ref[...]


def _proj(x, mod, norm1, bif, w_bf16, tm):
    B, S, D = x.shape
    return pl.pallas_call(
        _proj_kernel,
        grid=(B, S // tm),
        in_specs=[pl.BlockSpec((1, tm, D), lambda b, i: (b, i, 0)),
                  pl.BlockSpec((1, 6, D), lambda b, i: (b, 0, 0)),
                  pl.BlockSpec((1, D), lambda b, i: (0, 0)),
                  pl.BlockSpec((1, LANES), lambda b, i: (0, 0)),
                  pl.BlockSpec((D, N_PROJ), lambda b, i: (0, 0), pipeline_mode=pl.Buffered(1))],
        out_specs=[pl.BlockSpec((1, tm, N_PROJ), lambda b, i: (b, i, 0)),
                   pl.BlockSpec((1, tm, LANES), lambda b, i: (b, i, 0))],
        out_shape=[jax.ShapeDtypeStruct((B, S, N_PROJ), BF16),
                   jax.ShapeDtypeStruct((B, S, LANES), F32)],
        compiler_params=_cparams(("arbitrary", "arbitrary")),
    )(x, mod, norm1.reshape(1, D), bif, w_bf16)


ATTN_SUM_ROWS = 16

ATTN_HEADS = 2


def _attn_block(q_ref, k_ref, v_ref, eq_ref, ek_ref, mask_ref, m_scr, acc_scr, off, cbs, masked, q0, tq, tk):
    qs = slice(q0, tq)
    lane = lax.broadcasted_iota(jnp.int32, (1, LANES), 1)
    if masked:
        causal_bias = mask_ref[:, 0:tq - q0]
    scores = []
    for u in range(ATTN_HEADS):
        hl = slice(u * LANES, (u + 1) * LANES)
        q = q_ref[0, qs, hl]
        k = k_ref[0, :, hl]
        for c in range(2):
            sel = (lane < DA_HD) if c == 0 else (lane >= DA_HD)
            s = lax.dot_general(jnp.where(sel, k, ek_ref[u]), jnp.where(sel, q, eq_ref[u, qs]),
                                (((1,), (1,)), ((), ())), preferred_element_type=F32)
            scores.append(s + causal_bias if masked else s)
    for u in range(ATTN_HEADS):
        vt = jnp.concatenate([v_ref[0, :, u * LANES:(u + 1) * LANES].T,
                              jnp.ones((ATTN_SUM_ROWS, tk), BF16)], axis=0)
        for c in range(2):
            n = 2 * u + c
            s = scores[n]
            m_prev = m_scr[n, :, qs]
            m_new = jnp.maximum(m_prev, jnp.max(s, axis=0, keepdims=True) - cbs[u])
            alpha = jnp.exp(m_prev - m_new)
            p = jnp.exp((s - (m_new + cbs[u])).astype(BF16))
            acc_scr[n, :, qs] = alpha * acc_scr[n, :, qs] + jnp.dot(vt, p, preferred_element_type=F32)
            m_scr[n, :, qs] = m_new


def _attn_kernel(it_ref, jt_ref, slope_ref, q_ref, k_ref, v_ref, eq_ref, ek_ref, mask_ref, lam_ref, dn_ref, o_ref,
                 m_scr, acc_scr, *, tq, tk):
    hp = pl.program_id(1)
    step = pl.program_id(2)
    i = it_ref[step]
    j = jt_ref[step]
    ratio = tq // tk

    @pl.when(j == 0)
    def _():
        m_scr[...] = jnp.full(m_scr.shape, NEG, F32)
        acc_scr[...] = jnp.zeros(acc_scr.shape, F32)

    off = i * tq - j * tk
    cbs = [slope_ref[hp * ATTN_HEADS + u] * off.astype(F32) for u in range(ATTN_HEADS)]
    args = (q_ref, k_ref, v_ref, eq_ref, ek_ref, mask_ref, m_scr, acc_scr, off, cbs)

    @pl.when(j < i * ratio)
    def _():
        _attn_block(*args, masked=False, q0=0, tq=tq, tk=tk)

    for r in range(ratio):
        @pl.when(j == i * ratio + r)
        def _():
            _attn_block(*args, masked=True, q0=r * tk, tq=tq, tk=tk)

    @pl.when(j == (i + 1) * ratio - 1)
    def _():
        lv = lam_ref[...]
        lam = (jnp.exp(jnp.sum(lv[0:1] * lv[1:2], axis=-1, keepdims=True))
               - jnp.exp(jnp.sum(lv[2:3] * lv[3:4], axis=-1, keepdims=True)) + LAMBDA_INIT)
        vd = 2 * DA_HD
        for u in range(ATTN_HEADS):
            a0, a1 = acc_scr[2 * u], acc_scr[2 * u + 1]
            o = a0[0:vd] / a0[vd:vd + 1] - lam * (a1[0:vd] / a1[vd:vd + 1])
            o = o * lax.rsqrt(jnp.mean(o * o, axis=0, keepdims=True) + EPS)
            o_ref[0, :, u * LANES:(u + 1) * LANES] = (o.T * dn_ref[...] * (1.0 - LAMBDA_INIT)).astype(BF16)


def _alibi_columns(n, slopes, q_side):
    assert n <= 256 * 256 and all(math.log2(s).is_integer() for s in slopes)
    pos = np.arange(n)
    lo, hi = (pos % 256).astype(np.float64), (pos // 256 * 256).astype(np.float64)
    out = np.zeros((len(slopes), n, LANES), np.float64)
    for h, s in enumerate(slopes):
        cols = (-s * lo, -s * hi, np.ones(n), np.ones(n)) if q_side else (np.ones(n), np.ones(n), s * lo, s * hi)
        for base in (0, DA_HD):
            for c, v in enumerate(cols):
                out[h, :, base + c] = v
    return jnp.asarray(out, dtype=BF16)


def _attn(p, lamv, diff_norm, tq, tk):
    B, S, _ = p.shape
    nq, ratio = S // tq, tq // tk
    slopes = [2.0 ** (-8.0 * (h + 1) / DA_HEADS) for h in range(DA_HEADS)]
    steps = [(i, j) for i in range(nq) for j in range((i + 1) * ratio)]
    it = jnp.asarray([s[0] for s in steps], jnp.int32)
    jt = jnp.asarray([s[1] for s in steps], jnp.int32)
    hw = ATTN_HEADS * LANES
    qb, kb, vb = COL_DAQ // hw, COL_DAK // hw, COL_DAV // hw
    grid_spec = pltpu.PrefetchScalarGridSpec(
        num_scalar_prefetch=3,
        grid=(B, DA_HEADS // ATTN_HEADS, len(steps)),
        in_specs=[pl.BlockSpec((1, tq, hw), lambda b, h, s, it, jt, sl: (b, it[s], qb + h)),
                  pl.BlockSpec((1, tk, hw), lambda b, h, s, it, jt, sl: (b, jt[s], kb + h)),
                  pl.BlockSpec((1, tk, hw), lambda b, h, s, it, jt, sl: (b, jt[s], vb + h)),
                  pl.BlockSpec((ATTN_HEADS, tq, LANES), lambda b, h, s, it, jt, sl: (h, 0, 0)),
                  pl.BlockSpec((ATTN_HEADS, tk, LANES), lambda b, h, s, it, jt, sl: (h, 0, 0)),
                  pl.BlockSpec((tk, tq), lambda b, h, s, it, jt, sl: (0, 0)),
                  pl.BlockSpec((4, DA_HD), lambda b, h, s, it, jt, sl: (0, 0)),
                  pl.BlockSpec((1, 2 * DA_HD), lambda b, h, s, it, jt, sl: (0, 0))],
        out_specs=pl.BlockSpec((1, tq, hw), lambda b, h, s, it, jt, sl: (b, it[s], h)),
        scratch_shapes=[pltpu.VMEM((2 * ATTN_HEADS, 1, tq), F32),
                        pltpu.VMEM((2 * ATTN_HEADS, 2 * DA_HD + ATTN_SUM_ROWS, tq), F32)],
    )
    return pl.pallas_call(
        functools.partial(_attn_kernel, tq=tq, tk=tk),
        grid_spec=grid_spec,
        out_shape=jax.ShapeDtypeStruct((B, S, DA_WIDTH), BF16),
        compiler_params=_cparams(("arbitrary",) * 3),
    )(it, jt, jnp.asarray(slopes, F32), p, p, p, _alibi_columns(tq, slopes, True),
      _alibi_columns(tk, slopes, False),
      jnp.asarray(np.where(np.arange(tk)[:, None] <= np.arange(tq)[None, :], 0.0, NEG), F32),
      lamv, diff_norm.reshape(1, 2 * DA_HD))


def _mlstm_kernel(qk_ref, v_ref, o_ref, g_ref, cw_ref, cb_ref, nw_ref, y_ref,
                  ext_scr, c_scr, n_scr, m_scr, *, L):
    i = pl.program_id(1)

    @pl.when(i == 0)
    def _():
        ext_scr[...] = jnp.zeros(ext_scr.shape, BF16)
        c_scr[...] = jnp.zeros(c_scr.shape, F32)
        n_scr[...] = jnp.zeros(n_scr.shape, F32)
        m_scr[...] = jnp.zeros(m_scr.shape, F32)

    raw = qk_ref[0]
    tail = ext_scr[...]
    t_out = lax.broadcasted_iota(jnp.int32, (L, L), 0)
    t_in = lax.broadcasted_iota(jnp.int32, (L, L), 1)
    h_out = lax.broadcasted_iota(jnp.int32, (CONV_TAIL, CONV_TAIL), 0)
    h_in = lax.broadcasted_iota(jnp.int32, (CONV_TAIL, CONV_TAIL), 1)
    conv = cb_ref[...] + raw.astype(F32) * cw_ref[CONV_W - 1:CONV_W, :]
    for k in range(1, CONV_W):
        shift = jnp.where(t_in == t_out - k, 1.0, 0.0).astype(BF16)
        head = jnp.where(h_in == h_out - k + CONV_TAIL, 1.0, 0.0).astype(BF16)
        xk = jnp.dot(shift, raw, preferred_element_type=F32)
        fix = jnp.dot(head, tail, preferred_element_type=F32)
        xk = jnp.concatenate([xk[:CONV_TAIL] + fix, xk[CONV_TAIL:]], axis=0)
        conv = conv + xk * cw_ref[CONV_W - 1 - k:CONV_W - k, :]
    ext_scr[...] = raw[L - CONV_TAIL:L, :]
    qkc = conv * _sigmoid(conv)

    gts = g_ref[0]
    fpre = pltpu.roll(gts, LANES - ML_HEADS, axis=1)
    lf = jnp.minimum(fpre, 0.0) - jnp.log(1.0 + jnp.exp(-jnp.abs(fpre)))
    row = lax.broadcasted_iota(jnp.int32, (L, L), 0)
    col = lax.broadcasted_iota(jnp.int32, (L, L), 1)
    causal = col <= row
    bcum = jnp.dot(causal.astype(F32), lf, precision=lax.Precision.HIGHEST, preferred_element_type=F32)
    r = gts - bcum
    rt = r.T
    m_all = m_scr[...]
    lane = lax.broadcasted_iota(jnp.int32, (1, LANES), 1)
    m_next = m_all
    v_all = v_ref[0]
    o_all = o_ref[0]
    for h in range(ML_HEADS):
        hs = slice(h * ML_HD, (h + 1) * ML_HD)
        bcol = bcum[:, h:h + 1]
        rcol = r[:, h:h + 1]
        rrow = rt[h:h + 1, :]
        g = bcum[L - 1:L, h:h + 1]
        mh = m_all[:, h:h + 1]
        dm = jnp.where(causal, bcol + rrow, NEG)
        inter = bcol + mh
        mj = jnp.maximum(inter, jnp.max(dm, axis=-1, keepdims=True))
        w_intra = jnp.exp(dm - mj)
        w_inter = jnp.exp(inter - mj)
        qh = qkc[:, hs]
        kh = qkc[:, ML_WIDTH + h * ML_HD:ML_WIDTH + (h + 1) * ML_HD] * (ML_HD ** -0.5)
        vh = v_all[:, hs]
        qb = qh.astype(BF16)
        kb = kh.astype(BF16)
        s = lax.dot_general(qb, kb, (((1,), (1,)), ((), ())), preferred_element_type=F32) * w_intra
        c_old = c_scr[h]
        n_old = n_scr[h:h + 1, :]
        num = (jnp.dot(s.astype(BF16), vh, preferred_element_type=F32)
               + lax.dot_general(qb, c_old.astype(BF16), (((1,), (1,)), ((), ())),
                                 preferred_element_type=F32) * w_inter)
        den = (jnp.sum(s, axis=-1, keepdims=True)
               + w_inter * jnp.sum(qh * n_old, axis=-1, keepdims=True))
        denom = jnp.maximum(jnp.abs(den), jnp.exp(-mj))
        ht = num / denom
        a_col = g + rcol
        m_new = jnp.maximum(g + mh, jnp.max(a_col, axis=0, keepdims=True))
        wa = jnp.exp(a_col - m_new)
        decay = jnp.exp(g + mh - m_new)
        vw_t = (vh.astype(F32) * wa).T.astype(BF16)
        c_scr[h] = decay * c_old + jnp.dot(vw_t, kb, preferred_element_type=F32)
        n_scr[h:h + 1, :] = decay * n_old + jnp.sum(kh * wa, axis=0, keepdims=True)
        m_next = jnp.where(lane == h, m_new, m_next)
        z = _sigmoid(o_all[:, hs].astype(F32)) * ht
        z = z * lax.rsqrt(jnp.mean(z * z, axis=-1, keepdims=True) + EPS) * nw_ref[...]
        y_ref[0, :, hs] = z.astype(BF16)
    m_scr[...] = m_next


def _mlstm(p, gates, conv_w, conv_b, mlstm_norm, L):
    B, S, _ = p.shape
    return pl.pallas_call(
        functools.partial(_mlstm_kernel, L=L),
        grid=(B, S // L),
        in_specs=[pl.BlockSpec((1, L, 2 * ML_WIDTH), lambda b, i: (b, i, COL_MLQK // (2 * ML_WIDTH))),
                  pl.BlockSpec((1, L, ML_WIDTH), lambda b, i: (b, i, COL_MLV // ML_WIDTH)),
                  pl.BlockSpec((1, L, ML_WIDTH), lambda b, i: (b, i, COL_MLO // ML_WIDTH)),
                  pl.BlockSpec((1, L, LANES), lambda b, i: (b, i, 0)),
                  pl.BlockSpec((CONV_W, 2 * ML_WIDTH), lambda b, i: (0, 0)),
                  pl.BlockSpec((1, 2 * ML_WIDTH), lambda b, i: (0, 0)),
                  pl.BlockSpec((1, ML_HD), lambda b, i: (0, 0))],
        out_specs=pl.BlockSpec((1, L, ML_WIDTH), lambda b, i: (b, i, 0)),
        out_shape=jax.ShapeDtypeStruct((B, S, ML_WIDTH), BF16),
        scratch_shapes=[pltpu.VMEM((CONV_TAIL, 2 * ML_WIDTH), BF16),
                        pltpu.VMEM((ML_HEADS, ML_HD, ML_HD), F32),
                        pltpu.VMEM((SUBLANES, ML_HD), F32),
                        pltpu.VMEM((1, LANES), F32)],
        compiler_params=_cparams(("arbitrary", "arbitrary")),
    )(p, p, p, gates, conv_w, conv_b.reshape(1, -1), mlstm_norm.reshape(1, ML_HD))


def _post_kernel(ya_ref, ym_ref, ga_ref, gm_ref, x_ref, mod_ref, wa_ref, wm_ref, wo_ref, n2_ref,
                 wr_ref, br_ref, x1_ref, h2_ref, ri_ref, rw_ref, cnt_ref, run_scr, *, tm):
    first = jnp.logical_and(pl.program_id(0) == 0, pl.program_id(1) == 0)

    @pl.when(first)
    def _():
        run_scr[...] = jnp.zeros(run_scr.shape, F32)

    a = jnp.dot(ya_ref[0], wa_ref[...], preferred_element_type=F32)
    m = jnp.dot(ym_ref[0], wm_ref[...], preferred_element_type=F32)
    merged = _sigmoid(ga_ref[0]).astype(F32) * a + _sigmoid(gm_ref[0]).astype(F32) * m
    o = jnp.dot(merged.astype(BF16), wo_ref[...], preferred_element_type=F32)
    x1 = x_ref[0] + mod_ref[0, 2:3, :] * o
    x1_ref[0] = x1
    h2 = x1 * lax.rsqrt(jnp.mean(x1 * x1, axis=-1, keepdims=True) + EPS) * n2_ref[...]
    h2 = h2 * (1.0 + mod_ref[0, 4:5, :]) + mod_ref[0, 3:4, :]
    _store_rows(h2_ref, (0,), h2)

    hi = h2.astype(BF16)
    lo = (h2 - hi.astype(F32)).astype(BF16)
    logits = lax.dot_general(wr_ref[...], jnp.concatenate([hi, lo, hi], axis=1), (((1,), (1,)), ((), ())),
                             preferred_element_type=F32) + br_ref[...]
    sub = lax.broadcasted_iota(jnp.int32, (ROUTE_ROWS, tm), 0)
    big = jnp.int32(4 * ROUTE_ROWS)
    gl = jnp.where(sub < N_GROUPS, logits, NEG)
    gmax = jnp.max(gl, axis=0, keepdims=True)
    gsel = jnp.min(jnp.where(gl == gmax, sub, big), axis=0, keepdims=True)
    pgrp = 1.0 / jnp.sum(jnp.exp(gl - gmax), axis=0, keepdims=True)
    first = N_GROUPS + EXPERTS_PER_GROUP * gsel
    el = jnp.where(jnp.logical_and(sub >= first, sub < first + EXPERTS_PER_GROUP), logits, NEG)
    e1 = jnp.max(el, axis=0, keepdims=True)
    i1 = jnp.min(jnp.where(el == e1, sub, big), axis=0, keepdims=True)
    el2 = jnp.where(sub == i1, NEG, el)
    e2 = jnp.max(el2, axis=0, keepdims=True)
    i2 = jnp.min(jnp.where(el2 == e2, sub, big), axis=0, keepdims=True)
    tt = jnp.exp(e2 - e1)
    w1 = pgrp / (1.0 + tt)
    w2 = pgrp * tt / (1.0 + tt)

    oh1 = jnp.where(sub == i1, 1.0, 0.0).astype(F32)
    oh2 = jnp.where(sub == i2, 1.0, 0.0).astype(F32)
    cat = jnp.concatenate([oh1, oh2], axis=0).astype(BF16)
    src = lax.broadcasted_iota(jnp.int32, (tm, tm), 0)
    dst = lax.broadcasted_iota(jnp.int32, (tm, tm), 1)
    before = jnp.where(src < dst, 1.0, 0.0).astype(BF16)
    earlier = jnp.dot(cat, before, preferred_element_type=F32)
    c1 = jnp.sum(oh1, axis=1, keepdims=True)
    c2 = jnp.sum(oh2, axis=1, keepdims=True)
    run = run_scr[...]
    rank1 = jnp.sum((earlier[:ROUTE_ROWS] + run) * oh1, axis=0, keepdims=True)
    rank2 = jnp.sum((earlier[ROUTE_ROWS:] + run + c1) * oh2, axis=0, keepdims=True)
    run_new = run + c1 + c2
    run_scr[...] = run_new
    cnt_ref[...] = jnp.broadcast_to(run_new, cnt_ref.shape)
    row = lax.broadcasted_iota(jnp.int32, (SUBLANES, tm), 0)
    ri_ref[0] = jnp.where(row == 0, i1 - N_GROUPS,
                          jnp.where(row == 1, i2 - N_GROUPS,
                                    jnp.where(row == 2, rank1.astype(jnp.int32),
                                              jnp.where(row == 3, rank2.astype(jnp.int32), 0))))
    rw_ref[0] = jnp.where(row == 0, w1, jnp.where(row == 1, w2, 0.0))


def _post(ya, ym, p, x, mod, wa, wm, wo, norm2, wr, br, tm):
    B, S, D = x.shape
    tok = lambda b, i: (b, i, 0)
    const = lambda b, i: (0, 0)
    return pl.pallas_call(
        functools.partial(_post_kernel, tm=tm),
        grid=(B, S // tm),
        in_specs=[pl.BlockSpec((1, tm, DA_WIDTH), tok),
                  pl.BlockSpec((1, tm, ML_WIDTH), tok),
                  pl.BlockSpec((1, tm, D), lambda b, i: (b, i, COL_GA // D_MODEL)),
                  pl.BlockSpec((1, tm, D), lambda b, i: (b, i, COL_GM // D_MODEL)),
                  pl.BlockSpec((1, tm, D), tok),
                  pl.BlockSpec((1, 6, D), lambda b, i: (b, 0, 0)),
                  pl.BlockSpec((DA_WIDTH, D), const),
                  pl.BlockSpec((ML_WIDTH, D), const),
                  pl.BlockSpec((D, D), const),
                  pl.BlockSpec((1, D), const),
                  pl.BlockSpec((ROUTE_ROWS, 3 * D), const),
                  pl.BlockSpec((ROUTE_ROWS, 1), const)],
        out_specs=[pl.BlockSpec((1, tm, D), tok),
                   pl.BlockSpec((1, tm * ROW_PARTS, LANES), tok),
                   pl.BlockSpec((1, SUBLANES, tm), lambda b, i: (b, 0, i)),
                   pl.BlockSpec((1, SUBLANES, tm), lambda b, i: (b, 0, i)),
                   pl.BlockSpec((ROUTE_ROWS, LANES), const)],
        out_shape=[jax.ShapeDtypeStruct((B, S, D), F32),
                   jax.ShapeDtypeStruct((B, S * ROW_PARTS, LANES), F32),
                   jax.ShapeDtypeStruct((B, SUBLANES, S), jnp.int32),
                   jax.ShapeDtypeStruct((B, SUBLANES, S), F32),
                   jax.ShapeDtypeStruct((ROUTE_ROWS, LANES), F32)],
        scratch_shapes=[pltpu.VMEM((ROUTE_ROWS, 1), F32)],
        compiler_params=_cparams(("arbitrary", "arbitrary")),
    )(ya, ym, p, p, x, mod, wa, wm, wo, norm2.reshape(1, D), wr, br)


DMA_UNROLL = 8


def _dispatch_kernel(ends_ref, pc_ref, dest_ref, h2_ref, xs_hbm, zbuf, sem, zsem, *, td):
    i = pl.program_id(0)
    tile_rows = EXPERT_ROWS * ROW_PARTS

    def zero_tile(first_row):
        start = pl.multiple_of(first_row * ROW_PARTS, tile_rows)
        cp = pltpu.make_async_copy(zbuf, xs_hbm.at[pl.ds(start, tile_rows)], zsem)
        cp.start()
        cp.wait()

    @pl.when(i == 0)
    def _():
        zbuf[...] = jnp.zeros(zbuf.shape, zbuf.dtype)
        for e in range(N_EXPERTS):
            @pl.when(pc_ref[e] > 0)
            def _():
                zero_tile(ends_ref[e] - EXPERT_ROWS)

        def fill(tile, carry):
            zero_tile(tile * EXPERT_ROWS)
            return carry

        lax.fori_loop(ends_ref[N_EXPERTS - 1] // EXPERT_ROWS, xs_hbm.shape[0] // tile_rows, fill, 0)

    def issue(t, carry):
        src = h2_ref.at[pl.ds(pl.multiple_of(t * ROW_PARTS, ROW_PARTS), ROW_PARTS)]
        for s in range(2):
            dst = pl.multiple_of(dest_ref[2 * t + s] * ROW_PARTS, ROW_PARTS)
            pltpu.make_async_copy(src, xs_hbm.at[pl.ds(dst, ROW_PARTS)], sem).start(priority=s)
        return carry

    lax.fori_loop(0, td, issue, 0, unroll=DMA_UNROLL)
    for s in range(2):
        pltpu.make_async_copy(h2_ref, xs_hbm.at[pl.ds(0, td * ROW_PARTS)], sem).wait()


def _dispatch(ends, pc, dest_flat, h2, n_rows, td):
    T = h2.shape[0] // ROW_PARTS
    grid_spec = pltpu.PrefetchScalarGridSpec(
        num_scalar_prefetch=2,
        grid=(T // td,),
        in_specs=[pl.BlockSpec((2 * td,), lambda i, e, c: (i,), memory_space=pltpu.SMEM),
                  pl.BlockSpec((td * ROW_PARTS, LANES), lambda i, e, c: (i, 0))],
        out_specs=pl.BlockSpec(memory_space=pl.ANY),
        scratch_shapes=[pltpu.VMEM((EXPERT_ROWS * ROW_PARTS, LANES), h2.dtype),
                        pltpu.SemaphoreType.DMA(()), pltpu.SemaphoreType.DMA(())],
    )
    return pl.pallas_call(
        functools.partial(_dispatch_kernel, td=td),
        grid_spec=grid_spec,
        out_shape=jax.ShapeDtypeStruct((n_rows * ROW_PARTS, LANES), h2.dtype),
        compiler_params=_cparams(("arbitrary",)),
    )(ends, pc, dest_flat, h2)


def _experts_kernel(te_ref, nt_ref, xs_ref, w1_ref, w3_ref, w2_ref, ys_ref):
    i = pl.program_id(0)

    @pl.when(i < nt_ref[0])
    def _():
        x = _load_rows(xs_ref, (), EXPERT_ROWS).astype(BF16)
        a = jnp.dot(x, w1_ref[0, 0, 0].astype(BF16), preferred_element_type=F32)
        b = jnp.dot(x, w3_ref[0, 0, 0].astype(BF16), preferred_element_type=F32)
        hid = (a * _sigmoid(a) * b).astype(BF16)
        _store_rows(ys_ref, (), jnp.dot(hid, w2_ref[0, 0, 0].astype(BF16), preferred_element_type=F32))

    @pl.when(i >= nt_ref[0])
    def _():
        ys_ref[...] = jnp.zeros(ys_ref.shape, ys_ref.dtype)


def _experts(tile_e, n_tiles, xs, w1, w3, w2):
    n_rows = xs.shape[0] // ROW_PARTS
    D = D_MODEL
    nt = n_rows // EXPERT_ROWS
    tile = (EXPERT_ROWS * ROW_PARTS, LANES)
    rows = lambda i, te, n: (jnp.minimum(i, jnp.maximum(n[0] - 1, 0)), 0)
    wsel = lambda i, te, n: (0, te[i] // EXPERTS_PER_GROUP, te[i] % EXPERTS_PER_GROUP, 0, 0)
    grid_spec = pltpu.PrefetchScalarGridSpec(
        num_scalar_prefetch=2,
        grid=(nt,),
        in_specs=[pl.BlockSpec(tile, rows),
                  pl.BlockSpec((1, 1, 1, D, D_EXPERT), wsel),
                  pl.BlockSpec((1, 1, 1, D, D_EXPERT), wsel),
                  pl.BlockSpec((1, 1, 1, D_EXPERT, D), wsel)],
        out_specs=pl.BlockSpec(tile, lambda i, te, n: (i, 0)),
    )
    return pl.pallas_call(
        _experts_kernel,
        grid_spec=grid_spec,
        out_shape=jax.ShapeDtypeStruct((n_rows * ROW_PARTS, LANES), F32),
        compiler_params=_cparams(("arbitrary",)),
    )(tile_e, n_tiles, xs, w1, w3, w2)


def _combine_kernel(dcur_ref, dnxt_ref, ys_hbm, x1_ref, rw_ref, mod_ref, nf_ref, o_ref, ybuf, sem, *, tc, nsteps):
    g = pl.program_id(0)
    slot = g % 2

    def issue(dest_ref, sl, t):
        dst = pl.ds(pl.multiple_of(t * ROW_PARTS, ROW_PARTS), ROW_PARTS)
        for s in range(2):
            src = pl.multiple_of(dest_ref[2 * t + s] * ROW_PARTS, ROW_PARTS)
            pltpu.make_async_copy(ys_hbm.at[pl.ds(src, ROW_PARTS)], ybuf.at[sl, s, dst],
                                  sem.at[sl]).start(priority=s)

    def drain(sl):
        for s in range(2):
            pltpu.make_async_copy(ys_hbm.at[pl.ds(0, tc * ROW_PARTS)], ybuf.at[sl, s], sem.at[sl]).wait()

    @pl.when(g == 0)
    def _():
        lax.fori_loop(0, tc, lambda t, c: (issue(dcur_ref, 0, t), c)[1], 0, unroll=DMA_UNROLL)

    drain(slot)
    for t in range(tc):
        issue(dnxt_ref, 1 - slot, t)

    rw = jnp.concatenate([rw_ref[0], jnp.zeros((LANES - SUBLANES, tc), F32)], axis=0).T
    y = rw[:, 0:1] * _load_rows(ybuf, (slot, 0), tc) + rw[:, 1:2] * _load_rows(ybuf, (slot, 1), tc)
    xo = x1_ref[0] + mod_ref[0, 5:6, :] * y
    o_ref[0] = xo * lax.rsqrt(jnp.mean(xo * xo, axis=-1, keepdims=True) + EPS) * nf_ref[...]

    @pl.when(g == nsteps - 1)
    def _():
        drain(1 - slot)


def _combine(dest_flat, ys, x1, rw, mod, norm_f, tc):
    B, S, D = x1.shape
    n = S // tc
    nsteps = B * n
    tok = lambda g: (g // n, g % n, 0)
    return pl.pallas_call(
        functools.partial(_combine_kernel, tc=tc, nsteps=nsteps),
        grid=(nsteps,),
        in_specs=[pl.BlockSpec((2 * tc,), lambda g: (g,), memory_space=pltpu.SMEM),
                  pl.BlockSpec((2 * tc,), lambda g: (jnp.minimum(g + 1, nsteps - 1),), memory_space=pltpu.SMEM),
                  pl.BlockSpec(memory_space=pl.ANY),
                  pl.BlockSpec((1, tc, D), tok),
                  pl.BlockSpec((1, SUBLANES, tc), lambda g: (g // n, 0, g % n)),
                  pl.BlockSpec((1, 6, D), lambda g: (g // n, 0, 0)),
                  pl.BlockSpec((1, D), lambda g: (0, 0))],
        out_specs=pl.BlockSpec((1, tc, D), tok),
        out_shape=jax.ShapeDtypeStruct((B, S, D), F32),
        scratch_shapes=[pltpu.VMEM((2, 2, tc * ROW_PARTS, LANES), F32), pltpu.SemaphoreType.DMA((2,))],
        compiler_params=_cparams(("arbitrary",)),
    )(dest_flat, dest_flat, ys, x1, rw, mod, norm_f.reshape(1, D))


def _pick(n, pref):
    t = min(n, pref)
    assert n % t == 0, (n, pref)
    return t


def kernel(x, c, w_ada, b_ada, norm1, w_in, b_if, conv_w, conv_b, lam_q1, lam_k1, lam_q2, lam_k2,
           diff_norm, mlstm_norm, w_br_a, w_br_m, w_out, norm2, w_rg, b_rg, w_re, b_re,
           w_e1, w_e3, w_e2, norm_f):
    B, S, D = x.shape
    assert D == D_MODEL and w_ada.shape[0] == 1
    T = B * S
    l = 0

    mod = _ada(c, w_ada, b_ada).reshape(B, 6, D)

    w_perm = _permute_w_in(w_in)
    bif = jnp.concatenate([b_if[l], jnp.zeros((LANES - 2 * ML_HEADS,), F32)]).reshape(1, LANES)

    p, gates = _proj(x, mod, norm1[l], bif, w_perm, _pick(S, PROJ_TM))

    lamv = jnp.stack([lam_q1[l], lam_k1[l], lam_q2[l], lam_k2[l]])
    ya = _attn(p, lamv, diff_norm[l], _pick(S, ATTN_TQ), _pick(S, ATTN_TK))
    ym = _mlstm(p, gates, conv_w[l], conv_b[l], mlstm_norm[l], _pick(S, MLSTM_L))

    pad_rows = ROUTE_ROWS - N_GROUPS - N_EXPERTS
    wr = jnp.concatenate([w_rg[l], w_re[l], jnp.zeros((D, pad_rows), F32)], axis=1).T
    wr_hi = wr.astype(BF16)
    wr_lo = (wr - wr_hi.astype(F32)).astype(BF16)
    wr = jnp.concatenate([wr_hi, wr_hi, wr_lo], axis=1)
    br = jnp.concatenate([b_rg[l], b_re[l], jnp.zeros((pad_rows,), F32)]).reshape(ROUTE_ROWS, 1)
    x1, h2, ri, rw, cnt = _post(ya, ym, p, x, mod, w_br_a[l].astype(BF16), w_br_m[l].astype(BF16),
                                w_out[l].astype(BF16), norm2[l], wr, br, _pick(S, POST_TM))

    counts = cnt[N_GROUPS:N_GROUPS + N_EXPERTS, 0].astype(jnp.int32)
    pc = ((counts + EXPERT_ROWS - 1) // EXPERT_ROWS) * EXPERT_ROWS
    ends = jnp.cumsum(pc)
    offs = ends - pc
    eid = ri[:, 0:2, :]
    rank = ri[:, 2:4, :]
    dest = rank + jnp.sum(jnp.where(eid[..., None] == jnp.arange(N_EXPERTS), offs, 0), axis=-1)
    dest_flat = dest.transpose(0, 2, 1).reshape(2 * T).astype(jnp.int32)
    n_rows = 2 * T + N_EXPERTS * EXPERT_ROWS
    n_tiles = n_rows // EXPERT_ROWS
    tile_e = jnp.sum(jnp.arange(n_tiles)[:, None] * EXPERT_ROWS >= ends[None, :], axis=1)
    tile_e = jnp.minimum(tile_e, N_EXPERTS - 1).astype(jnp.int32)
    used_tiles = (ends[-1:] // EXPERT_ROWS).astype(jnp.int32)

    xs = _dispatch(ends.astype(jnp.int32), pc.astype(jnp.int32), dest_flat, h2.reshape(T * ROW_PARTS, LANES), n_rows,
                   _pick(T, SMEM_BLOCK_1D))
    ys = _experts(tile_e, used_tiles, xs, w_e1, w_e3, w_e2)
    return _combine(dest_flat, ys, x1, rw, mod, norm_f, _pick(S, SMEM_BLOCK_1D // 2))
```
